```python
import math
import functools
import jax
import jax.numpy as jnp
from jax import lax
import numpy as np

D_MODEL = 1024
BATCH = 4
SEQ = 4096
DEPTH = 4
DEC_BATCH = 32
DEC_SEQ = 8
PAST_LEN = 8192
PAGE_SIZE = 128

N_MIXERS = 4
W_GRP = D_MODEL // N_MIXERS
D_IN = 9 * W_GRP
LRU_HEADS = 4
LRU_HD = W_GRP // LRU_HEADS
LRU_CONV = 4
LRU_C = 8.0
SCONV_W = 3
S5_CH = 16
S5_GROUPS = W_GRP // S5_CH
S5_STATE = 64
ATT_HEADS = 4
HEAD_DIM = W_GRP // ATT_HEADS
PATTERNS = ((128, 1), (512, 4), (2048, 16))
MAX_WINDOW = 2048
Q_BLK = 128
REL_BUCKETS = 32
REL_MAX_DIST = 2048
N_GROUPS = 4
EXP_PER_GROUP = 8
N_EXPERTS = N_GROUPS * EXP_PER_GROUP
TOP_K = 2
D_EXPERT = 512
MOE_BLK = 128
EPS = 1e-6
NEG = -1e30

kernel_name = 'hybrid_parallel_groups_decode_step'


def _rms(x, g):
    xf = x.astype(jnp.float32)
    return xf * lax.rsqrt(jnp.mean(xf * xf, axis=-1, keepdims=True) + EPS) * g.astype(jnp.float32)


def _causal_conv(x, buf, w):
    K = w.shape[0]
    L = x.shape[1]
    xp = jnp.concatenate([buf.astype(x.dtype), x], axis=1)
    y = xp[:, 0:L] * w[0]
    for j in range(1, K):
        y = y + xp[:, j:j + L] * w[j]
    return y, xp[:, xp.shape[1] - (K - 1):]


def _lin_combine(e1, e2):
    a1, b1 = e1
    a2, b2 = e2
    return a1 * a2, a2 * b1 + b2


def _cplx_combine(e1, e2):
    ar1, ai1, br1, bi1 = e1
    ar2, ai2, br2, bi2 = e2
    return (ar1 * ar2 - ai1 * ai2, ar1 * ai2 + ai1 * ar2,
            ar2 * br1 - ai2 * bi1 + br2, ar2 * bi1 + ai2 * br1 + bi2)


def _rglru(x, h0, wa, ba, wx, bx, lam):
    f32 = jnp.float32
    Bn, L, W = x.shape
    xf = x.astype(f32)
    xh = xf.reshape(Bn, L, LRU_HEADS, LRU_HD)
    r = jax.nn.sigmoid(jnp.einsum('blhi,hij->blhj', xh, wa.astype(f32)).reshape(Bn, L, W) + ba.astype(f32))
    ig = jax.nn.sigmoid(jnp.einsum('blhi,hij->blhj', xh, wx.astype(f32)).reshape(Bn, L, W) + bx.astype(f32))
    log_a = -LRU_C * r * jax.nn.softplus(-lam.astype(f32))
    a = jnp.exp(log_a)
    b = jnp.sqrt(-jnp.expm1(2.0 * log_a)) * (ig * xf)
    b = b.at[:, 0].add(a[:, 0] * h0.astype(f32))
    _, h = lax.associative_scan(_lin_combine, (a, b), axis=1)
    return h, h[:, -1]


def _s5(u, h0r, h0i, p):
    f32 = jnp.float32
    Bn, L, W = u.shape
    uf = u.astype(f32)
    ug = uf.reshape(Bn, L, S5_GROUPS, S5_CH)
    step = jnp.exp(p['s5_log_dt'].astype(f32))[:, None]
    ar = p['s5_a_re'].astype(f32)
    ai = p['s5_a_im'].astype(f32)
    mag = jnp.exp(ar * step)
    ang = ai * step
    abr = mag * jnp.cos(ang)
    abi = mag * jnp.sin(ang)
    den = ar * ar + ai * ai
    zr = ((abr - 1.0) * ar + abi * ai) / den
    zi = (abi * ar - (abr - 1.0) * ai) / den
    b_re = p['s5_b_re'].astype(f32)
    b_im = p['s5_b_im'].astype(f32)
    bbr = zr[..., None] * b_re - zi[..., None] * b_im
    bbi = zr[..., None] * b_im + zi[..., None] * b_re
    bur = jnp.einsum('blgc,gpc->blgp', ug, bbr)
    bui = jnp.einsum('blgc,gpc->blgp', ug, bbi)
    h0r = h0r.astype(f32)
    h0i = h0i.astype(f32)
    bur = bur.at[:, 0].add(abr * h0r - abi * h0i)
    bui = bui.at[:, 0].add(abr * h0i + abi * h0r)
    are = jnp.broadcast_to(abr, bur.shape)
    aim = jnp.broadcast_to(abi, bur.shape)
    _, _, hr, hi = lax.associative_scan(_cplx_combine, (are, aim, bur, bui), axis=1)
    y = (jnp.einsum('blgp,gcp->blgc', hr, p['s5_c_re'].astype(f32))
         - jnp.einsum('blgp,gcp->blgc', hi, p['s5_c_im'].astype(f32))).reshape(Bn, L, W) + p['s5_d'].astype(f32) * uf
    g = jax.nn.gelu(y) @ p['s5_glu_w'].astype(f32) + p['s5_glu_b'].astype(f32)
    out = g[..., :W] * jax.nn.sigmoid(g[..., W:])
    return out, hr[:, -1], hi[:, -1]


def _t5_bucket(n):
    n = np.asarray(n).astype(np.int32)
    max_exact = REL_BUCKETS // 2
    nf = np.maximum(n, 1).astype(np.float32)
    large = max_exact + (np.log(nf / max_exact) / np.log(REL_MAX_DIST / max_exact)
                         * (REL_BUCKETS - max_exact)).astype(np.int32)
    large = np.minimum(large, REL_BUCKETS - 1)
    return np.where(n < max_exact, n, large).astype(np.int32)


def _head_prep(q, k, v, qn, kn):
    Bn, L, _ = q.shape
    q = _rms(q.reshape(Bn, L, ATT_HEADS, HEAD_DIM), qn) * (HEAD_DIM ** -0.5)
    k = _rms(k.reshape(Bn, L, ATT_HEADS, HEAD_DIM), kn)
    v = v.reshape(Bn, L, ATT_HEADS, HEAD_DIM).astype(jnp.float32)
    return q, k, v


def _merge_patterns(outs, lses):
    wts = jax.nn.softmax(jnp.stack(lses, axis=0), axis=0)
    return jnp.sum(wts[..., None] * jnp.stack(outs, axis=0), axis=0)


def _dilated_prompt_one(q, k, v, win, dil, rel_bias):
    Bn, L, H, E = q.shape
    span = win // dil
    md = L // dil
    nb = -(-md // Q_BLK)
    mp = nb * Q_BLK

    def to_res(t, front):
        t = t.reshape(Bn, md, dil, H, E).transpose(0, 2, 1, 3, 4)
        return jnp.pad(t, ((0, 0), (0, 0), (front, mp - md), (0, 0), (0, 0)))

    qb = to_res(q, 0).reshape(Bn, dil, nb, Q_BLK, H, E)
    kb = to_res(k, Q_BLK).reshape(Bn, dil, nb + 1, Q_BLK, H, E)
    vb = to_res(v, Q_BLK).reshape(Bn, dil, nb + 1, Q_BLK, H, E)
    kw = jnp.concatenate([kb[:, :, :-1], kb[:, :, 1:]], axis=3)
    vw = jnp.concatenate([vb[:, :, :-1], vb[:, :, 1:]], axis=3)
    i = np.arange(Q_BLK)[:, None]
    j = np.arange(2 * Q_BLK)[None, :]
    rel = Q_BLK + i - j
    key_m = np.arange(nb)[:, None, None] * Q_BLK - Q_BLK + j[None]
    mask = ((rel >= 0) & (rel <= span))[None] & (key_m >= 0)
    bias = rel_bias.astype(jnp.float32)[_t5_bucket(np.clip(rel, 0, None) * dil)].transpose(2, 0, 1)
    s = jnp.einsum('brnqhe,brnkhe->brnhqk', qb, kw) + bias
    s = jnp.where(mask[None, None, :, None], s, NEG)
    m = jnp.max(s, axis=-1, keepdims=True)
    pr = jnp.exp(s - m)
    den = jnp.sum(pr, axis=-1)
    o = jnp.einsum('brnhqk,brnkhe->brnqhe', pr, vw) / jnp.swapaxes(den, -1, -2)[..., None]
    lse = jnp.swapaxes(m[..., 0] + jnp.log(den), -1, -2)
    o = o.reshape(Bn, dil, mp, H, E)[:, :, :md].transpose(0, 2, 1, 3, 4).reshape(Bn, L, H, E)
    lse = lse.reshape(Bn, dil, mp, H)[:, :, :md].transpose(0, 2, 1, 3).reshape(Bn, L, H)
    return o, lse


def _attn_prompt(q, k, v, qn, kn, rel_bias):
    Bn, L, W = q.shape
    q, k, v = _head_prep(q, k, v, qn, kn)
    outs, lses = [], []
    for win, dil in PATTERNS:
        o, lse = _dilated_prompt_one(q, k, v, win, dil, rel_bias)
        outs.append(o)
        lses.append(lse)
    return _merge_patterns(outs, lses).reshape(Bn, L, W), k, v


def _dilated_sample_one(q, kall, vall, wb, win, dil, rel_bias):
    S = q.shape[1]
    span = win // dil
    kk = np.arange(span + 1)
    idx = wb + np.arange(S)[:, None] - kk[None, :] * dil
    valid = idx >= 0
    idxc = np.maximum(idx, 0)
    kg = kall[:, idxc]
    vg = vall[:, idxc]
    bias = rel_bias.astype(jnp.float32)[_t5_bucket(kk * dil)].T[:, None, :]
    s = jnp.einsum('bshe,bskhe->bhsk', q, kg) + bias
    s = jnp.where(valid[None, None], s, NEG)
    m = jnp.max(s, axis=-1, keepdims=True)
    pr = jnp.exp(s - m)
    den = jnp.sum(pr, axis=-1)
    o = jnp.einsum('bhsk,bskhe->bshe', pr, vg) / jnp.transpose(den, (0, 2, 1))[..., None]
    lse = jnp.transpose(m[..., 0] + jnp.log(den), (0, 2, 1))
    return o, lse


def _attn_sample(q, k, v, qn, kn, rel_bias, buf_k, buf_v):
    Bn, S, W = q.shape
    q, k, v = _head_prep(q, k, v, qn, kn)
    wb = buf_k.shape[1]
    kall = jnp.concatenate([buf_k.astype(jnp.float32), k], axis=1)
    vall = jnp.concatenate([buf_v.astype(jnp.float32), v], axis=1)
    outs, lses = [], []
    for win, dil in PATTERNS:
        o, lse = _dilated_sample_one(q, kall, vall, wb, win, dil, rel_bias)
        outs.append(o)
        lses.append(lse)
    return _merge_patterns(outs, lses).reshape(Bn, S, W), k, v


def _moe(h, p):
    f32 = jnp.float32
    Bn, L, D = h.shape
    T = Bn * L
    hf = h.reshape(T, D)
    g_logits = hf.astype(f32) @ p['router_g_w'].astype(f32) + p['router_g_b'].astype(f32)
    g_sel = jnp.argmax(g_logits, axis=-1).astype(jnp.int32)
    g_prob = jnp.take_along_axis(jax.nn.softmax(g_logits, axis=-1), g_sel[:, None], axis=1)
    e_logits = (hf.astype(f32) @ p['router_e_w'].astype(f32) + p['router_e_b'].astype(f32)).reshape(T, N_GROUPS, EXP_PER_GROUP)
    e_in = jnp.take_along_axis(e_logits, g_sel[:, None, None], axis=1)[:, 0]
    top_v, top_i = lax.top_k(e_in, TOP_K)
    gates = jax.nn.softmax(top_v, axis=-1) * g_prob
    ex = (g_sel[:, None] * EXP_PER_GROUP + top_i).reshape(-1).astype(jnp.int32)
    n_assign = T * TOP_K
    tok = jnp.repeat(jnp.arange(T, dtype=jnp.int32), TOP_K)
    wts = gates.reshape(-1)
    onehot = (ex[:, None] == jnp.arange(N_EXPERTS, dtype=jnp.int32)[None, :]).astype(jnp.int32)
    rank = jnp.take_along_axis(jnp.cumsum(onehot, axis=0), ex[:, None], axis=1)[:, 0] - 1
    counts = jnp.sum(onehot, axis=0)
    padded = (counts + MOE_BLK - 1) // MOE_BLK * MOE_BLK
    ends = jnp.cumsum(padded)
    dest = (ends - padded)[ex] + rank
    n_blk = -(-n_assign // MOE_BLK) + N_EXPERTS
    slot_tok = jnp.full((n_blk * MOE_BLK,), T, jnp.int32).at[dest].set(tok)
    slot_w = jnp.zeros((n_blk * MOE_BLK,), f32).at[dest].set(wts)
    blk_e = jnp.minimum(jnp.searchsorted(ends, jnp.arange(n_blk, dtype=jnp.int32) * MOE_BLK, side='right'), N_EXPERTS - 1)
    h_pad = jnp.concatenate([hf, jnp.zeros((1, D), hf.dtype)], axis=0)
    wg, wu, wd = p['moe_w_gate'], p['moe_w_up'], p['moe_w_down']

    def expert_block(args):
        ids, e = args
        xb = h_pad[ids]
        return (jax.nn.silu(xb @ wg[e]) * (xb @ wu[e])) @ wd[e]

    out = lax.map(expert_block, (slot_tok.reshape(n_blk, MOE_BLK), blk_e))
    y = jnp.zeros((T + 1, D), f32).at[slot_tok].add(out.reshape(-1, D).astype(f32) * slot_w[:, None])
    return y[:T].reshape(Bn, L, D)


def _layer(x, c, st, p, attn_fn):
    f32 = jnp.float32
    Bn, L, D = x.shape
    dt = x.dtype
    mod = jax.nn.silu(c.astype(f32)) @ p['mod_w'].astype(f32) + p['mod_b'].astype(f32)
    sh1, sc1, g1, sh2, sc2, g2 = [t[:, None, :] for t in jnp.split(mod, 6, axis=-1)]
    h = (_rms(x, p['norm_mix']) * (1.0 + sc1) + sh1).astype(dt)
    z = h @ p['w_in']
    xa, ga, gb, gc, xb, uc, q, k, v = jnp.split(z, 9, axis=-1)
    xa_conv, lru_conv_new = _causal_conv(xa, st['lru_conv'], p['lru_conv_w'])
    ha, lru_h_new = _rglru(xa_conv + p['lru_conv_b'], st['lru_h'], p['lru_wa'], p['lru_ba'],
                           p['lru_wx'], p['lru_bx'], p['lru_lambda'])
    out_a = ha * jax.nn.gelu(ga.astype(f32))
    yb, sconv_new = _causal_conv(gc * xb, st['sconv'], p['sconv_w'])
    out_b = (gb * yb).astype(f32)
    out_c, s5_re_new, s5_im_new = _s5(uc, st['s5_re'], st['s5_im'], p)
    out_d, k_rows, v_rows = attn_fn(q, k, v, p['q_norm'], p['k_norm'])
    mix = jnp.concatenate([out_a, out_b, out_c, out_d], axis=-1).reshape(Bn, L, N_MIXERS, W_GRP)
    mix = (mix * lax.rsqrt(jnp.mean(mix * mix, axis=-1, keepdims=True) + EPS)).reshape(Bn, L, D) * p['out_norm'].astype(f32)
    x = x + (g1 * (mix.astype(dt) @ p['w_out'])).astype(dt)
    h2 = (_rms(x, p['norm_ffn']) * (1.0 + sc2) + sh2).astype(dt)
    x = x + (g2 * _moe(h2, p)).astype(dt)
    new_st = {'lru_h': lru_h_new, 'lru_conv': lru_conv_new, 'sconv': sconv_new,
              's5_re': s5_re_new, 's5_im': s5_im_new, 'win_k': k_rows, 'win_v': v_rows}
    return x, new_st


def setup_inputs(seed: int = 0) -> dict:
    key = jax.random.key(seed)
    ks = iter(jax.random.split(key, 64))

    def nrm(shape, scale):
        return jax.random.normal(next(ks), shape, jnp.float32) * scale

    def gain(shape):
        return 1.0 + nrm(shape, 0.01)

    wb = min(MAX_WINDOW, PAST_LEN)
    lam_u = jax.random.uniform(next(ks), (DEPTH, W_GRP), jnp.float32, 0.9, 0.999)
    lam_s = lam_u ** (1.0 / LRU_C)
    n_idx = jnp.arange(S5_STATE, dtype=jnp.float32)
    return {
        'x_prompt': nrm((BATCH, SEQ, D_MODEL), 1.0),
        'x_sample': nrm((DEC_BATCH, DEC_SEQ, D_MODEL), 1.0),
        'c_prompt': nrm((BATCH, D_MODEL), 1.0),
        'c_sample': nrm((DEC_BATCH, D_MODEL), 1.0),
        'state_lru_h': nrm((DEPTH, DEC_BATCH, W_GRP), 0.5),
        'state_lru_conv': nrm((DEPTH, DEC_BATCH, LRU_CONV - 1, W_GRP), 1.0),
        'state_sconv': nrm((DEPTH, DEC_BATCH, SCONV_W - 1, W_GRP), 1.0),
        'state_s5_re': nrm((DEPTH, DEC_BATCH, S5_GROUPS, S5_STATE), 0.3),
        'state_s5_im': nrm((DEPTH, DEC_BATCH, S5_GROUPS, S5_STATE), 0.3),
        'cache_win_k': nrm((DEPTH, DEC_BATCH, wb, ATT_HEADS, HEAD_DIM), 1.0),
        'cache_win_v': nrm((DEPTH, DEC_BATCH, wb, ATT_HEADS, HEAD_DIM), 1.0),
        'rel_bias': nrm((REL_BUCKETS, ATT_HEADS), 0.5),
        'mod_w': nrm((DEPTH, D_MODEL, 6 * D_MODEL), 0.5 * D_MODEL ** -0.5),
        'mod_b': nrm((DEPTH, 6 * D_MODEL), 0.02),
        'norm_mix': gain((DEPTH, D_MODEL)),
        'norm_ffn': gain((DEPTH, D_MODEL)),
        'w_in': nrm((DEPTH, D_MODEL, D_IN), D_MODEL ** -0.5),
        'lru_conv_w': nrm((DEPTH, LRU_CONV, W_GRP), LRU_CONV ** -0.5),
        'lru_conv_b': nrm((DEPTH, W_GRP), 0.02),
        'lru_wa': nrm((DEPTH, LRU_HEADS, LRU_HD, LRU_HD), LRU_HD ** -0.5),
        'lru_ba': nrm((DEPTH, W_GRP), 0.02),
        'lru_wx': nrm((DEPTH, LRU_HEADS, LRU_HD, LRU_HD), LRU_HD ** -0.5),
        'lru_bx': nrm((DEPTH, W_GRP), 0.02),
        'lru_lambda': jnp.log(lam_s) - jnp.log1p(-lam_s),
        'sconv_w': nrm((DEPTH, SCONV_W, W_GRP), SCONV_W ** -0.5),
        's5_log_dt': jax.random.uniform(next(ks), (DEPTH, S5_GROUPS), jnp.float32, math.log(0.001), math.log(0.1)),
        's5_a_re': -0.5 + nrm((DEPTH, S5_GROUPS, S5_STATE), 0.01),
        's5_a_im': jnp.pi * n_idx + nrm((DEPTH, S5_GROUPS, S5_STATE), 0.01),
        's5_b_re': nrm((DEPTH, S5_GROUPS, S5_STATE, S5_CH), S5_CH ** -0.5),
        's5_b_im': nrm((DEPTH, S5_GROUPS, S5_STATE, S5_CH), S5_CH ** -0.5),
        's5_c_re': nrm((DEPTH, S5_GROUPS, S5_CH, S5_STATE), S5_STATE ** -0.5),
        's5_c_im': nrm((DEPTH, S5_GROUPS, S5_CH, S5_STATE), S5_STATE ** -0.5),
        's5_d': nrm((DEPTH, W_GRP), 1.0),
        's5_glu_w': nrm((DEPTH, W_GRP, 2 * W_GRP), W_GRP ** -0.5),
        's5_glu_b': nrm((DEPTH, 2 * W_GRP), 0.02),
        'q_norm': gain((DEPTH, HEAD_DIM)),
        'k_norm': gain((DEPTH, HEAD_DIM)),
        'out_norm': gain((DEPTH, D_MODEL)),
        'w_out': nrm((DEPTH, D_MODEL, D_MODEL), D_MODEL ** -0.5),
        'router_g_w': nrm((DEPTH, D_MODEL, N_GROUPS), D_MODEL ** -0.5),
        'router_g_b': nrm((DEPTH, N_GROUPS), 0.01),
        'router_e_w': nrm((DEPTH, D_MODEL, N_EXPERTS), D_MODEL ** -0.5),
        'router_e_b': nrm((DEPTH, N_EXPERTS), 0.01),
        'moe_w_gate': nrm((DEPTH, N_EXPERTS, D_MODEL, D_EXPERT), D_MODEL ** -0.5),
        'moe_w_up': nrm((DEPTH, N_EXPERTS, D_MODEL, D_EXPERT), D_MODEL ** -0.5),
        'moe_w_down': nrm((DEPTH, N_EXPERTS, D_EXPERT, D_MODEL), D_EXPERT ** -0.5),
    }


def reference(x_prompt, x_sample, c_prompt, c_sample, state_lru_h, state_lru_conv, state_sconv, state_s5_re, state_s5_im,
              cache_win_k, cache_win_v, rel_bias, mod_w, mod_b, norm_mix, norm_ffn, w_in, lru_conv_w, lru_conv_b,
              lru_wa, lru_ba, lru_wx, lru_bx, lru_lambda, sconv_w, s5_log_dt, s5_a_re, s5_a_im, s5_b_re, s5_b_im,
              s5_c_re, s5_c_im, s5_d, s5_glu_w, s5_glu_b, q_norm, k_norm, out_norm, w_out, router_g_w, router_g_b,
              router_e_w, router_e_b, moe_w_gate, moe_w_up, moe_w_down):
    Bp, L, _ = x_prompt.shape
    dt = x_prompt.dtype
    wp = min(MAX_WINDOW, L)
    names = ('lru_h', 'lru_conv', 'sconv', 's5_re', 's5_im', 'win_k', 'win_v')
    acc_p = {n: [] for n in names}
    acc_s = {n: [] for n in names}
    xp, xs = x_prompt, x_sample
    for l in range(DEPTH):
        p = {'mod_w': mod_w[l], 'mod_b': mod_b[l], 'norm_mix': norm_mix[l], 'norm_ffn': norm_ffn[l], 'w_in': w_in[l],
             'lru_conv_w': lru_conv_w[l], 'lru_conv_b': lru_conv_b[l], 'lru_wa': lru_wa[l], 'lru_ba': lru_ba[l],
             'lru_wx': lru_wx[l], 'lru_bx': lru_bx[l], 'lru_lambda': lru_lambda[l], 'sconv_w': sconv_w[l],
             's5_log_dt': s5_log_dt[l], 's5_a_re': s5_a_re[l], 's5_a_im': s5_a_im[l], 's5_b_re': s5_b_re[l],
             's5_b_im': s5_b_im[l], 's5_c_re': s5_c_re[l], 's5_c_im': s5_c_im[l], 's5_d': s5_d[l],
             's5_glu_w': s5_glu_w[l], 's5_glu_b': s5_glu_b[l], 'q_norm': q_norm[l], 'k_norm': k_norm[l],
             'out_norm': out_norm[l], 'w_out': w_out[l], 'router_g_w': router_g_w[l], 'router_g_b': router_g_b[l],
             'router_e_w': router_e_w[l], 'router_e_b': router_e_b[l], 'moe_w_gate': moe_w_gate[l],
             'moe_w_up': moe_w_up[l], 'moe_w_down': moe_w_down[l]}
        zero_st = {'lru_h': jnp.zeros((Bp, W_GRP), dt),
                   'lru_conv': jnp.zeros((Bp, LRU_CONV - 1, W_GRP), dt),
                   'sconv': jnp.zeros((Bp, SCONV_W - 1, W_GRP), dt),
                   's5_re': jnp.zeros((Bp, S5_GROUPS, S5_STATE), dt),
                   's5_im': jnp.zeros((Bp, S5_GROUPS, S5_STATE), dt)}
        xp, stp = _layer(xp, c_prompt, zero_st, p, functools.partial(_attn_prompt, rel_bias=rel_bias))
        stp['win_k'] = stp['win_k'][:, L - wp:]
        stp['win_v'] = stp['win_v'][:, L - wp:]
        samp_st = {'lru_h': state_lru_h[l], 'lru_conv': state_lru_conv[l], 'sconv': state_sconv[l],
                   's5_re': state_s5_re[l], 's5_im': state_s5_im[l]}
        xs, sts = _layer(xs, c_sample, samp_st, p,
                         functools.partial(_attn_sample, rel_bias=rel_bias, buf_k=cache_win_k[l], buf_v=cache_win_v[l]))
        for n in names:
            acc_p[n].append(stp[n])
            acc_s[n].append(sts[n])
    new_p = {n: jnp.stack(acc_p[n], axis=0) for n in names}
    new_s = {n: jnp.stack(acc_s[n], axis=0) for n in names}
    return (xp, xs,
            new_p['lru_h'], new_p['lru_conv'], new_p['sconv'], new_p['s5_re'], new_p['s5_im'], new_p['win_k'], new_p['win_v'],
            new_s['lru_h'], new_s['lru_conv'], new_s['sconv'], new_s['s5_re'], new_s['s5_im'], new_s['win_k'], new_s['win_v'])
```

```python
import functools
import math

import numpy as np
import jax
import jax.numpy as jnp
from jax import lax
from jax.experimental import pallas as pl
from jax.experimental.pallas import tpu as pltpu

F32 = jnp.float32
MXU_DT = jnp.bfloat16
HIGHEST = lax.Precision.HIGHEST

D_MODEL = 1024
DEPTH = 4
W_GRP = 256
N_Z = 9
LRU_HEADS = 4
LRU_CONV = 4
LRU_C = 8.0
SCONV_W = 3
S5_CH = 16
S5_GROUPS = 16
S5_STATE = 64
S5_N = S5_GROUPS * S5_STATE
ATT_HEADS = 4
HEAD_DIM = 64
PATTERNS = ((128, 1), (512, 4), (2048, 16))
Q_BLK = 128
REL_BUCKETS = 32
REL_MAX_DIST = 2048
N_GROUPS = 4
EXP_PER_GROUP = 8
N_EXPERTS = 32
D_EXPERT = 512
EPS = 1e-6
NEG = -1e30

VMEM_LIMIT = 56 * 1024 * 1024
LANES = 128
MOE_ROWS = 256


def _cparams(sem):
    return pltpu.CompilerParams(dimension_semantics=sem, vmem_limit_bytes=VMEM_LIMIT)


def _gelu(x):
    return 0.5 * x * (1.0 + jnp.tanh(math.sqrt(2.0 / math.pi) * (x + 0.044715 * (x * x * x))))


def _sigmoid(x):
    return 1.0 / (1.0 + jnp.exp(-x))


def _rms_rows(x):
    return x * lax.rsqrt(jnp.mean(x * x, axis=-1, keepdims=True) + EPS)


def _shift_rows(x, s, fill, row):
    return jnp.where(row >= s, pltpu.roll(x, s, 0), fill)


def _mod_body(c_ref, w_ref, b_ref, o_ref):
    c = c_ref[...]
    s = c * _sigmoid(c)
    o_ref[...] = jnp.dot(s, w_ref[...], precision=HIGHEST, preferred_element_type=F32) + b_ref[...]


def _modulation(c_all, mod_w, mod_b):
    nb = c_all.shape[0]
    depth = mod_w.shape[0]
    n_out = mod_w.shape[2]
    tn = D_MODEL
    return pl.pallas_call(
        _mod_body,
        grid=(depth, n_out // tn),
        in_specs=[pl.BlockSpec((nb, D_MODEL), lambda l, j: (0, 0)),
                  pl.BlockSpec((None, D_MODEL, tn), lambda l, j: (l, 0, j)),
                  pl.BlockSpec((None, 1, tn), lambda l, j: (l, 0, j))],
        out_specs=pl.BlockSpec((None, nb, tn), lambda l, j: (l, 0, j)),
        out_shape=jax.ShapeDtypeStruct((depth, nb, n_out), F32),
        compiler_params=_cparams(("arbitrary", "arbitrary")),
        name="modulation",
    )(c_all, mod_w, mod_b.reshape(depth, 1, n_out))


def _head_mean_sq(t, bones):
    sq = t * t
    hi = sq.astype(jnp.bfloat16)
    lo = (sq - hi.astype(F32)).astype(jnp.bfloat16)
    return (jnp.dot(hi, bones, preferred_element_type=F32)
            + jnp.dot(lo, bones, preferred_element_type=F32))


def _inproj_body(x_ref, sc_ref, sh_ref, g_ref, w_ref, bones_ref, qg_ref, kg_ref,
                 zm_ref, q_ref, kf_ref, vf_ref, kb_ref, vb_ref):
    x = x_ref[...]
    h = _rms_rows(x) * g_ref[...]
    h = h * (1.0 + sc_ref[...]) + sh_ref[...]
    z = jnp.dot(h.astype(MXU_DT), w_ref[...], preferred_element_type=F32)
    nm = 6 * W_GRP
    zm_ref[...] = z[:, :nm]
    q = z[:, nm:nm + W_GRP]
    k = z[:, nm + W_GRP:nm + 2 * W_GRP]
    v = z[:, nm + 2 * W_GRP:]
    bones = bones_ref[...]
    qn = (q * lax.rsqrt(_head_mean_sq(q, bones) + EPS) * qg_ref[...]) * (HEAD_DIM ** -0.5)
    kn = k * lax.rsqrt(_head_mean_sq(k, bones) + EPS) * kg_ref[...]
    q_ref[...] = qn.astype(q_ref.dtype)
    kf_ref[...] = kn
    vf_ref[...] = v
    kb_ref[...] = kn.astype(kb_ref.dtype)
    vb_ref[...] = v.astype(vb_ref.dtype)


def _mod_spec(mod, k, tm, rows_per_mod):
    if mod.ndim == 4:
        return pl.BlockSpec((None, None, 1, D_MODEL), lambda i: (k, (i * tm) // rows_per_mod, 0, 0))
    return pl.BlockSpec((None, tm, D_MODEL), lambda i: (k, i, 0))


def _in_proj(x, mod, norm_g, w_in, bones, qg, kg, *, tm, rows_per_mod):
    t = x.shape[0]
    n_in = w_in.shape[1]
    nm = 6 * W_GRP
    row = lambda i: (i, 0)
    const = lambda i: (0, 0)
    outs = [jax.ShapeDtypeStruct((t, nm), F32),
            jax.ShapeDtypeStruct((t, W_GRP), MXU_DT),
            jax.ShapeDtypeStruct((t, W_GRP), F32),
            jax.ShapeDtypeStruct((t, W_GRP), F32),
            jax.ShapeDtypeStruct((t, W_GRP), MXU_DT),
            jax.ShapeDtypeStruct((t, W_GRP), MXU_DT)]
    return pl.pallas_call(
        _inproj_body,
        grid=(t // tm,),
        in_specs=[pl.BlockSpec((tm, D_MODEL), row),
                  _mod_spec(mod, 1, tm, rows_per_mod),
                  _mod_spec(mod, 0, tm, rows_per_mod),
                  pl.BlockSpec((1, D_MODEL), const),
                  pl.BlockSpec((D_MODEL, n_in), const),
                  pl.BlockSpec((W_GRP, W_GRP), const),
                  pl.BlockSpec((1, W_GRP), const),
                  pl.BlockSpec((1, W_GRP), const)],
        out_specs=[pl.BlockSpec((tm, nm), row)] + [pl.BlockSpec((tm, W_GRP), row)] * 5,
        out_shape=outs,
        compiler_params=_cparams(("arbitrary",)),
        name="in_proj",
    )(x, mod, mod, norm_g, w_in, bones, qg, kg)


def _softplus(x):
    return jnp.maximum(x, 0.0) + jnp.log(1.0 + jnp.exp(-jnp.abs(x)))


def _mixer_ab_body(xa_ref, ga_ref, gb_ref, gc_ref, xb_ref,
                   cw_ref, cb_ref, wa_ref, ba_ref, wx_ref, bx_ref, lam_ref, sw_ref, on_a_ref, on_b_ref,
                   h0_ref, conv0_ref, sconv0_ref,
                   oa_ref, ob_ref, hn_ref, convn_ref, sconvn_ref,
                   xe_ref, pe_ref, hc_ref):
    i = pl.program_id(1)
    tl = xa_ref.shape[0]

    @pl.when(i == 0)
    def _():
        xe_ref[8 - (LRU_CONV - 1):8, :] = conv0_ref[...]
        pe_ref[8 - (SCONV_W - 1):8, :] = sconv0_ref[...]
        hc_ref[...] = h0_ref[...]

    row = lax.broadcasted_iota(jnp.int32, (tl, 1), 0)
    xa = xa_ref[...]
    xe_ref[8:, :] = xa
    cw = cw_ref[...]
    xc = cw[LRU_CONV - 1:LRU_CONV, :] * xa
    for s in range(1, LRU_CONV):
        xc = xc + cw[LRU_CONV - 1 - s:LRU_CONV - s, :] * xe_ref[8 - s:8 - s + tl, :]
    xc = xc + cb_ref[...]
    convn_ref[...] = xa[tl - (LRU_CONV - 1):, :]
    xe_ref[0:8, :] = xa[tl - 8:, :]
    xcb = xc.astype(MXU_DT)
    r = _sigmoid(jnp.dot(xcb, wa_ref[...], preferred_element_type=F32) + ba_ref[...])
    ig = _sigmoid(jnp.dot(xcb, wx_ref[...], preferred_element_type=F32) + bx_ref[...])
    log_a = (-LRU_C * r) * _softplus(-lam_ref[...])
    a = jnp.exp(log_a)
    b = jnp.sqrt(-jnp.tanh(log_a) * (a * a + 1.0)) * (ig * xc)
    s = 1
    while s < tl:
        b = a * _shift_rows(b, s, 0.0, row) + b
        a = a * _shift_rows(a, s, 1.0, row)
        s *= 2
    h = b + a * hc_ref[...]
    hc_ref[...] = h[tl - 1:, :]
    hn_ref[...] = h[tl - 1:, :]
    out_a = h * _gelu(ga_ref[...])
    oa_ref[...] = (_rms_rows(out_a) * on_a_ref[...]).astype(oa_ref.dtype)
    p = gc_ref[...] * xb_ref[...]
    pe_ref[8:, :] = p
    sw = sw_ref[...]
    yb = sw[SCONV_W - 1:SCONV_W, :] * p
    for s in range(1, SCONV_W):
        yb = yb + sw[SCONV_W - 1 - s:SCONV_W - s, :] * pe_ref[8 - s:8 - s + tl, :]
    sconvn_ref[...] = p[tl - (SCONV_W - 1):, :]
    pe_ref[0:8, :] = p[tl - 8:, :]
    out_b = gb_ref[...] * yb
    ob_ref[...] = (_rms_rows(out_b) * on_b_ref[...]).astype(ob_ref.dtype)


def _mixer_ab(zm, lp, h0, conv0, sconv0, *, tl):
    bsz, seq, _ = zm.shape
    col = lambda c: pl.BlockSpec((None, tl, W_GRP), lambda b, i, c=c: (b, i, c))
    const = lambda shp: pl.BlockSpec(shp, lambda b, i: (0,) * len(shp))
    per_b = lambda n: pl.BlockSpec((None, n, W_GRP), lambda b, i: (b, 0, 0))
    outs = [jax.ShapeDtypeStruct((bsz, seq, W_GRP), MXU_DT),
            jax.ShapeDtypeStruct((bsz, seq, W_GRP), MXU_DT),
            jax.ShapeDtypeStruct((bsz, 1, W_GRP), F32),
            jax.ShapeDtypeStruct((bsz, LRU_CONV - 1, W_GRP), F32),
            jax.ShapeDtypeStruct((bsz, SCONV_W - 1, W_GRP), F32)]
    return pl.pallas_call(
        _mixer_ab_body,
        grid=(bsz, seq // tl),
        in_specs=[col(0), col(1), col(2), col(3), col(4),
                  const((LRU_CONV, W_GRP)), const((1, W_GRP)),
                  const((W_GRP, W_GRP)), const((1, W_GRP)),
                  const((W_GRP, W_GRP)), const((1, W_GRP)),
                  const((1, W_GRP)), const((SCONV_W, W_GRP)),
                  const((1, W_GRP)), const((1, W_GRP)),
                  per_b(1), per_b(LRU_CONV - 1), per_b(SCONV_W - 1)],
        out_specs=[pl.BlockSpec((None, tl, W_GRP), lambda b, i: (b, i, 0)),
                   pl.BlockSpec((None, tl, W_GRP), lambda b, i: (b, i, 0)),
                   per_b(1), per_b(LRU_CONV - 1), per_b(SCONV_W - 1)],
        out_shape=outs,
        scratch_shapes=[pltpu.VMEM((tl + 8, W_GRP), F32),
                        pltpu.VMEM((tl + 8, W_GRP), F32),
                        pltpu.VMEM((1, W_GRP), F32)],
        compiler_params=_cparams(("arbitrary", "arbitrary")),
        name="mixer_ab",
    )(zm, zm, zm, zm, zm,
      lp['lru_conv_w'], lp['lru_conv_b'], lp['lru_wa_blk'], lp['lru_ba'], lp['lru_wx_blk'], lp['lru_bx'],
      lp['lru_lambda'], lp['sconv_w'], lp['on_a'], lp['on_b'],
      h0, conv0, sconv0)


def _s5_body(u_ref, bb_ref, cre_ref, cim_ref, ar_ref, ai_ref, d_ref, gw_ref, gb_ref, on_ref,
             h0r_ref, h0i_ref,
             o_ref, hnr_ref, hni_ref,
             tr_ref, ti_ref, hr_ref, hi_ref):
    b = pl.program_id(0)
    i = pl.program_id(1)
    tl = u_ref.shape[0]
    row = lax.broadcasted_iota(jnp.int32, (tl, 1), 0)
    ar = ar_ref[...]
    ai = ai_ref[...]

    @pl.when((b == 0) & (i == 0))
    def _():
        tr = jnp.broadcast_to(ar, (tl, S5_N))
        ti = jnp.broadcast_to(ai, (tl, S5_N))
        s = 1
        while s < tl:
            sr = _shift_rows(tr, s, 1.0, row)
            si = _shift_rows(ti, s, 0.0, row)
            tr, ti = tr * sr - ti * si, tr * si + ti * sr
            s *= 2
        tr_ref[...] = tr
        ti_ref[...] = ti

    @pl.when(i == 0)
    def _():
        hr_ref[...] = h0r_ref[...]
        hi_ref[...] = h0i_ref[...]

    u = u_ref[...]
    bu = jnp.dot(u.astype(MXU_DT), bb_ref[...], preferred_element_type=F32)
    xr = bu[:, :S5_N]
    xi = bu[:, S5_N:]
    pr, pi = ar, ai
    s = 1
    while s < tl:
        sr = _shift_rows(xr, s, 0.0, row)
        si = _shift_rows(xi, s, 0.0, row)
        xr, xi = xr + (pr * sr - pi * si), xi + (pr * si + pi * sr)
        pr, pi = pr * pr - pi * pi, 2.0 * (pr * pi)
        s *= 2
    h0r = hr_ref[...]
    h0i = hi_ref[...]
    tr = tr_ref[...]
    ti = ti_ref[...]
    hr = xr + (tr * h0r - ti * h0i)
    hi = xi + (tr * h0i + ti * h0r)
    hr_ref[...] = hr[tl - 1:, :]
    hi_ref[...] = hi[tl - 1:, :]
    hnr_ref[...] = hr[tl - 1:, :]
    hni_ref[...] = hi[tl - 1:, :]
    y = (jnp.dot(hr.astype(MXU_DT), cre_ref[...], preferred_element_type=F32)
         - jnp.dot(hi.astype(MXU_DT), cim_ref[...], preferred_element_type=F32)) + d_ref[...] * u
    g = jnp.dot(_gelu(y).astype(MXU_DT), gw_ref[...], preferred_element_type=F32) + gb_ref[...]
    out = g[:, :W_GRP] * _sigmoid(g[:, W_GRP:])
    o_ref[...] = (_rms_rows(out) * on_ref[...]).astype(o_ref.dtype)


def _s5_mixer(zm, lp, h0r, h0i, *, tl):
    bsz, seq, _ = zm.shape
    const = lambda shp: pl.BlockSpec(shp, lambda b, i: (0,) * len(shp))
    per_b = pl.BlockSpec((None, 1, S5_N), lambda b, i: (b, 0, 0))
    outs = [jax.ShapeDtypeStruct((bsz, seq, W_GRP), MXU_DT),
            jax.ShapeDtypeStruct((bsz, 1, S5_N), F32),
            jax.ShapeDtypeStruct((bsz, 1, S5_N), F32)]
    return pl.pallas_call(
        _s5_body,
        grid=(bsz, seq // tl),
        in_specs=[pl.BlockSpec((None, tl, W_GRP), lambda b, i: (b, i, 5)),
                  const((W_GRP, 2 * S5_N)), const((S5_N, W_GRP)), const((S5_N, W_GRP)),
                  const((1, S5_N)), const((1, S5_N)), const((1, W_GRP)),
                  const((W_GRP, 2 * W_GRP)), const((1, 2 * W_GRP)), const((1, W_GRP)),
                  per_b, per_b],
        out_specs=[pl.BlockSpec((None, tl, W_GRP), lambda b, i: (b, i, 0)), per_b, per_b],
        out_shape=outs,
        scratch_shapes=[pltpu.VMEM((tl, S5_N), F32), pltpu.VMEM((tl, S5_N), F32),
                        pltpu.VMEM((1, S5_N), F32), pltpu.VMEM((1, S5_N), F32)],
        compiler_params=_cparams(("arbitrary", "arbitrary")),
        name="s5_mixer",
    )(zm, lp['s5_bb'], lp['s5_cre'], lp['s5_cim'], lp['s5_abr'], lp['s5_abi'], lp['s5_d'],
      lp['s5_glu_w'], lp['s5_glu_b'], lp['on_c'], h0r, h0i)


CODE_MASKED = -1
CODE_ZERO = -2


def _bias_body(rb_ref, code_ref, o_ref):
    code = code_ref[...]
    acc = jnp.where(code == CODE_MASKED, NEG, 0.0).astype(F32)
    for c in range(REL_BUCKETS * ATT_HEADS):
        acc = jnp.where(code == c, rb_ref[c], acc)
    o_ref[...] = acc


def _bias_table(rel_bias, codes):
    rows, cols = codes.shape
    tr = max(t for t in range(8, 513, 8) if rows % t == 0)
    return pl.pallas_call(
        _bias_body,
        grid_spec=pltpu.PrefetchScalarGridSpec(
            num_scalar_prefetch=1,
            grid=(rows // tr,),
            in_specs=[pl.BlockSpec((tr, cols), lambda i, rb: (i, 0))],
            out_specs=pl.BlockSpec((tr, cols), lambda i, rb: (i, 0))),
        out_shape=jax.ShapeDtypeStruct((rows, cols), F32),
        compiler_params=_cparams(("arbitrary",)),
        name="bias_table",
    )(rel_bias.reshape(-1), jnp.asarray(codes))


def _t5_bucket(n):
    n = np.asarray(n).astype(np.int32)
    max_exact = REL_BUCKETS // 2
    nf = np.maximum(n, 1).astype(np.float32)
    large = max_exact + (np.log(nf / max_exact) / np.log(REL_MAX_DIST / max_exact)
                         * (REL_BUCKETS - max_exact)).astype(np.int32)
    large = np.minimum(large, REL_BUCKETS - 1)
    return np.where(n < max_exact, n, large).astype(np.int32)


def _prompt_bias_codes():
    i = np.arange(Q_BLK)[:, None]
    j = np.arange(2 * Q_BLK)[None, :]
    rel = Q_BLK + i - j
    out = np.zeros((len(PATTERNS), 2, ATT_HEADS, Q_BLK, 2 * Q_BLK), np.int32)
    for p, (win, dil) in enumerate(PATTERNS):
        span = win // dil
        valid = (rel >= 0) & (rel <= span)
        bucket = _t5_bucket(np.clip(rel, 0, None) * dil)
        for var in range(2):
            v = valid & ((j >= Q_BLK) | (var == 1))
            for h in range(ATT_HEADS):
                out[p, var, h] = np.where(v, bucket * ATT_HEADS + h, CODE_MASKED)
    return out.reshape(-1, 2 * Q_BLK)


S_KEYS = 136


def _sample_bias_codes(dec_seq, wb):
    out = np.full((len(PATTERNS), dec_seq, S_KEYS, W_GRP), CODE_MASKED, np.int32)
    head = np.arange(W_GRP) // HEAD_DIM
    for p, (win, dil) in enumerate(PATTERNS):
        span = win // dil
        for s in range(dec_seq):
            qpos = wb + s
            for r in range(S_KEYS):
                if r < 128:
                    if dil == 1:
                        pos = wb - 128 + r
                    else:
                        pos = wb - 128 * dil + r * dil + (s % dil)
                else:
                    pos = wb + (r - 128)
                dist = qpos - pos
                if dist >= 0 and dist % dil == 0 and dist // dil <= span:
                    out[p, s, r] = _t5_bucket(np.array(dist)) * ATT_HEADS + head
    return out.reshape(-1, W_GRP)


def _attn_p_body(*refs, n_prev):
    q_ref, kp_ref, kc_ref, vp_ref, vc_ref, bias_ref = refs[:6]
    prev = refs[6:6 + 2 * n_prev]
    rest = refs[6 + 2 * n_prev:]
    q = q_ref[...]
    k2 = jnp.concatenate([kp_ref[...], kc_ref[...]], axis=0)
    v2 = jnp.concatenate([vp_ref[...], vc_ref[...]], axis=0)
    lane = lax.broadcasted_iota(jnp.int32, (1, W_GRP), 1)
    o = jnp.zeros((Q_BLK, W_GRP), F32)
    lse = jnp.zeros((Q_BLK, W_GRP), F32)
    for h in range(ATT_HEADS):
        hm = (lane >= h * HEAD_DIM) & (lane < (h + 1) * HEAD_DIM)
        qh = jnp.where(hm, q, jnp.zeros_like(q))
        s = lax.dot_general(qh, k2, (((1,), (1,)), ((), ())), preferred_element_type=F32) + bias_ref[h]
        m = jnp.max(s, axis=-1, keepdims=True)
        pr = jnp.exp(s - m)
        den = jnp.sum(pr, axis=-1, keepdims=True)
        oh = jnp.dot(pr.astype(MXU_DT), v2, preferred_element_type=F32) / den
        o = jnp.where(hm, oh, o)
        lse = jnp.where(hm, m + jnp.log(den), lse)
    if n_prev == 0:
        o_ref, lse_ref = rest
        o_ref[...] = o
        lse_ref[...] = lse
    else:
        on_ref, out_ref = rest
        os_ = [o] + [prev[2 * t][...] for t in range(n_prev)]
        ls_ = [lse] + [prev[2 * t + 1][...] for t in range(n_prev)]
        mx = functools.reduce(jnp.maximum, ls_)
        ws = [jnp.exp(l_ - mx) for l_ in ls_]
        num = functools.reduce(lambda a_, b_: a_ + b_, [w_ * o_ for w_, o_ in zip(ws, os_)])
        den = functools.reduce(lambda a_, b_: a_ + b_, ws)
        merged = num / den
        out_ref[...] = (_rms_rows(merged) * on_ref[...]).astype(out_ref.dtype)


def _attn_prompt_pattern(q, k, v, bias, dil, prev=(), on_d=None):
    bsz, seq, _ = q.shape
    md = seq // dil
    nb = md // Q_BLK
    view = lambda t: t.reshape(bsz, md, dil * W_GRP)
    cur = pl.BlockSpec((None, Q_BLK, W_GRP), lambda b, r, n: (b, n, r))
    prv = pl.BlockSpec((None, Q_BLK, W_GRP), lambda b, r, n: (b, jnp.maximum(n - 1, 0), r))
    bsp = pl.BlockSpec((None, ATT_HEADS, Q_BLK, 2 * Q_BLK), lambda b, r, n: (jnp.minimum(n, 1), 0, 0, 0))
    n_prev = len(prev) // 2
    in_specs = [cur, prv, cur, prv, cur, bsp] + [cur] * len(prev)
    args = [view(q), view(k), view(k), view(v), view(v), bias] + [view(t) for t in prev]
    if n_prev == 0:
        out_specs = [cur, cur]
        out_shape = [jax.ShapeDtypeStruct((bsz, md, dil * W_GRP), F32)] * 2
    else:
        in_specs.append(pl.BlockSpec((1, W_GRP), lambda b, r, n: (0, 0)))
        args.append(on_d)
        out_specs = cur
        out_shape = jax.ShapeDtypeStruct((bsz, md, dil * W_GRP), MXU_DT)
    res = pl.pallas_call(
        functools.partial(_attn_p_body, n_prev=n_prev),
        grid=(bsz, dil, nb),
        in_specs=in_specs, out_specs=out_specs, out_shape=out_shape,
        compiler_params=_cparams(("arbitrary", "arbitrary", "arbitrary")),
        name=f"attn_prompt_d{dil}",
    )(*args)
    if n_prev == 0:
        return tuple(t.reshape(bsz, seq, W_GRP) for t in res)
    return res.reshape(bsz, seq, W_GRP)


def _attn_prompt(q, k, v, bias_all, on_d):
    prev = ()
    for p, (_, dil) in enumerate(PATTERNS):
        if p + 1 < len(PATTERNS):
            prev = prev + _attn_prompt_pattern(q, k, v, bias_all[p], dil)
        else:
            return _attn_prompt_pattern(q, k, v, bias_all[p], dil, prev=prev, on_d=on_d)


def _attn_s_body(q_ref, kn_ref, vn_ref, k1_ref, k4_ref, k16_ref, v1_ref, v4_ref, v16_ref,
                 bias_ref, ones_ref, on_ref, o_ref):
    dec = q_ref.shape[0]
    rnd = lambda t: t.astype(MXU_DT).astype(F32)
    q = q_ref[...].astype(F32)
    kn = rnd(kn_ref[...])
    vn = rnd(vn_ref[...])
    ones_blk = ones_ref[...]
    rows = []
    for s in range(dec):
        qs = q[s:s + 1, :]
        os_, ls_ = [], []
        for p, (_, dil) in enumerate(PATTERNS):
            kref, vref = ((k1_ref, v1_ref), (k4_ref, v4_ref), (k16_ref, v16_ref))[p]
            c0 = (s % dil) * W_GRP
            kcat = jnp.concatenate([rnd(kref[:, c0:c0 + W_GRP]), kn], axis=0)
            vcat = jnp.concatenate([rnd(vref[:, c0:c0 + W_GRP]), vn], axis=0)
            prod = kcat * qs
            hi = prod.astype(jnp.bfloat16)
            lo = (prod - hi.astype(F32)).astype(jnp.bfloat16)
            sc = (jnp.dot(hi, ones_blk, preferred_element_type=F32)
                  + jnp.dot(lo, ones_blk, preferred_element_type=F32)) + bias_ref[p, s]
            m = jnp.max(sc, axis=0, keepdims=True)
            pr = jnp.exp(sc - m)
            den = jnp.sum(pr, axis=0, keepdims=True)
            os_.append(jnp.sum(rnd(pr) * vcat, axis=0, keepdims=True) / den)
            ls_.append(m + jnp.log(den))
        mx = functools.reduce(jnp.maximum, ls_)
        ws = [jnp.exp(l_ - mx) for l_ in ls_]
        num = functools.reduce(lambda a_, b_: a_ + b_, [w_ * o_ for w_, o_ in zip(ws, os_)])
        rows.append(num / functools.reduce(lambda a_, b_: a_ + b_, ws))
    merged = jnp.concatenate(rows, axis=0)
    o_ref[...] = (_rms_rows(merged) * on_ref[...]).astype(o_ref.dtype)


def _attn_sample(q, kn, vn, cache_k, cache_v, l, bias_s, ones_blk, on_d):
    bsz, dec, _ = q.shape
    depth, _, wb = cache_k.shape[:3]
    assert wb % 128 == 0 and wb >= 128 * PATTERNS[-1][1] and dec <= 8
    per_b = pl.BlockSpec((None, dec, W_GRP), lambda b: (b, 0, 0))

    def cache_view(c, dil):
        cv = c.reshape(depth, bsz, wb // dil, dil * W_GRP)
        width = min(dil, 8) * W_GRP
        return cv, pl.BlockSpec((None, None, 128, width), lambda b: (l, b, wb // dil // 128 - 1, 0))

    views = [cache_view(c, dil) for c in (cache_k, cache_v) for (_, dil) in PATTERNS]
    return pl.pallas_call(
        _attn_s_body,
        grid=(bsz,),
        in_specs=[per_b, per_b, per_b] + [sp for _, sp in views] + [
            pl.BlockSpec((len(PATTERNS), dec, S_KEYS, W_GRP), lambda b: (0, 0, 0, 0)),
            pl.BlockSpec((W_GRP, W_GRP), lambda b: (0, 0)),
            pl.BlockSpec((1, W_GRP), lambda b: (0, 0))],
        out_specs=per_b,
        out_shape=jax.ShapeDtypeStruct((bsz, dec, W_GRP), MXU_DT),
        compiler_params=_cparams(("arbitrary",)),
        name="attn_sample",
    )(q, kn, vn, *[cv for cv, _ in views], bias_s, ones_blk, on_d)


ROUTER_LANES = 128


def _post_mix_body(x_ref, a_ref, b_ref, c_ref, d_ref, w_ref, g1_ref, sc_ref, sh_ref, g_ref,
                   wr_ref, br_ref, tri_ref, cnt_in_ref,
                   x1_ref, h2_ref, eid_ref, gate_ref, rank_ref, cnt_ref,
                   run_ref):
    i = pl.program_id(0)

    @pl.when(i == 0)
    def _():
        run_ref[...] = cnt_in_ref[...]

    mix = jnp.concatenate([a_ref[...], b_ref[...], c_ref[...], d_ref[...]], axis=1)
    y = jnp.dot(mix, w_ref[...], preferred_element_type=F32)
    x1 = x_ref[...] + g1_ref[...] * y
    x1_ref[...] = x1
    h2 = _rms_rows(x1) * g_ref[...]
    h2 = h2 * (1.0 + sc_ref[...]) + sh_ref[...]
    h2_ref[...] = h2
    logits = jnp.dot(h2, wr_ref[...], precision=HIGHEST, preferred_element_type=F32) + br_ref[...]
    tm = logits.shape[0]
    lane = lax.broadcasted_iota(jnp.int32, (tm, ROUTER_LANES), 1)
    big = jnp.int32(10 ** 6)
    is_g = lane < N_GROUPS
    gl = jnp.where(is_g, logits, -jnp.inf)
    gmax = jnp.max(gl, axis=-1, keepdims=True)
    gsel = jnp.min(jnp.where(gl == gmax, lane, big), axis=-1, keepdims=True)
    gprob = 1.0 / jnp.sum(jnp.where(is_g, jnp.exp(logits - gmax), 0.0), axis=-1, keepdims=True)
    lo_lane = N_GROUPS + gsel * EXP_PER_GROUP
    in_grp = (lane >= lo_lane) & (lane < lo_lane + EXP_PER_GROUP)
    el = jnp.where(in_grp, logits, -jnp.inf)
    v1 = jnp.max(el, axis=-1, keepdims=True)
    i1 = jnp.min(jnp.where(el == v1, lane, big), axis=-1, keepdims=True)
    el2 = jnp.where(lane == i1, -jnp.inf, el)
    v2 = jnp.max(el2, axis=-1, keepdims=True)
    i2 = jnp.min(jnp.where(el2 == v2, lane, big), axis=-1, keepdims=True)
    e2w = jnp.exp(v2 - v1)
    gate1 = (1.0 / (1.0 + e2w)) * gprob
    gate2 = (e2w / (1.0 + e2w)) * gprob
    e1 = i1 - N_GROUPS
    e2 = i2 - N_GROUPS
    oh1 = lane == e1
    oh2 = lane == e2
    both = jnp.where(oh1 | oh2, 1.0, 0.0)
    before = jnp.dot(tri_ref[...], both.astype(jnp.bfloat16), preferred_element_type=F32) + run_ref[...]
    r1 = jnp.sum(jnp.where(oh1, before, 0.0), axis=-1, keepdims=True).astype(jnp.int32)
    r2 = jnp.sum(jnp.where(oh2, before, 0.0), axis=-1, keepdims=True).astype(jnp.int32)
    run = run_ref[...] + jnp.sum(both, axis=0, keepdims=True)
    run_ref[...] = run
    cnt_ref[...] = run
    eid_ref[...] = jnp.where(lane == 0, e1, jnp.where(lane == 1, e2, 0))
    gate_ref[...] = jnp.where(lane == 0, gate1, jnp.where(lane == 1, gate2, 0.0))
    rank_ref[...] = jnp.where(lane == 0, r1, jnp.where(lane == 1, r2, 0))


def _post_mix(x, pieces, mod, lp, cnt_in, *, tm, rows_per_mod):
    t = x.shape[0]
    row = lambda i: (i, 0)
    const = lambda i: (0, 0)
    piece = pl.BlockSpec((tm, W_GRP), row)
    wide = pl.BlockSpec((tm, D_MODEL), row)
    lanes = pl.BlockSpec((tm, ROUTER_LANES), row)
    outs = [jax.ShapeDtypeStruct((t, D_MODEL), F32), jax.ShapeDtypeStruct((t, D_MODEL), F32),
            jax.ShapeDtypeStruct((t, ROUTER_LANES), jnp.int32), jax.ShapeDtypeStruct((t, ROUTER_LANES), F32),
            jax.ShapeDtypeStruct((t, ROUTER_LANES), jnp.int32), jax.ShapeDtypeStruct((1, ROUTER_LANES), F32)]
    return pl.pallas_call(
        _post_mix_body,
        grid=(t // tm,),
        in_specs=[wide, piece, piece, piece, piece,
                  pl.BlockSpec((D_MODEL, D_MODEL), const),
                  _mod_spec(mod, 2, tm, rows_per_mod), _mod_spec(mod, 4, tm, rows_per_mod),
                  _mod_spec(mod, 3, tm, rows_per_mod),
                  pl.BlockSpec((1, D_MODEL), const),
                  pl.BlockSpec((D_MODEL, ROUTER_LANES), const), pl.BlockSpec((1, ROUTER_LANES), const),
                  pl.BlockSpec((tm, tm), const), pl.BlockSpec((1, ROUTER_LANES), const)],
        out_specs=[wide, wide, lanes, lanes, lanes, pl.BlockSpec((1, ROUTER_LANES), const)],
        out_shape=outs,
        scratch_shapes=[pltpu.VMEM((1, ROUTER_LANES), F32)],
        compiler_params=_cparams(("arbitrary",)),
        name="post_mix",
    )(x, *pieces, lp['w_out'], mod, mod, mod, lp['norm_ffn'], lp['router_w'], lp['router_b'], lp['tri'], cnt_in)


def _row_copy(src_ref, s, dst_ref, d, sem):
    return pltpu.make_async_copy(src_ref.at[pl.ds(s, 1)], dst_ref.at[pl.ds(d, 1)], sem)


def _dispatch_body(dest_ref, h_ref, xs_in_ref, xs_ref, sem):
    del xs_in_ref
    tm = h_ref.shape[0]
    base = pl.program_id(0) * (2 * tm)

    def issue(t, c):
        _row_copy(h_ref, t, xs_ref, dest_ref[base + 2 * t], sem).start()
        _row_copy(h_ref, t, xs_ref, dest_ref[base + 2 * t + 1], sem).start()
        return c

    lax.fori_loop(0, tm, issue, 0)

    def drain(t, c):
        _row_copy(h_ref, 0, xs_ref, 0, sem).wait()
        return c

    lax.fori_loop(0, 2 * tm, drain, 0)


def _dispatch(h2, dest_flat, xs, *, tm):
    t = h2.shape[0]
    return pl.pallas_call(
        _dispatch_body,
        grid_spec=pltpu.PrefetchScalarGridSpec(
            num_scalar_prefetch=1,
            grid=(t // tm,),
            in_specs=[pl.BlockSpec((tm, D_MODEL), lambda i, d: (i, 0)),
                      pl.BlockSpec(memory_space=pl.ANY)],
            out_specs=pl.BlockSpec(memory_space=pl.ANY),
            scratch_shapes=[pltpu.SemaphoreType.DMA(())]),
        out_shape=jax.ShapeDtypeStruct(xs.shape, xs.dtype),
        input_output_aliases={2: 0},
        compiler_params=_cparams(("arbitrary",)),
        name="moe_dispatch",
    )(dest_flat, h2, xs)


def _experts_body(blk_e_ref, n_used_ref, xs_ref, wg_ref, wu_ref, wd_ref, o_ref, wgb_ref, wub_ref, wdb_ref):
    i = pl.program_id(0)
    e = blk_e_ref[i]
    e_prev = blk_e_ref[jnp.maximum(i - 1, 0)]

    @pl.when((i == 0) | (e != e_prev))
    def _():
        wgb_ref[...] = wg_ref[...].astype(MXU_DT)
        wub_ref[...] = wu_ref[...].astype(MXU_DT)
        wdb_ref[...] = wd_ref[...].astype(MXU_DT)

    @pl.when(i < n_used_ref[0])
    def _():
        x = xs_ref[...].astype(MXU_DT)
        g = jnp.dot(x, wgb_ref[...], preferred_element_type=F32)
        u = jnp.dot(x, wub_ref[...], preferred_element_type=F32)
        hmid = (g * _sigmoid(g)) * u
        o_ref[...] = jnp.dot(hmid.astype(MXU_DT), wdb_ref[...], preferred_element_type=F32)

    @pl.when(i >= n_used_ref[0])
    def _():
        o_ref[...] = jnp.zeros_like(o_ref)


def _experts(xs, blk_e, n_used, wg, wu, wd, l):
    n_blk = xs.shape[0] // MOE_ROWS
    xmap = lambda i, be, nu: (jnp.minimum(i, nu[0] - 1), 0)
    return pl.pallas_call(
        _experts_body,
        grid_spec=pltpu.PrefetchScalarGridSpec(
            num_scalar_prefetch=2,
            grid=(n_blk,),
            in_specs=[pl.BlockSpec((MOE_ROWS, D_MODEL), xmap),
                      pl.BlockSpec((None, None, D_MODEL, D_EXPERT), lambda i, be, nu: (l, be[i], 0, 0)),
                      pl.BlockSpec((None, None, D_MODEL, D_EXPERT), lambda i, be, nu: (l, be[i], 0, 0)),
                      pl.BlockSpec((None, None, D_EXPERT, D_MODEL), lambda i, be, nu: (l, be[i], 0, 0))],
            out_specs=pl.BlockSpec((MOE_ROWS, D_MODEL), lambda i, be, nu: (i, 0)),
            scratch_shapes=[pltpu.VMEM((D_MODEL, D_EXPERT), MXU_DT), pltpu.VMEM((D_MODEL, D_EXPERT), MXU_DT),
                            pltpu.VMEM((D_EXPERT, D_MODEL), MXU_DT)]),
        out_shape=jax.ShapeDtypeStruct(xs.shape, F32),
        compiler_params=_cparams(("arbitrary",)),
        name="moe_experts",
    )(blk_e, n_used, xs, wg, wu, wd)


def _combine_body(dest_ref, ys_ref, x1_ref, gate_ref, g2_ref, x2_ref, buf0_ref, buf1_ref, sem):
    tm = x1_ref.shape[0]
    base = pl.program_id(0) * (2 * tm)

    def issue(t, c):
        _row_copy(ys_ref, dest_ref[base + 2 * t], buf0_ref, t, sem).start()
        _row_copy(ys_ref, dest_ref[base + 2 * t + 1], buf1_ref, t, sem).start()
        return c

    lax.fori_loop(0, tm, issue, 0)

    def drain(t, c):
        _row_copy(ys_ref, 0, buf0_ref, 0, sem).wait()
        return c

    lax.fori_loop(0, 2 * tm, drain, 0)
    gate = gate_ref[...]
    y = buf0_ref[...] * gate[:, 0:1] + buf1_ref[...] * gate[:, 1:2]
    x2_ref[...] = x1_ref[...] + g2_ref[...] * y


def _combine(ys, dest_flat, x1, gate, mod, *, tm, rows_per_mod):
    t = x1.shape[0]
    if mod.ndim == 4:
        g2_spec = pl.BlockSpec((None, None, 1, D_MODEL), lambda i, d: (5, (i * tm) // rows_per_mod, 0, 0))
    else:
        g2_spec = pl.BlockSpec((None, tm, D_MODEL), lambda i, d: (5, i, 0))
    return pl.pallas_call(
        _combine_body,
        grid_spec=pltpu.PrefetchScalarGridSpec(
            num_scalar_prefetch=1,
            grid=(t // tm,),
            in_specs=[pl.BlockSpec(memory_space=pl.ANY),
                      pl.BlockSpec((tm, D_MODEL), lambda i, d: (i, 0)),
                      pl.BlockSpec((tm, ROUTER_LANES), lambda i, d: (i, 0)),
                      g2_spec],
            out_specs=pl.BlockSpec((tm, D_MODEL), lambda i, d: (i, 0)),
            scratch_shapes=[pltpu.VMEM((tm, D_MODEL), F32), pltpu.VMEM((tm, D_MODEL), F32),
                            pltpu.SemaphoreType.DMA(())]),
        out_shape=jax.ShapeDtypeStruct((t, D_MODEL), F32),
        compiler_params=_cparams(("arbitrary",)),
        name="moe_combine",
    )(dest_flat, ys, x1, gate, mod)


def _routing_tables(cnt, eids, ranks, n_blk):
    counts = cnt[0, :N_EXPERTS].astype(jnp.int32)
    padded = (counts + MOE_ROWS - 1) // MOE_ROWS * MOE_ROWS
    ends = jnp.cumsum(padded)
    starts = ends - padded
    dests = [(starts[e[:, :2]] + r[:, :2]).reshape(-1) for e, r in zip(eids, ranks)]
    blk_start = jnp.arange(n_blk, dtype=jnp.int32) * MOE_ROWS
    blk_e = jnp.minimum(jnp.searchsorted(ends, blk_start, side='right'), N_EXPERTS - 1).astype(jnp.int32)
    n_used = (ends[-1] // MOE_ROWS).astype(jnp.int32).reshape(1)
    return dests, blk_e, n_used


def _block_diag(w):
    g, r, c = w.shape
    eye = jnp.eye(g, dtype=w.dtype)
    return (eye[:, None, :, None] * w[:, :, None, :]).reshape(g * r, g * c)


def _s5_discretise(log_dt, a_re, a_im, b_re, b_im):
    step = jnp.exp(log_dt)[:, None]
    mag = jnp.exp(a_re * step)
    ang = a_im * step
    abr = mag * jnp.cos(ang)
    abi = mag * jnp.sin(ang)
    den = a_re * a_re + a_im * a_im
    zr = ((abr - 1.0) * a_re + abi * a_im) / den
    zi = (abi * a_re - (abr - 1.0) * a_im) / den
    bbr = zr[..., None] * b_re - zi[..., None] * b_im
    bbi = zr[..., None] * b_im + zi[..., None] * b_re
    return abr, abi, bbr, bbi


def _prep_layer(P, l):
    row = lambda a: a.reshape(1, -1)
    abr, abi, bbr, bbi = _s5_discretise(P['s5_log_dt'][l], P['s5_a_re'][l], P['s5_a_im'][l],
                                        P['s5_b_re'][l], P['s5_b_im'][l])
    on = P['out_norm'][l]
    bones = _block_diag(jnp.full((ATT_HEADS, HEAD_DIM, HEAD_DIM), 1.0 / HEAD_DIM, F32)).astype(jnp.bfloat16)
    return {
        'norm_mix': row(P['norm_mix'][l]), 'norm_ffn': row(P['norm_ffn'][l]),
        'w_in': P['w_in'][l].astype(MXU_DT), 'w_out': P['w_out'][l].astype(MXU_DT),
        'bones': bones,
        'qg': row(jnp.tile(P['q_norm'][l], ATT_HEADS)), 'kg': row(jnp.tile(P['k_norm'][l], ATT_HEADS)),
        'lru_conv_w': P['lru_conv_w'][l], 'lru_conv_b': row(P['lru_conv_b'][l]),
        'lru_wa_blk': _block_diag(P['lru_wa'][l]).astype(MXU_DT), 'lru_ba': row(P['lru_ba'][l]),
        'lru_wx_blk': _block_diag(P['lru_wx'][l]).astype(MXU_DT), 'lru_bx': row(P['lru_bx'][l]),
        'lru_lambda': row(P['lru_lambda'][l]), 'sconv_w': P['sconv_w'][l],
        'on_a': row(on[0:W_GRP]), 'on_b': row(on[W_GRP:2 * W_GRP]),
        'on_c': row(on[2 * W_GRP:3 * W_GRP]), 'on_d': row(on[3 * W_GRP:]),
        's5_bb': jnp.concatenate([_block_diag(bbr.transpose(0, 2, 1)), _block_diag(bbi.transpose(0, 2, 1))],
                                 axis=1).astype(MXU_DT),
        's5_cre': _block_diag(P['s5_c_re'][l].transpose(0, 2, 1)).astype(MXU_DT),
        's5_cim': _block_diag(P['s5_c_im'][l].transpose(0, 2, 1)).astype(MXU_DT),
        's5_abr': row(abr), 's5_abi': row(abi), 's5_d': row(P['s5_d'][l]),
        's5_glu_w': P['s5_glu_w'][l].astype(MXU_DT), 's5_glu_b': row(P['s5_glu_b'][l]),
        'router_w': jnp.zeros((D_MODEL, ROUTER_LANES), F32)
                       .at[:, :N_GROUPS].set(P['router_g_w'][l])
                       .at[:, N_GROUPS:N_GROUPS + N_EXPERTS].set(P['router_e_w'][l]),
        'router_b': jnp.zeros((1, ROUTER_LANES), F32)
                       .at[0, :N_GROUPS].set(P['router_g_b'][l])
                       .at[0, N_GROUPS:N_GROUPS + N_EXPERTS].set(P['router_e_b'][l]),
    }


TOKEN_TILE = 256
SEQ_TILE = 256

_PARAM_NAMES = ('rel_bias', 'mod_w', 'mod_b', 'norm_mix', 'norm_ffn', 'w_in', 'lru_conv_w', 'lru_conv_b',
                'lru_wa', 'lru_ba', 'lru_wx', 'lru_bx', 'lru_lambda', 'sconv_w', 's5_log_dt', 's5_a_re',
                's5_a_im', 's5_b_re', 's5_b_im', 's5_c_re', 's5_c_im', 's5_d', 's5_glu_w', 's5_glu_b',
                'q_norm', 'k_norm', 'out_norm', 'w_out', 'router_g_w', 'router_g_b', 'router_e_w',
                'router_e_b', 'moe_w_gate', 'moe_w_up', 'moe_w_down')


def _mixers(x, mod, lp, st, attn_fn, *, batch, seq, tm, tl, rows_per_mod):
    zm, q, kf, vf, kb, vb = _in_proj(x, mod, lp['norm_mix'], lp['w_in'], lp['bones'], lp['qg'], lp['kg'],
                                     tm=tm, rows_per_mod=rows_per_mod)
    zm3 = zm.reshape(batch, seq, 6 * W_GRP)
    oa, ob, lru_h, lru_conv, sconv = _mixer_ab(zm3, lp, st['lru_h'], st['lru_conv'], st['sconv'], tl=tl)
    oc, s5_re, s5_im = _s5_mixer(zm3, lp, st['s5_re'], st['s5_im'], tl=tl)
    r3 = lambda t: t.reshape(batch, seq, W_GRP)
    od = attn_fn(r3(q), r3(kf), r3(vf), r3(kb), r3(vb))
    flat = lambda t: t.reshape(batch * seq, W_GRP)
    new_st = {'lru_h': lru_h[:, 0], 'lru_conv': lru_conv, 'sconv': sconv,
              's5_re': s5_re.reshape(batch, S5_GROUPS, S5_STATE), 's5_im': s5_im.reshape(batch, S5_GROUPS, S5_STATE),
              'win_k': kf.reshape(batch, seq, ATT_HEADS, HEAD_DIM), 'win_v': vf.reshape(batch, seq, ATT_HEADS, HEAD_DIM)}
    return [flat(oa), flat(ob), flat(oc), flat(od)], new_st


def kernel(x_prompt, x_sample, c_prompt, c_sample, state_lru_h, state_lru_conv, state_sconv, state_s5_re, state_s5_im, cache_win_k, cache_win_v, rel_bias, mod_w, mod_b, norm_mix, norm_ffn, w_in, lru_conv_w, lru_conv_b, lru_wa, lru_ba, lru_wx, lru_bx, lru_lambda, sconv_w, s5_log_dt, s5_a_re, s5_a_im, s5_b_re, s5_b_im, s5_c_re, s5_c_im, s5_d, s5_glu_w, s5_glu_b, q_norm, k_norm, out_norm, w_out, router_g_w, router_g_b, router_e_w, router_e_b, moe_w_gate, moe_w_up, moe_w_down):
    P = dict(zip(_PARAM_NAMES, (rel_bias, mod_w, mod_b, norm_mix, norm_ffn, w_in, lru_conv_w, lru_conv_b,
                                lru_wa, lru_ba, lru_wx, lru_bx, lru_lambda, sconv_w, s5_log_dt, s5_a_re,
                                s5_a_im, s5_b_re, s5_b_im, s5_c_re, s5_c_im, s5_d, s5_glu_w, s5_glu_b,
                                q_norm, k_norm, out_norm, w_out, router_g_w, router_g_b, router_e_w,
                                router_e_b, moe_w_gate, moe_w_up, moe_w_down)))
    bp, seq, d = x_prompt.shape
    bs, dec, _ = x_sample.shape
    depth = mod_w.shape[0]
    tp, ts = bp * seq, bs * dec
    wb = cache_win_k.shape[2]
    wp = min(PATTERNS[-1][0], seq)
    tm_p = min(TOKEN_TILE, tp)
    tm_s = min(TOKEN_TILE, ts)
    tl_p = min(SEQ_TILE, seq)

    nc = -(-(bp + bs) // 8) * 8
    c_all = jnp.zeros((nc, d), F32).at[:bp].set(c_prompt).at[bp:bp + bs].set(c_sample)
    mod_all = _modulation(c_all, mod_w, mod_b)
    bias_p = _bias_table(rel_bias, _prompt_bias_codes()).reshape(len(PATTERNS), 2, ATT_HEADS, Q_BLK, 2 * Q_BLK)
    bias_s = _bias_table(rel_bias, _sample_bias_codes(dec, wb)).reshape(len(PATTERNS), dec, S_KEYS, W_GRP)
    ones_blk = _block_diag(jnp.ones((ATT_HEADS, HEAD_DIM, HEAD_DIM), jnp.bfloat16))
    tri = jnp.asarray(np.tril(np.ones((TOKEN_TILE, TOKEN_TILE), np.float32), -1), jnp.bfloat16)
    n_blk = (2 * (tp + ts)) // MOE_ROWS + N_EXPERTS

    zero_st = {'lru_h': jnp.zeros((bp, 1, W_GRP), F32), 'lru_conv': jnp.zeros((bp, LRU_CONV - 1, W_GRP), F32),
               'sconv': jnp.zeros((bp, SCONV_W - 1, W_GRP), F32),
               's5_re': jnp.zeros((bp, 1, S5_N), F32), 's5_im': jnp.zeros((bp, 1, S5_N), F32)}
    names = ('lru_h', 'lru_conv', 'sconv', 's5_re', 's5_im', 'win_k', 'win_v')
    acc_p = {n: [] for n in names}
    acc_s = {n: [] for n in names}
    xp = x_prompt.reshape(tp, d)
    xs = x_sample.reshape(ts, d)
    for l in range(depth):
        lp = _prep_layer(P, l)
        lp['tri'] = tri
        m6 = mod_all[l].reshape(nc, 6, d).transpose(1, 0, 2)
        mod_p = m6[:, :bp].reshape(6, bp, 1, d)
        mod_s = jnp.repeat(m6[:, bp:bp + bs], dec, axis=1)
        attn_p = lambda q, kf, vf, kb, vb: _attn_prompt(q, kb, vb, bias_p, lp['on_d'])
        pieces_p, st_p = _mixers(xp, mod_p, lp, zero_st, attn_p, batch=bp, seq=seq, tm=tm_p, tl=tl_p,
                                 rows_per_mod=seq)
        st_p['win_k'] = st_p['win_k'][:, seq - wp:]
        st_p['win_v'] = st_p['win_v'][:, seq - wp:]
        cnt0 = jnp.zeros((1, ROUTER_LANES), F32)
        x1p, h2p, eid_p, gate_p, rank_p, cnt = _post_mix(xp, pieces_p, mod_p, lp, cnt0, tm=tm_p, rows_per_mod=seq)
        samp_st = {'lru_h': state_lru_h[l][:, None], 'lru_conv': state_lru_conv[l], 'sconv': state_sconv[l],
                   's5_re': state_s5_re[l].reshape(bs, 1, S5_N), 's5_im': state_s5_im[l].reshape(bs, 1, S5_N)}
        attn_s = lambda q, kf, vf, kb, vb: _attn_sample(q, kf, vf, cache_win_k, cache_win_v, l, bias_s,
                                                        ones_blk, lp['on_d'])
        pieces_s, st_s = _mixers(xs, mod_s, lp, samp_st, attn_s, batch=bs, seq=dec, tm=tm_s, tl=dec,
                                 rows_per_mod=dec)
        x1s, h2s, eid_s, gate_s, rank_s, cnt = _post_mix(xs, pieces_s, mod_s, lp, cnt, tm=tm_s, rows_per_mod=dec)
        dests, blk_e, n_used = _routing_tables(cnt, [eid_p, eid_s], [rank_p, rank_s], n_blk)
        slots = jnp.zeros((n_blk * MOE_ROWS, d), F32)
        slots = _dispatch(h2p, dests[0], slots, tm=tm_p)
        slots = _dispatch(h2s, dests[1], slots, tm=tm_s)
        ys = _experts(slots, blk_e, n_used, moe_w_gate, moe_w_up, moe_w_down, l)
        xp = _combine(ys, dests[0], x1p, gate_p, mod_p, tm=tm_p, rows_per_mod=seq)
        xs = _combine(ys, dests[1], x1s, gate_s, mod_s, tm=tm_s, rows_per_mod=dec)
        for n in names:
            acc_p[n].append(st_p[n])
            acc_s[n].append(st_s[n])
    new_p = {n: jnp.stack(acc_p[n], axis=0) for n in names}
    new_s = {n: jnp.stack(acc_s[n], axis=0) for n in names}
    return (xp.reshape(bp, seq, d), xs.reshape(bs, dec, d),
            new_p['lru_h'], new_p['lru_conv'], new_p['sconv'], new_p['s5_re'], new_p['s5_im'],
            new_p['win_k'], new_p['win_v'],
            new_s['lru_h'], new_s['lru_conv'], new_s['sconv'], new_s['s5_re'], new_s['s5_im'],
            new_s['win_k'], new_s['win_v'])
```

```python
import functools
import math

import numpy as np
import jax
import jax.numpy as jnp
from jax import lax
from jax.experimental import pallas as pl
from jax.experimental.pallas import tpu as pltpu

F32 = jnp.float32
MXU_DT = jnp.bfloat16
HIGHEST = lax.Precision.HIGHEST

D_MODEL = 1024
DEPTH = 4
W_GRP = 256
N_Z = 9
LRU_HEADS = 4
LRU_CONV = 4
LRU_C = 8.0
SCONV_W = 3
S5_CH = 16
S5_GROUPS = 16
S5_STATE = 64
S5_N = S5_GROUPS * S5_STATE
ATT_HEADS = 4
HEAD_DIM = 64
PATTERNS = ((128, 1), (512, 4), (2048, 16))
Q_BLK = 128
REL_BUCKETS = 32
REL_MAX_DIST = 2048
N_GROUPS = 4
EXP_PER_GROUP = 8
N_EXPERTS = 32
D_EXPERT = 512
EPS = 1e-6
NEG = -1e30

VMEM_LIMIT = 56 * 1024 * 1024
LANES = 128
MOE_ROWS = 256


def _cparams(sem):
    return pltpu.CompilerParams(dimension_semantics=sem, vmem_limit_bytes=VMEM_LIMIT)


def _gelu(x):
    return 0.5 * x * (1.0 + jnp.tanh(math.sqrt(2.0 / math.pi) * (x + 0.044715 * (x * x * x))))


def _sigmoid(x):
    return 1.0 / (1.0 + jnp.exp(-x))


def _rms_rows(x):
    return x * lax.rsqrt(jnp.mean(x * x, axis=-1, keepdims=True) + EPS)


def _shift_rows(x, s, fill, row):
    return jnp.where(row >= s, pltpu.roll(x, s, 0), fill)


def _mod_body(c_ref, w_ref, b_ref, o_ref):
    c = c_ref[...]
    s = c * _sigmoid(c)
    o_ref[...] = jnp.dot(s.astype(MXU_DT), w_ref[...].astype(MXU_DT), preferred_element_type=F32) + b_ref[...]


def _modulation(c_all, mod_w, mod_b):
    nb = c_all.shape[0]
    depth = mod_w.shape[0]
    n_out = mod_w.shape[2]
    tn = D_MODEL
    return pl.pallas_call(
        _mod_body,
        grid=(depth, n_out // tn),
        in_specs=[pl.BlockSpec((nb, D_MODEL), lambda l, j: (0, 0)),
                  pl.BlockSpec((None, D_MODEL, tn), lambda l, j: (l, 0, j)),
                  pl.BlockSpec((None, 1, tn), lambda l, j: (l, 0, j))],
        out_specs=pl.BlockSpec((None, nb, tn), lambda l, j: (l, 0, j)),
        out_shape=jax.ShapeDtypeStruct((depth, nb, n_out), F32),
        compiler_params=_cparams(("arbitrary", "arbitrary")),
        name="modulation",
    )(c_all, mod_w, mod_b.reshape(depth, 1, n_out))


def _head_mean_sq(t, bones):
    sq = t * t
    hi = sq.astype(jnp.bfloat16)
    lo = (sq - hi.astype(F32)).astype(jnp.bfloat16)
    return (jnp.dot(hi, bones, preferred_element_type=F32)
            + jnp.dot(lo, bones, preferred_element_type=F32))


def _inproj_body(x_ref, sc_ref, sh_ref, g_ref, w_ref, bones_ref, qg_ref, kg_ref,
                 zm_ref, q_ref, kf_ref, vf_ref, kb_ref, vb_ref):
    x = x_ref[...]
    h = _rms_rows(x) * g_ref[...]
    h = h * (1.0 + sc_ref[...]) + sh_ref[...]
    z = jnp.dot(h.astype(MXU_DT), w_ref[...], preferred_element_type=F32)
    nm = 6 * W_GRP
    zm_ref[...] = z[:, :nm]
    q = z[:, nm:nm + W_GRP]
    k = z[:, nm + W_GRP:nm + 2 * W_GRP]
    v = z[:, nm + 2 * W_GRP:]
    bones = bones_ref[...]
    qn = (q * lax.rsqrt(_head_mean_sq(q, bones) + EPS) * qg_ref[...]) * (HEAD_DIM ** -0.5)
    kn = k * lax.rsqrt(_head_mean_sq(k, bones) + EPS) * kg_ref[...]
    q_ref[...] = qn.astype(q_ref.dtype)
    kf_ref[...] = kn
    vf_ref[...] = v
    kb_ref[...] = kn.astype(kb_ref.dtype)
    vb_ref[...] = v.astype(vb_ref.dtype)


def _mod_spec(mod, k, tm, rows_per_mod):
    if mod.ndim == 4:
        return pl.BlockSpec((None, None, 1, D_MODEL), lambda i: (k, (i * tm) // rows_per_mod, 0, 0))
    return pl.BlockSpec((None, tm, D_MODEL), lambda i: (k, i, 0))


def _in_proj(x, mod, norm_g, w_in, bones, qg, kg, *, tm, rows_per_mod):
    t = x.shape[0]
    n_in = w_in.shape[1]
    nm = 6 * W_GRP
    row = lambda i: (i, 0)
    const = lambda i: (0, 0)
    outs = [jax.ShapeDtypeStruct((t, nm), F32),
            jax.ShapeDtypeStruct((t, W_GRP), MXU_DT),
            jax.ShapeDtypeStruct((t, W_GRP), F32),
            jax.ShapeDtypeStruct((t, W_GRP), F32),
            jax.ShapeDtypeStruct((t, W_GRP), MXU_DT),
            jax.ShapeDtypeStruct((t, W_GRP), MXU_DT)]
    return pl.pallas_call(
        _inproj_body,
        grid=(t // tm,),
        in_specs=[pl.BlockSpec((tm, D_MODEL), row),
                  _mod_spec(mod, 1, tm, rows_per_mod),
                  _mod_spec(mod, 0, tm, rows_per_mod),
                  pl.BlockSpec((1, D_MODEL), const),
                  pl.BlockSpec((D_MODEL, n_in), const),
                  pl.BlockSpec((W_GRP, W_GRP), const),
                  pl.BlockSpec((1, W_GRP), const),
                  pl.BlockSpec((1, W_GRP), const)],
        out_specs=[pl.BlockSpec((tm, nm), row)] + [pl.BlockSpec((tm, W_GRP), row)] * 5,
        out_shape=outs,
        compiler_params=_cparams(("arbitrary",)),
        name="in_proj",
    )(x, mod, mod, norm_g, w_in, bones, qg, kg)


def _softplus(x):
    return jnp.maximum(x, 0.0) + jnp.log(1.0 + jnp.exp(-jnp.abs(x)))


def _mixer_ab_body(xa_ref, ga_ref, gb_ref, gc_ref, xb_ref,
                   cw_ref, cb_ref, wa_ref, ba_ref, wx_ref, bx_ref, lam_ref, sw_ref, on_a_ref, on_b_ref,
                   h0_ref, conv0_ref, sconv0_ref,
                   oa_ref, ob_ref, hn_ref, convn_ref, sconvn_ref,
                   xe_ref, pe_ref, hc_ref):
    i = pl.program_id(1)
    tl = xa_ref.shape[0]

    @pl.when(i == 0)
    def _():
        xe_ref[8 - (LRU_CONV - 1):8, :] = conv0_ref[...]
        pe_ref[8 - (SCONV_W - 1):8, :] = sconv0_ref[...]
        hc_ref[...] = h0_ref[...]

    row = lax.broadcasted_iota(jnp.int32, (tl, 1), 0)
    xa = xa_ref[...]
    xe_ref[8:, :] = xa
    cw = cw_ref[...]
    xc = cw[LRU_CONV - 1:LRU_CONV, :] * xa
    for s in range(1, LRU_CONV):
        xc = xc + cw[LRU_CONV - 1 - s:LRU_CONV - s, :] * xe_ref[8 - s:8 - s + tl, :]
    xc = xc + cb_ref[...]
    convn_ref[...] = xa[tl - (LRU_CONV - 1):, :]
    xe_ref[0:8, :] = xa[tl - 8:, :]
    xcb = xc.astype(MXU_DT)
    r = _sigmoid(jnp.dot(xcb, wa_ref[...], preferred_element_type=F32) + ba_ref[...])
    ig = _sigmoid(jnp.dot(xcb, wx_ref[...], preferred_element_type=F32) + bx_ref[...])
    log_a = (-LRU_C * r) * _softplus(-lam_ref[...])
    a = jnp.exp(log_a)
    b = jnp.sqrt(-jnp.tanh(log_a) * (a * a + 1.0)) * (ig * xc)
    s = 1
    while s < tl:
        b = a * _shift_rows(b, s, 0.0, row) + b
        a = a * _shift_rows(a, s, 1.0, row)
        s *= 2
    h = b + a * hc_ref[...]
    hc_ref[...] = h[tl - 1:, :]
    hn_ref[...] = h[tl - 1:, :]
    out_a = h * _gelu(ga_ref[...])
    oa_ref[...] = (_rms_rows(out_a) * on_a_ref[...]).astype(oa_ref.dtype)
    p = gc_ref[...] * xb_ref[...]
    pe_ref[8:, :] = p
    sw = sw_ref[...]
    yb = sw[SCONV_W - 1:SCONV_W, :] * p
    for s in range(1, SCONV_W):
        yb = yb + sw[SCONV_W - 1 - s:SCONV_W - s, :] * pe_ref[8 - s:8 - s + tl, :]
    sconvn_ref[...] = p[tl - (SCONV_W - 1):, :]
    pe_ref[0:8, :] = p[tl - 8:, :]
    out_b = gb_ref[...] * yb
    ob_ref[...] = (_rms_rows(out_b) * on_b_ref[...]).astype(ob_ref.dtype)


def _mixer_ab(zm, lp, h0, conv0, sconv0, *, tl):
    bsz, seq, _ = zm.shape
    col = lambda c: pl.BlockSpec((None, tl, W_GRP), lambda b, i, c=c: (b, i, c))
    const = lambda shp: pl.BlockSpec(shp, lambda b, i: (0,) * len(shp))
    per_b = lambda n: pl.BlockSpec((None, n, W_GRP), lambda b, i: (b, 0, 0))
    outs = [jax.ShapeDtypeStruct((bsz, seq, W_GRP), MXU_DT),
            jax.ShapeDtypeStruct((bsz, seq, W_GRP), MXU_DT),
            jax.ShapeDtypeStruct((bsz, 1, W_GRP), F32),
            jax.ShapeDtypeStruct((bsz, LRU_CONV - 1, W_GRP), F32),
            jax.ShapeDtypeStruct((bsz, SCONV_W - 1, W_GRP), F32)]
    return pl.pallas_call(
        _mixer_ab_body,
        grid=(bsz, seq // tl),
        in_specs=[col(0), col(1), col(2), col(3), col(4),
                  const((LRU_CONV, W_GRP)), const((1, W_GRP)),
                  const((W_GRP, W_GRP)), const((1, W_GRP)),
                  const((W_GRP, W_GRP)), const((1, W_GRP)),
                  const((1, W_GRP)), const((SCONV_W, W_GRP)),
                  const((1, W_GRP)), const((1, W_GRP)),
                  per_b(1), per_b(LRU_CONV - 1), per_b(SCONV_W - 1)],
        out_specs=[pl.BlockSpec((None, tl, W_GRP), lambda b, i: (b, i, 0)),
                   pl.BlockSpec((None, tl, W_GRP), lambda b, i: (b, i, 0)),
                   per_b(1), per_b(LRU_CONV - 1), per_b(SCONV_W - 1)],
        out_shape=outs,
        scratch_shapes=[pltpu.VMEM((tl + 8, W_GRP), F32),
                        pltpu.VMEM((tl + 8, W_GRP), F32),
                        pltpu.VMEM((1, W_GRP), F32)],
        compiler_params=_cparams(("arbitrary", "arbitrary")),
        name="mixer_ab",
    )(zm, zm, zm, zm, zm,
      lp['lru_conv_w'], lp['lru_conv_b'], lp['lru_wa_blk'], lp['lru_ba'], lp['lru_wx_blk'], lp['lru_bx'],
      lp['lru_lambda'], lp['sconv_w'], lp['on_a'], lp['on_b'],
      h0, conv0, sconv0)


def _s5_body(u_ref, bb_ref, cre_ref, cim_ref, ar_ref, ai_ref, d_ref, gw_ref, gb_ref, on_ref,
             h0r_ref, h0i_ref,
             o_ref, hnr_ref, hni_ref,
             tr_ref, ti_ref, hr_ref, hi_ref):
    b = pl.program_id(0)
    i = pl.program_id(1)
    tl = u_ref.shape[0]
    row = lax.broadcasted_iota(jnp.int32, (tl, 1), 0)
    ar = ar_ref[...]
    ai = ai_ref[...]

    @pl.when((b == 0) & (i == 0))
    def _():
        tr = jnp.broadcast_to(ar, (tl, S5_N))
        ti = jnp.broadcast_to(ai, (tl, S5_N))
        s = 1
        while s < tl:
            sr = _shift_rows(tr, s, 1.0, row)
            si = _shift_rows(ti, s, 0.0, row)
            tr, ti = tr * sr - ti * si, tr * si + ti * sr
            s *= 2
        tr_ref[...] = tr
        ti_ref[...] = ti

    @pl.when(i == 0)
    def _():
        hr_ref[...] = h0r_ref[...]
        hi_ref[...] = h0i_ref[...]

    u = u_ref[...]
    bu = jnp.dot(u.astype(MXU_DT), bb_ref[...], preferred_element_type=F32)
    xr = bu[:, :S5_N]
    xi = bu[:, S5_N:]
    pr, pi = ar, ai
    s = 1
    while s < tl:
        sr = _shift_rows(xr, s, 0.0, row)
        si = _shift_rows(xi, s, 0.0, row)
        xr, xi = xr + (pr * sr - pi * si), xi + (pr * si + pi * sr)
        pr, pi = pr * pr - pi * pi, 2.0 * (pr * pi)
        s *= 2
    h0r = hr_ref[...]
    h0i = hi_ref[...]
    tr = tr_ref[...]
    ti = ti_ref[...]
    hr = xr + (tr * h0r - ti * h0i)
    hi = xi + (tr * h0i + ti * h0r)
    hr_ref[...] = hr[tl - 1:, :]
    hi_ref[...] = hi[tl - 1:, :]
    hnr_ref[...] = hr[tl - 1:, :]
    hni_ref[...] = hi[tl - 1:, :]
    y = (jnp.dot(hr.astype(MXU_DT), cre_ref[...], preferred_element_type=F32)
         - jnp.dot(hi.astype(MXU_DT), cim_ref[...], preferred_element_type=F32)) + d_ref[...] * u
    g = jnp.dot(_gelu(y).astype(MXU_DT), gw_ref[...], preferred_element_type=F32) + gb_ref[...]
    out = g[:, :W_GRP] * _sigmoid(g[:, W_GRP:])
    o_ref[...] = (_rms_rows(out) * on_ref[...]).astype(o_ref.dtype)


def _s5_mixer(zm, lp, h0r, h0i, *, tl):
    bsz, seq, _ = zm.shape
    const = lambda shp: pl.BlockSpec(shp, lambda b, i: (0,) * len(shp))
    per_b = pl.BlockSpec((None, 1, S5_N), lambda b, i: (b, 0, 0))
    outs = [jax.ShapeDtypeStruct((bsz, seq, W_GRP), MXU_DT),
            jax.ShapeDtypeStruct((bsz, 1, S5_N), F32),
            jax.ShapeDtypeStruct((bsz, 1, S5_N), F32)]
    return pl.pallas_call(
        _s5_body,
        grid=(bsz, seq // tl),
        in_specs=[pl.BlockSpec((None, tl, W_GRP), lambda b, i: (b, i, 5)),
                  const((W_GRP, 2 * S5_N)), const((S5_N, W_GRP)), const((S5_N, W_GRP)),
                  const((1, S5_N)), const((1, S5_N)), const((1, W_GRP)),
                  const((W_GRP, 2 * W_GRP)), const((1, 2 * W_GRP)), const((1, W_GRP)),
                  per_b, per_b],
        out_specs=[pl.BlockSpec((None, tl, W_GRP), lambda b, i: (b, i, 0)), per_b, per_b],
        out_shape=outs,
        scratch_shapes=[pltpu.VMEM((tl, S5_N), F32), pltpu.VMEM((tl, S5_N), F32),
                        pltpu.VMEM((1, S5_N), F32), pltpu.VMEM((1, S5_N), F32)],
        compiler_params=_cparams(("arbitrary", "arbitrary")),
        name="s5_mixer",
    )(zm, lp['s5_bb'], lp['s5_cre'], lp['s5_cim'], lp['s5_abr'], lp['s5_abi'], lp['s5_d'],
      lp['s5_glu_w'], lp['s5_glu_b'], lp['on_c'], h0r, h0i)


CODE_MASKED = -1
CODE_ZERO = -2


def _bias_body(rb_ref, code_ref, o_ref):
    code = code_ref[...]
    acc = jnp.where(code == CODE_MASKED, NEG, 0.0).astype(F32)
    for c in range(REL_BUCKETS * ATT_HEADS):
        acc = jnp.where(code == c, rb_ref[c], acc)
    o_ref[...] = acc


def _bias_table(rel_bias, codes):
    rows, cols = codes.shape
    tr = max(t for t in range(8, 513, 8) if rows % t == 0)
    return pl.pallas_call(
        _bias_body,
        grid_spec=pltpu.PrefetchScalarGridSpec(
            num_scalar_prefetch=1,
            grid=(rows // tr,),
            in_specs=[pl.BlockSpec((tr, cols), lambda i, rb: (i, 0))],
            out_specs=pl.BlockSpec((tr, cols), lambda i, rb: (i, 0))),
        out_shape=jax.ShapeDtypeStruct((rows, cols), F32),
        compiler_params=_cparams(("arbitrary",)),
        name="bias_table",
    )(rel_bias.reshape(-1), jnp.asarray(codes))


def _t5_bucket(n):
    n = np.asarray(n).astype(np.int32)
    max_exact = REL_BUCKETS // 2
    nf = np.maximum(n, 1).astype(np.float32)
    large = max_exact + (np.log(nf / max_exact) / np.log(REL_MAX_DIST / max_exact)
                         * (REL_BUCKETS - max_exact)).astype(np.int32)
    large = np.minimum(large, REL_BUCKETS - 1)
    return np.where(n < max_exact, n, large).astype(np.int32)


def _prompt_bias_codes():
    i = np.arange(Q_BLK)[:, None]
    j = np.arange(2 * Q_BLK)[None, :]
    rel = Q_BLK + i - j
    out = np.zeros((len(PATTERNS), 2, ATT_HEADS, Q_BLK, 2 * Q_BLK), np.int32)
    for p, (win, dil) in enumerate(PATTERNS):
        span = win // dil
        valid = (rel >= 0) & (rel <= span)
        bucket = _t5_bucket(np.clip(rel, 0, None) * dil)
        for var in range(2):
            v = valid & ((j >= Q_BLK) | (var == 1))
            for h in range(ATT_HEADS):
                out[p, var, h] = np.where(v, bucket * ATT_HEADS + h, CODE_MASKED)
    return out.reshape(-1, 2 * Q_BLK)


S_KEYS = 136


def _sample_bias_codes(dec_seq, wb):
    out = np.full((len(PATTERNS), dec_seq, S_KEYS, W_GRP), CODE_MASKED, np.int32)
    head = np.arange(W_GRP) // HEAD_DIM
    for p, (win, dil) in enumerate(PATTERNS):
        span = win // dil
        for s in range(dec_seq):
            qpos = wb + s
            for r in range(S_KEYS):
                if r < 128:
                    if dil == 1:
                        pos = wb - 128 + r
                    else:
                        pos = wb - 128 * dil + r * dil + (s % dil)
                else:
                    pos = wb + (r - 128)
                dist = qpos - pos
                if dist >= 0 and dist % dil == 0 and dist // dil <= span:
                    out[p, s, r] = _t5_bucket(np.array(dist)) * ATT_HEADS + head
    return out.reshape(-1, W_GRP)


def _attn_p_body(*refs, n_prev):
    q_ref, kp_ref, kc_ref, vp_ref, vc_ref, bias_ref = refs[:6]
    prev = refs[6:6 + 2 * n_prev]
    rest = refs[6 + 2 * n_prev:]
    q = q_ref[...]
    k2 = jnp.concatenate([kp_ref[...], kc_ref[...]], axis=0)
    v2 = jnp.concatenate([vp_ref[...], vc_ref[...]], axis=0)
    lane = lax.broadcasted_iota(jnp.int32, (1, W_GRP), 1)
    o = jnp.zeros((Q_BLK, W_GRP), F32)
    lse = jnp.zeros((Q_BLK, W_GRP), F32)
    for h in range(ATT_HEADS):
        hm = (lane >= h * HEAD_DIM) & (lane < (h + 1) * HEAD_DIM)
        qh = jnp.where(hm, q, jnp.zeros_like(q))
        s = lax.dot_general(qh, k2, (((1,), (1,)), ((), ())), preferred_element_type=F32) + bias_ref[h]
        m = jnp.max(s, axis=-1, keepdims=True)
        pr = jnp.exp(s - m)
        den = jnp.sum(pr, axis=-1, keepdims=True)
        oh = jnp.dot(pr.astype(MXU_DT), v2, preferred_element_type=F32) / den
        o = jnp.where(hm, oh, o)
        lse = jnp.where(hm, m + jnp.log(den), lse)
    if n_prev == 0:
        o_ref, lse_ref = rest
        o_ref[...] = o
        lse_ref[...] = lse
    else:
        on_ref, out_ref = rest
        os_ = [o] + [prev[2 * t][...] for t in range(n_prev)]
        ls_ = [lse] + [prev[2 * t + 1][...] for t in range(n_prev)]
        mx = functools.reduce(jnp.maximum, ls_)
        ws = [jnp.exp(l_ - mx) for l_ in ls_]
        num = functools.reduce(lambda a_, b_: a_ + b_, [w_ * o_ for w_, o_ in zip(ws, os_)])
        den = functools.reduce(lambda a_, b_: a_ + b_, ws)
        merged = num / den
        out_ref[...] = (_rms_rows(merged) * on_ref[...]).astype(out_ref.dtype)


def _attn_prompt_pattern(q, k, v, bias, dil, prev=(), on_d=None):
    bsz, seq, _ = q.shape
    md = seq // dil
    nb = md // Q_BLK
    view = lambda t: t.reshape(bsz, md, dil * W_GRP)
    cur = pl.BlockSpec((None, Q_BLK, W_GRP), lambda b, r, n: (b, n, r))
    prv = pl.BlockSpec((None, Q_BLK, W_GRP), lambda b, r, n: (b, jnp.maximum(n - 1, 0), r))
    bsp = pl.BlockSpec((None, ATT_HEADS, Q_BLK, 2 * Q_BLK), lambda b, r, n: (jnp.minimum(n, 1), 0, 0, 0))
    n_prev = len(prev) // 2
    in_specs = [cur, prv, cur, prv, cur, bsp] + [cur] * len(prev)
    args = [view(q), view(k), view(k), view(v), view(v), bias] + [view(t) for t in prev]
    if n_prev == 0:
        out_specs = [cur, cur]
        out_shape = [jax.ShapeDtypeStruct((bsz, md, dil * W_GRP), F32)] * 2
    else:
        in_specs.append(pl.BlockSpec((1, W_GRP), lambda b, r, n: (0, 0)))
        args.append(on_d)
        out_specs = cur
        out_shape = jax.ShapeDtypeStruct((bsz, md, dil * W_GRP), MXU_DT)
    res = pl.pallas_call(
        functools.partial(_attn_p_body, n_prev=n_prev),
        grid=(bsz, dil, nb),
        in_specs=in_specs, out_specs=out_specs, out_shape=out_shape,
        compiler_params=_cparams(("arbitrary", "arbitrary", "arbitrary")),
        name=f"attn_prompt_d{dil}",
    )(*args)
    if n_prev == 0:
        return tuple(t.reshape(bsz, seq, W_GRP) for t in res)
    return res.reshape(bsz, seq, W_GRP)


def _attn_prompt(q, k, v, bias_all, on_d):
    prev = ()
    for p, (_, dil) in enumerate(PATTERNS):
        if p + 1 < len(PATTERNS):
            prev = prev + _attn_prompt_pattern(q, k, v, bias_all[p], dil)
        else:
            return _attn_prompt_pattern(q, k, v, bias_all[p], dil, prev=prev, on_d=on_d)


def _attn_s_body(q_ref, kn_ref, vn_ref, k1_ref, k4_ref, k16_ref, v1_ref, v4_ref, v16_ref,
                 bias_ref, ones_ref, on_ref, o_ref):
    dec = q_ref.shape[0]
    rnd = lambda t: t.astype(MXU_DT).astype(F32)
    q = q_ref[...].astype(F32)
    kn = rnd(kn_ref[...])
    vn = rnd(vn_ref[...])
    ones_blk = ones_ref[...]
    rows = []
    for s in range(dec):
        qs = q[s:s + 1, :]
        os_, ls_ = [], []
        for p, (_, dil) in enumerate(PATTERNS):
            kref, vref = ((k1_ref, v1_ref), (k4_ref, v4_ref), (k16_ref, v16_ref))[p]
            c0 = (s % dil) * W_GRP
            kcat = jnp.concatenate([rnd(kref[:, c0:c0 + W_GRP]), kn], axis=0)
            vcat = jnp.concatenate([rnd(vref[:, c0:c0 + W_GRP]), vn], axis=0)
            prod = kcat * qs
            hi = prod.astype(jnp.bfloat16)
            lo = (prod - hi.astype(F32)).astype(jnp.bfloat16)
            sc = (jnp.dot(hi, ones_blk, preferred_element_type=F32)
                  + jnp.dot(lo, ones_blk, preferred_element_type=F32)) + bias_ref[p, s]
            m = jnp.max(sc, axis=0, keepdims=True)
            pr = jnp.exp(sc - m)
            den = jnp.sum(pr, axis=0, keepdims=True)
            os_.append(jnp.sum(rnd(pr) * vcat, axis=0, keepdims=True) / den)
            ls_.append(m + jnp.log(den))
        mx = functools.reduce(jnp.maximum, ls_)
        ws = [jnp.exp(l_ - mx) for l_ in ls_]
        num = functools.reduce(lambda a_, b_: a_ + b_, [w_ * o_ for w_, o_ in zip(ws, os_)])
        rows.append(num / functools.reduce(lambda a_, b_: a_ + b_, ws))
    merged = jnp.concatenate(rows, axis=0)
    o_ref[...] = (_rms_rows(merged) * on_ref[...]).astype(o_ref.dtype)


def _attn_sample(q, kn, vn, cache_k, cache_v, l, bias_s, ones_blk, on_d):
    bsz, dec, _ = q.shape
    depth, _, wb = cache_k.shape[:3]
    assert wb % 128 == 0 and wb >= 128 * PATTERNS[-1][1] and dec <= 8
    per_b = pl.BlockSpec((None, dec, W_GRP), lambda b: (b, 0, 0))

    def cache_view(c, dil):
        keep = min(dil, 8)
        cv = c[:, :, wb - 128 * dil:].reshape(depth, bsz, 128, dil, W_GRP)[:, :, :, :keep]
        cv = cv.reshape(depth, bsz, 128, keep * W_GRP)
        return cv, pl.BlockSpec((None, None, 128, keep * W_GRP), lambda b: (l, b, 0, 0))

    views = [cache_view(c, dil) for c in (cache_k, cache_v) for (_, dil) in PATTERNS]
    return pl.pallas_call(
        _attn_s_body,
        grid=(bsz,),
        in_specs=[per_b, per_b, per_b] + [sp for _, sp in views] + [
            pl.BlockSpec((len(PATTERNS), dec, S_KEYS, W_GRP), lambda b: (0, 0, 0, 0)),
            pl.BlockSpec((W_GRP, W_GRP), lambda b: (0, 0)),
            pl.BlockSpec((1, W_GRP), lambda b: (0, 0))],
        out_specs=per_b,
        out_shape=jax.ShapeDtypeStruct((bsz, dec, W_GRP), MXU_DT),
        compiler_params=_cparams(("arbitrary",)),
        name="attn_sample",
    )(q, kn, vn, *[cv for cv, _ in views], bias_s, ones_blk, on_d)


ROUTER_LANES = 128
ROW_TILE = D_MODEL // LANES


def _store_row_tiles(ref, val):
    n = val.shape[0]
    for j in range(ROW_TILE):
        ref[pl.ds(j, n, stride=ROW_TILE), :] = val[:, j * LANES:(j + 1) * LANES]


def _load_row_tiles(ref):
    n = ref.shape[0] // ROW_TILE
    return jnp.concatenate([ref[pl.ds(j, n, stride=ROW_TILE), :] for j in range(ROW_TILE)], axis=1)


def _post_mix_body(x_ref, a_ref, b_ref, c_ref, d_ref, w_ref, g1_ref, sc_ref, sh_ref, g_ref,
                   wr_ref, br_ref, tri_ref, cnt_in_ref,
                   x1_ref, h2_ref, eid_ref, gate_ref, rank_ref, cnt_ref,
                   run_ref):
    i = pl.program_id(0)

    @pl.when(i == 0)
    def _():
        run_ref[...] = cnt_in_ref[...]

    mix = jnp.concatenate([a_ref[...], b_ref[...], c_ref[...], d_ref[...]], axis=1)
    y = jnp.dot(mix, w_ref[...], preferred_element_type=F32)
    x1 = x_ref[...] + g1_ref[...] * y
    x1_ref[...] = x1
    h2 = _rms_rows(x1) * g_ref[...]
    h2 = h2 * (1.0 + sc_ref[...]) + sh_ref[...]
    _store_row_tiles(h2_ref, h2)
    logits = jnp.dot(h2.astype(MXU_DT), wr_ref[...], preferred_element_type=F32) + br_ref[...]
    tm = logits.shape[0]
    lane = lax.broadcasted_iota(jnp.int32, (tm, ROUTER_LANES), 1)
    big = jnp.int32(10 ** 6)
    is_g = lane < N_GROUPS
    gl = jnp.where(is_g, logits, -jnp.inf)
    gmax = jnp.max(gl, axis=-1, keepdims=True)
    gsel = jnp.min(jnp.where(gl == gmax, lane, big), axis=-1, keepdims=True)
    gprob = 1.0 / jnp.sum(jnp.where(is_g, jnp.exp(logits - gmax), 0.0), axis=-1, keepdims=True)
    lo_lane = N_GROUPS + gsel * EXP_PER_GROUP
    in_grp = (lane >= lo_lane) & (lane < lo_lane + EXP_PER_GROUP)
    el = jnp.where(in_grp, logits, -jnp.inf)
    v1 = jnp.max(el, axis=-1, keepdims=True)
    i1 = jnp.min(jnp.where(el == v1, lane, big), axis=-1, keepdims=True)
    el2 = jnp.where(lane == i1, -jnp.inf, el)
    v2 = jnp.max(el2, axis=-1, keepdims=True)
    i2 = jnp.min(jnp.where(el2 == v2, lane, big), axis=-1, keepdims=True)
    e2w = jnp.exp(v2 - v1)
    gate1 = (1.0 / (1.0 + e2w)) * gprob
    gate2 = (e2w / (1.0 + e2w)) * gprob
    e1 = i1 - N_GROUPS
    e2 = i2 - N_GROUPS
    oh1 = lane == e1
    oh2 = lane == e2
    both = jnp.where(oh1 | oh2, 1.0, 0.0)
    before = jnp.dot(tri_ref[...], both.astype(jnp.bfloat16), preferred_element_type=F32) + run_ref[...]
    r1 = jnp.sum(jnp.where(oh1, before, 0.0), axis=-1, keepdims=True).astype(jnp.int32)
    r2 = jnp.sum(jnp.where(oh2, before, 0.0), axis=-1, keepdims=True).astype(jnp.int32)
    run = run_ref[...] + jnp.sum(both, axis=0, keepdims=True)
    run_ref[...] = run
    cnt_ref[...] = run
    eid_ref[...] = jnp.where(lane == 0, e1, jnp.where(lane == 1, e2, 0))
    gate_ref[...] = jnp.where(lane == 0, gate1, jnp.where(lane == 1, gate2, 0.0))
    rank_ref[...] = jnp.where(lane == 0, r1, jnp.where(lane == 1, r2, 0))


def _post_mix(x, pieces, mod, lp, cnt_in, *, tm, rows_per_mod):
    t = x.shape[0]
    row = lambda i: (i, 0)
    const = lambda i: (0, 0)
    piece = pl.BlockSpec((tm, W_GRP), row)
    wide = pl.BlockSpec((tm, D_MODEL), row)
    lanes = pl.BlockSpec((tm, ROUTER_LANES), row)
    outs = [jax.ShapeDtypeStruct((t, D_MODEL), F32), jax.ShapeDtypeStruct((t * ROW_TILE, LANES), F32),
            jax.ShapeDtypeStruct((t, ROUTER_LANES), jnp.int32), jax.ShapeDtypeStruct((t, ROUTER_LANES), F32),
            jax.ShapeDtypeStruct((t, ROUTER_LANES), jnp.int32), jax.ShapeDtypeStruct((1, ROUTER_LANES), F32)]
    return pl.pallas_call(
        _post_mix_body,
        grid=(t // tm,),
        in_specs=[wide, piece, piece, piece, piece,
                  pl.BlockSpec((D_MODEL, D_MODEL), const),
                  _mod_spec(mod, 2, tm, rows_per_mod), _mod_spec(mod, 4, tm, rows_per_mod),
                  _mod_spec(mod, 3, tm, rows_per_mod),
                  pl.BlockSpec((1, D_MODEL), const),
                  pl.BlockSpec((D_MODEL, ROUTER_LANES), const), pl.BlockSpec((1, ROUTER_LANES), const),
                  pl.BlockSpec((tm, tm), const), pl.BlockSpec((1, ROUTER_LANES), const)],
        out_specs=[wide, pl.BlockSpec((tm * ROW_TILE, LANES), row), lanes, lanes, lanes,
                   pl.BlockSpec((1, ROUTER_LANES), const)],
        out_shape=outs,
        scratch_shapes=[pltpu.VMEM((1, ROUTER_LANES), F32)],
        compiler_params=_cparams(("arbitrary",)),
        name="post_mix",
    )(x, *pieces, lp['w_out'], mod, mod, mod, lp['norm_ffn'], lp['router_w'], lp['router_b'], lp['tri'], cnt_in)


def _row_copy(src_ref, s, dst_ref, d, sem):
    return pltpu.make_async_copy(src_ref.at[pl.ds(pl.multiple_of(s * ROW_TILE, ROW_TILE), ROW_TILE)],
                                 dst_ref.at[pl.ds(pl.multiple_of(d * ROW_TILE, ROW_TILE), ROW_TILE)], sem)


def _dispatch_body(dest_ref, h_ref, xs_in_ref, xs_ref, sem):
    del xs_in_ref
    tm = h_ref.shape[0] // ROW_TILE
    base = pl.program_id(0) * (2 * tm)

    def issue(t, c):
        _row_copy(h_ref, t, xs_ref, dest_ref[base + 2 * t], sem).start()
        _row_copy(h_ref, t, xs_ref, dest_ref[base + 2 * t + 1], sem).start()
        return c

    lax.fori_loop(0, tm, issue, 0)
    for _ in range(2):
        pltpu.make_async_copy(h_ref, xs_ref.at[pl.ds(0, tm * ROW_TILE)], sem).wait()


def _dispatch(h2, dest_flat, xs, *, tm):
    t = h2.shape[0] // ROW_TILE
    return pl.pallas_call(
        _dispatch_body,
        grid_spec=pltpu.PrefetchScalarGridSpec(
            num_scalar_prefetch=1,
            grid=(t // tm,),
            in_specs=[pl.BlockSpec((tm * ROW_TILE, LANES), lambda i, d: (i, 0)),
                      pl.BlockSpec(memory_space=pl.ANY)],
            out_specs=pl.BlockSpec(memory_space=pl.ANY),
            scratch_shapes=[pltpu.SemaphoreType.DMA(())]),
        out_shape=jax.ShapeDtypeStruct(xs.shape, xs.dtype),
        input_output_aliases={2: 0},
        compiler_params=_cparams(("arbitrary",)),
        name="moe_dispatch",
    )(dest_flat, h2, xs)


def _experts_body(blk_e_ref, n_used_ref, xs_ref, wg_ref, wu_ref, wd_ref, o_ref, wgb_ref, wub_ref, wdb_ref):
    i = pl.program_id(0)
    e = blk_e_ref[i]
    e_prev = blk_e_ref[jnp.maximum(i - 1, 0)]

    @pl.when((i == 0) | (e != e_prev))
    def _():
        wgb_ref[...] = wg_ref[...].astype(MXU_DT)
        wub_ref[...] = wu_ref[...].astype(MXU_DT)
        wdb_ref[...] = wd_ref[...].astype(MXU_DT)

    @pl.when(i < n_used_ref[0])
    def _():
        x = _load_row_tiles(xs_ref).astype(MXU_DT)
        g = jnp.dot(x, wgb_ref[...], preferred_element_type=F32)
        u = jnp.dot(x, wub_ref[...], preferred_element_type=F32)
        hmid = (g * _sigmoid(g)) * u
        _store_row_tiles(o_ref, jnp.dot(hmid.astype(MXU_DT), wdb_ref[...], preferred_element_type=F32))

    @pl.when(i >= n_used_ref[0])
    def _():
        o_ref[...] = jnp.zeros_like(o_ref)


def _experts(xs, blk_e, n_used, wg, wu, wd, l):
    n_blk = xs.shape[0] // (MOE_ROWS * ROW_TILE)
    xmap = lambda i, be, nu: (jnp.minimum(i, nu[0] - 1), 0)
    return pl.pallas_call(
        _experts_body,
        grid_spec=pltpu.PrefetchScalarGridSpec(
            num_scalar_prefetch=2,
            grid=(n_blk,),
            in_specs=[pl.BlockSpec((MOE_ROWS * ROW_TILE, LANES), xmap),
                      pl.BlockSpec((None, None, D_MODEL, D_EXPERT), lambda i, be, nu: (l, be[i], 0, 0)),
                      pl.BlockSpec((None, None, D_MODEL, D_EXPERT), lambda i, be, nu: (l, be[i], 0, 0)),
                      pl.BlockSpec((None, None, D_EXPERT, D_MODEL), lambda i, be, nu: (l, be[i], 0, 0))],
            out_specs=pl.BlockSpec((MOE_ROWS * ROW_TILE, LANES), lambda i, be, nu: (i, 0)),
            scratch_shapes=[pltpu.VMEM((D_MODEL, D_EXPERT), MXU_DT), pltpu.VMEM((D_MODEL, D_EXPERT), MXU_DT),
                            pltpu.VMEM((D_EXPERT, D_MODEL), MXU_DT)]),
        out_shape=jax.ShapeDtypeStruct(xs.shape, F32),
        compiler_params=_cparams(("arbitrary",)),
        name="moe_experts",
    )(blk_e, n_used, xs, wg, wu, wd)


def _combine_body(dest_ref, ys_ref, x1_ref, gate_ref, g2_ref, x2_ref, buf0_ref, buf1_ref, sem):
    tm = x1_ref.shape[0]
    base = pl.program_id(0) * (2 * tm)

    def issue(t, c):
        _row_copy(ys_ref, dest_ref[base + 2 * t], buf0_ref, t, sem).start()
        _row_copy(ys_ref, dest_ref[base + 2 * t + 1], buf1_ref, t, sem).start()
        return c

    lax.fori_loop(0, tm, issue, 0)
    for buf_ref in (buf0_ref, buf1_ref):
        pltpu.make_async_copy(ys_ref.at[pl.ds(0, tm * ROW_TILE)], buf_ref, sem).wait()
    gate = gate_ref[...]
    y = _load_row_tiles(buf0_ref) * gate[:, 0:1] + _load_row_tiles(buf1_ref) * gate[:, 1:2]
    x2_ref[...] = x1_ref[...] + g2_ref[...] * y


def _combine(ys, dest_flat, x1, gate, mod, *, tm, rows_per_mod):
    t = x1.shape[0]
    if mod.ndim == 4:
        g2_spec = pl.BlockSpec((None, None, 1, D_MODEL), lambda i, d: (5, (i * tm) // rows_per_mod, 0, 0))
    else:
        g2_spec = pl.BlockSpec((None, tm, D_MODEL), lambda i, d: (5, i, 0))
    return pl.pallas_call(
        _combine_body,
        grid_spec=pltpu.PrefetchScalarGridSpec(
            num_scalar_prefetch=1,
            grid=(t // tm,),
            in_specs=[pl.BlockSpec(memory_space=pl.ANY),
                      pl.BlockSpec((tm, D_MODEL), lambda i, d: (i, 0)),
                      pl.BlockSpec((tm, ROUTER_LANES), lambda i, d: (i, 0)),
                      g2_spec],
            out_specs=pl.BlockSpec((tm, D_MODEL), lambda i, d: (i, 0)),
            scratch_shapes=[pltpu.VMEM((tm * ROW_TILE, LANES), F32), pltpu.VMEM((tm * ROW_TILE, LANES), F32),
                            pltpu.SemaphoreType.DMA(())]),
        out_shape=jax.ShapeDtypeStruct((t, D_MODEL), F32),
        compiler_params=_cparams(("arbitrary",)),
        name="moe_combine",
    )(dest_flat, ys, x1, gate, mod)


def _routing_tables(cnt, eids, ranks, n_blk):
    counts = cnt[0, :N_EXPERTS].astype(jnp.int32)
    padded = (counts + MOE_ROWS - 1) // MOE_ROWS * MOE_ROWS
    ends = jnp.cumsum(padded)
    starts = ends - padded
    dests = [(starts[e[:, :2]] + r[:, :2]).reshape(-1) for e, r in zip(eids, ranks)]
    blk_start = jnp.arange(n_blk, dtype=jnp.int32) * MOE_ROWS
    blk_e = jnp.minimum(jnp.sum((ends[None, :] <= blk_start[:, None]).astype(jnp.int32), axis=1), N_EXPERTS - 1)
    n_used = (ends[-1] // MOE_ROWS).astype(jnp.int32).reshape(1)
    return dests, blk_e, n_used


def _block_diag(w):
    g, r, c = w.shape
    eye = jnp.eye(g, dtype=w.dtype)
    return (eye[:, None, :, None] * w[:, :, None, :]).reshape(g * r, g * c)


def _s5_discretise(log_dt, a_re, a_im, b_re, b_im):
    step = jnp.exp(log_dt)[:, None]
    mag = jnp.exp(a_re * step)
    ang = a_im * step
    abr = mag * jnp.cos(ang)
    abi = mag * jnp.sin(ang)
    den = a_re * a_re + a_im * a_im
    zr = ((abr - 1.0) * a_re + abi * a_im) / den
    zi = (abi * a_re - (abr - 1.0) * a_im) / den
    bbr = zr[..., None] * b_re - zi[..., None] * b_im
    bbi = zr[..., None] * b_im + zi[..., None] * b_re
    return abr, abi, bbr, bbi


def _prep_layer(P, l):
    row = lambda a: a.reshape(1, -1)
    abr, abi, bbr, bbi = _s5_discretise(P['s5_log_dt'][l], P['s5_a_re'][l], P['s5_a_im'][l],
                                        P['s5_b_re'][l], P['s5_b_im'][l])
    on = P['out_norm'][l]
    bones = _block_diag(jnp.full((ATT_HEADS, HEAD_DIM, HEAD_DIM), 1.0 / HEAD_DIM, F32)).astype(jnp.bfloat16)
    return {
        'norm_mix': row(P['norm_mix'][l]), 'norm_ffn': row(P['norm_ffn'][l]),
        'w_in': P['w_in'][l].astype(MXU_DT), 'w_out': P['w_out'][l].astype(MXU_DT),
        'bones': bones,
        'qg': row(jnp.tile(P['q_norm'][l], ATT_HEADS)), 'kg': row(jnp.tile(P['k_norm'][l], ATT_HEADS)),
        'lru_conv_w': P['lru_conv_w'][l], 'lru_conv_b': row(P['lru_conv_b'][l]),
        'lru_wa_blk': _block_diag(P['lru_wa'][l]).astype(MXU_DT), 'lru_ba': row(P['lru_ba'][l]),
        'lru_wx_blk': _block_diag(P['lru_wx'][l]).astype(MXU_DT), 'lru_bx': row(P['lru_bx'][l]),
        'lru_lambda': row(P['lru_lambda'][l]), 'sconv_w': P['sconv_w'][l],
        'on_a': row(on[0:W_GRP]), 'on_b': row(on[W_GRP:2 * W_GRP]),
        'on_c': row(on[2 * W_GRP:3 * W_GRP]), 'on_d': row(on[3 * W_GRP:]),
        's5_bb': jnp.concatenate([_block_diag(bbr.transpose(0, 2, 1)), _block_diag(bbi.transpose(0, 2, 1))],
                                 axis=1).astype(MXU_DT),
        's5_cre': _block_diag(P['s5_c_re'][l].transpose(0, 2, 1)).astype(MXU_DT),
        's5_cim': _block_diag(P['s5_c_im'][l].transpose(0, 2, 1)).astype(MXU_DT),
        's5_abr': row(abr), 's5_abi': row(abi), 's5_d': row(P['s5_d'][l]),
        's5_glu_w': P['s5_glu_w'][l].astype(MXU_DT), 's5_glu_b': row(P['s5_glu_b'][l]),
        'router_w': jnp.zeros((D_MODEL, ROUTER_LANES), F32)
                       .at[:, :N_GROUPS].set(P['router_g_w'][l])
                       .at[:, N_GROUPS:N_GROUPS + N_EXPERTS].set(P['router_e_w'][l]).astype(MXU_DT),
        'router_b': jnp.zeros((1, ROUTER_LANES), F32)
                       .at[0, :N_GROUPS].set(P['router_g_b'][l])
                       .at[0, N_GROUPS:N_GROUPS + N_EXPERTS].set(P['router_e_b'][l]),
    }


TOKEN_TILE = 256
SEQ_TILE = 256

_PARAM_NAMES = ('rel_bias', 'mod_w', 'mod_b', 'norm_mix', 'norm_ffn', 'w_in', 'lru_conv_w', 'lru_conv_b',
                'lru_wa', 'lru_ba', 'lru_wx', 'lru_bx', 'lru_lambda', 'sconv_w', 's5_log_dt', 's5_a_re',
                's5_a_im', 's5_b_re', 's5_b_im', 's5_c_re', 's5_c_im', 's5_d', 's5_glu_w', 's5_glu_b',
                'q_norm', 'k_norm', 'out_norm', 'w_out', 'router_g_w', 'router_g_b', 'router_e_w',
                'router_e_b', 'moe_w_gate', 'moe_w_up', 'moe_w_down')


def _mixers(x, mod, lp, st, attn_fn, *, batch, seq, tm, tl, rows_per_mod):
    zm, q, kf, vf, kb, vb = _in_proj(x, mod, lp['norm_mix'], lp['w_in'], lp['bones'], lp['qg'], lp['kg'],
                                     tm=tm, rows_per_mod=rows_per_mod)
    zm3 = zm.reshape(batch, seq, 6 * W_GRP)
    oa, ob, lru_h, lru_conv, sconv = _mixer_ab(zm3, lp, st['lru_h'], st['lru_conv'], st['sconv'], tl=tl)
    oc, s5_re, s5_im = _s5_mixer(zm3, lp, st['s5_re'], st['s5_im'], tl=tl)
    r3 = lambda t: t.reshape(batch, seq, W_GRP)
    od = attn_fn(r3(q), r3(kf), r3(vf), r3(kb), r3(vb))
    flat = lambda t: t.reshape(batch * seq, W_GRP)
    new_st = {'lru_h': lru_h[:, 0], 'lru_conv': lru_conv, 'sconv': sconv,
              's5_re': s5_re.reshape(batch, S5_GROUPS, S5_STATE), 's5_im': s5_im.reshape(batch, S5_GROUPS, S5_STATE),
              'win_k': kf.reshape(batch, seq, ATT_HEADS, HEAD_DIM), 'win_v': vf.reshape(batch, seq, ATT_HEADS, HEAD_DIM)}
    return [flat(oa), flat(ob), flat(oc), flat(od)], new_st


def kernel(x_prompt, x_sample, c_prompt, c_sample, state_lru_h, state_lru_conv, state_sconv, state_s5_re, state_s5_im, cache_win_k, cache_win_v, rel_bias, mod_w, mod_b, norm_mix, norm_ffn, w_in, lru_conv_w, lru_conv_b, lru_wa, lru_ba, lru_wx, lru_bx, lru_lambda, sconv_w, s5_log_dt, s5_a_re, s5_a_im, s5_b_re, s5_b_im, s5_c_re, s5_c_im, s5_d, s5_glu_w, s5_glu_b, q_norm, k_norm, out_norm, w_out, router_g_w, router_g_b, router_e_w, router_e_b, moe_w_gate, moe_w_up, moe_w_down):
    P = dict(zip(_PARAM_NAMES, (rel_bias, mod_w, mod_b, norm_mix, norm_ffn, w_in, lru_conv_w, lru_conv_b,
                                lru_wa, lru_ba, lru_wx, lru_bx, lru_lambda, sconv_w, s5_log_dt, s5_a_re,
                                s5_a_im, s5_b_re, s5_b_im, s5_c_re, s5_c_im, s5_d, s5_glu_w, s5_glu_b,
                                q_norm, k_norm, out_norm, w_out, router_g_w, router_g_b, router_e_w,
                                router_e_b, moe_w_gate, moe_w_up, moe_w_down)))
    bp, seq, d = x_prompt.shape
    bs, dec, _ = x_sample.shape
    depth = mod_w.shape[0]
    tp, ts = bp * seq, bs * dec
    wb = cache_win_k.shape[2]
    wp = min(PATTERNS[-1][0], seq)
    tm_p = min(TOKEN_TILE, tp)
    tm_s = min(TOKEN_TILE, ts)
    tl_p = min(SEQ_TILE, seq)

    nc = -(-(bp + bs) // 8) * 8
    c_all = jnp.zeros((nc, d), F32).at[:bp].set(c_prompt).at[bp:bp + bs].set(c_sample)
    mod_all = _modulation(c_all, mod_w, mod_b)
    bias_p = _bias_table(rel_bias, _prompt_bias_codes()).reshape(len(PATTERNS), 2, ATT_HEADS, Q_BLK, 2 * Q_BLK)
    bias_s = _bias_table(rel_bias, _sample_bias_codes(dec, wb)).reshape(len(PATTERNS), dec, S_KEYS, W_GRP)
    ones_blk = _block_diag(jnp.ones((ATT_HEADS, HEAD_DIM, HEAD_DIM), jnp.bfloat16))
    tri = jnp.asarray(np.tril(np.ones((TOKEN_TILE, TOKEN_TILE), np.float32), -1), jnp.bfloat16)
    n_blk = (2 * (tp + ts)) // MOE_ROWS + N_EXPERTS

    zero_st = {'lru_h': jnp.zeros((bp, 1, W_GRP), F32), 'lru_conv': jnp.zeros((bp, LRU_CONV - 1, W_GRP), F32),
               'sconv': jnp.zeros((bp, SCONV_W - 1, W_GRP), F32),
               's5_re': jnp.zeros((bp, 1, S5_N), F32), 's5_im': jnp.zeros((bp, 1, S5_N), F32)}
    names = ('lru_h', 'lru_conv', 'sconv', 's5_re', 's5_im', 'win_k', 'win_v')
    acc_p = {n: [] for n in names}
    acc_s = {n: [] for n in names}
    xp = x_prompt.reshape(tp, d)
    xs = x_sample.reshape(ts, d)
    for l in range(depth):
        lp = _prep_layer(P, l)
        lp['tri'] = tri
        m6 = mod_all[l].reshape(nc, 6, d).transpose(1, 0, 2)
        mod_p = m6[:, :bp].reshape(6, bp, 1, d)
        mod_s = jnp.repeat(m6[:, bp:bp + bs], dec, axis=1)
        attn_p = lambda q, kf, vf, kb, vb: _attn_prompt(q, kb, vb, bias_p, lp['on_d'])
        pieces_p, st_p = _mixers(xp, mod_p, lp, zero_st, attn_p, batch=bp, seq=seq, tm=tm_p, tl=tl_p,
                                 rows_per_mod=seq)
        st_p['win_k'] = st_p['win_k'][:, seq - wp:]
        st_p['win_v'] = st_p['win_v'][:, seq - wp:]
        cnt0 = jnp.zeros((1, ROUTER_LANES), F32)
        x1p, h2p, eid_p, gate_p, rank_p, cnt = _post_mix(xp, pieces_p, mod_p, lp, cnt0, tm=tm_p, rows_per_mod=seq)
        samp_st = {'lru_h': state_lru_h[l][:, None], 'lru_conv': state_lru_conv[l], 'sconv': state_sconv[l],
                   's5_re': state_s5_re[l].reshape(bs, 1, S5_N), 's5_im': state_s5_im[l].reshape(bs, 1, S5_N)}
        attn_s = lambda q, kf, vf, kb, vb: _attn_sample(q, kf, vf, cache_win_k, cache_win_v, l, bias_s,
                                                        ones_blk, lp['on_d'])
        pieces_s, st_s = _mixers(xs, mod_s, lp, samp_st, attn_s, batch=bs, seq=dec, tm=tm_s, tl=dec,
                                 rows_per_mod=dec)
        x1s, h2s, eid_s, gate_s, rank_s, cnt = _post_mix(xs, pieces_s, mod_s, lp, cnt, tm=tm_s, rows_per_mod=dec)
        dests, blk_e, n_used = _routing_tables(cnt, [eid_p, eid_s], [rank_p, rank_s], n_blk)
        slots = jnp.zeros((n_blk * MOE_ROWS * ROW_TILE, LANES), F32)
        slots = _dispatch(h2p, dests[0], slots, tm=tm_p)
        slots = _dispatch(h2s, dests[1], slots, tm=tm_s)
        ys = _experts(slots, blk_e, n_used, moe_w_gate, moe_w_up, moe_w_down, l)
        xp = _combine(ys, dests[0], x1p, gate_p, mod_p, tm=tm_p, rows_per_mod=seq)
        xs = _combine(ys, dests[1], x1s, gate_s, mod_s, tm=tm_s, rows_per_mod=dec)
        for n in names:
            acc_p[n].append(st_p[n])
            acc_s[n].append(st_s[n])
    new_p = {n: jnp.stack(acc_p[n], axis=0) for n in names}
    new_s = {n: jnp.stack(acc_s[n], axis=0) for n in names}
    return (xp.reshape(bp, seq, d), xs.reshape(bs, dec, d),
            new_p['lru_h'], new_p['lru_conv'], new_p['sconv'], new_p['s5_re'], new_p['s5_im'],
            new_p['win_k'], new_p['win_v'],
            new_s['lru_h'], new_s['lru_conv'], new_s['sconv'], new_s['s5_re'], new_s['s5_im'],
            new_s['win_k'], new_s['win_v'])
```

```python
import functools
import math

import numpy as np
import jax
import jax.numpy as jnp
from jax import lax
from jax.experimental import pallas as pl
from jax.experimental.pallas import tpu as pltpu

F32 = jnp.float32
MXU_DT = jnp.bfloat16
HIGHEST = lax.Precision.HIGHEST

D_MODEL = 1024
DEPTH = 4
W_GRP = 256
N_Z = 9
LRU_HEADS = 4
LRU_CONV = 4
LRU_C = 8.0
SCONV_W = 3
S5_CH = 16
S5_GROUPS = 16
S5_STATE = 64
S5_N = S5_GROUPS * S5_STATE
ATT_HEADS = 4
HEAD_DIM = 64
PATTERNS = ((128, 1), (512, 4), (2048, 16))
Q_BLK = 128
REL_BUCKETS = 32
REL_MAX_DIST = 2048
N_GROUPS = 4
EXP_PER_GROUP = 8
N_EXPERTS = 32
D_EXPERT = 512
EPS = 1e-6
NEG = -1e30

VMEM_LIMIT = 56 * 1024 * 1024
LANES = 128
MOE_ROWS = 256
SEGS = 8


def _cparams(sem):
    return pltpu.CompilerParams(dimension_semantics=sem, vmem_limit_bytes=VMEM_LIMIT)


def _gelu(x):
    return 0.5 * x * (1.0 + jnp.tanh(math.sqrt(2.0 / math.pi) * (x + 0.044715 * (x * x * x))))


def _sigmoid(x):
    return 1.0 / (1.0 + jnp.exp(-x))


def _rms_rows(x):
    return x * lax.rsqrt(jnp.mean(x * x, axis=-1, keepdims=True) + EPS)


def _shift_rows(x, s, fill, row):
    return jnp.where(row >= s, pltpu.roll(x, s, 0), fill)


def _mod_body(c_ref, w_ref, b_ref, o_ref):
    c = c_ref[...]
    s = c * _sigmoid(c)
    o_ref[...] = jnp.dot(s.astype(MXU_DT), w_ref[...].astype(MXU_DT), preferred_element_type=F32) + b_ref[...]


def _modulation(c_all, mod_w, mod_b):
    nb = c_all.shape[0]
    depth = mod_w.shape[0]
    n_out = mod_w.shape[2]
    tn = D_MODEL
    return pl.pallas_call(
        _mod_body,
        grid=(depth, n_out // tn),
        in_specs=[pl.BlockSpec((nb, D_MODEL), lambda l, j: (0, 0)),
                  pl.BlockSpec((None, D_MODEL, tn), lambda l, j: (l, 0, j)),
                  pl.BlockSpec((None, 1, tn), lambda l, j: (l, 0, j))],
        out_specs=pl.BlockSpec((None, nb, tn), lambda l, j: (l, 0, j)),
        out_shape=jax.ShapeDtypeStruct((depth, nb, n_out), F32),
        compiler_params=_cparams(("arbitrary", "arbitrary")),
        name="modulation",
    )(c_all, mod_w, mod_b.reshape(depth, 1, n_out))


def _head_mean_sq(t, bones):
    sq = t * t
    hi = sq.astype(jnp.bfloat16)
    lo = (sq - hi.astype(F32)).astype(jnp.bfloat16)
    return (jnp.dot(hi, bones, preferred_element_type=F32)
            + jnp.dot(lo, bones, preferred_element_type=F32))


def _inproj_body(*refs, dils):
    (x_ref, sc_ref, sh_ref, g_ref, w_ref, bones_ref, qg_ref, kg_ref,
     zm_ref, q_ref, kf_ref, vf_ref, kb_ref, vb_ref) = refs[:14]
    x = x_ref[...]
    h = _rms_rows(x) * g_ref[...]
    h = h * (1.0 + sc_ref[...]) + sh_ref[...]
    z = jnp.dot(h.astype(MXU_DT), w_ref[...], preferred_element_type=F32)
    nm = 6 * W_GRP
    zm_ref[...] = z[:, :nm]
    q = z[:, nm:nm + W_GRP]
    k = z[:, nm + W_GRP:nm + 2 * W_GRP]
    v = z[:, nm + 2 * W_GRP:]
    bones = bones_ref[...]
    qn = (q * lax.rsqrt(_head_mean_sq(q, bones) + EPS) * qg_ref[...]) * (HEAD_DIM ** -0.5)
    kn = k * lax.rsqrt(_head_mean_sq(k, bones) + EPS) * kg_ref[...]
    q_ref[...] = qn.astype(q_ref.dtype)
    kf_ref[...] = kn
    vf_ref[...] = v
    kb_ref[...] = kn.astype(kb_ref.dtype)
    vb_ref[...] = v.astype(vb_ref.dtype)
    if dils:
        stage_ref = refs[-1]
        tm = x_ref.shape[0]
        for a, val in enumerate((qn, kn, v)):
            for half in range(W_GRP // LANES):
                stage_ref[a, half] = val[:, half * LANES:(half + 1) * LANES]
        for di, dil in enumerate(dils):
            for a in range(3):
                out_ref = refs[14 + 3 * di + a]
                for r in range(dil):
                    for half in range(W_GRP // LANES):
                        out_ref[r, :, half * LANES:(half + 1) * LANES] = (
                            stage_ref.at[a, half][pl.ds(r, tm // dil, stride=dil), :].astype(out_ref.dtype))


def _mod_spec(mod, k, tm, rows_per_mod):
    if mod.ndim == 4:
        return pl.BlockSpec((None, None, 1, D_MODEL), lambda i: (k, (i * tm) // rows_per_mod, 0, 0))
    return pl.BlockSpec((None, tm, D_MODEL), lambda i: (k, i, 0))


def _in_proj(x, mod, norm_g, w_in, bones, qg, kg, *, tm, rows_per_mod, dils=(), seq=None):
    t = x.shape[0]
    n_in = w_in.shape[1]
    nm = 6 * W_GRP
    row = lambda i: (i, 0)
    const = lambda i: (0, 0)
    outs = [jax.ShapeDtypeStruct((t, nm), F32),
            jax.ShapeDtypeStruct((t, W_GRP), MXU_DT),
            jax.ShapeDtypeStruct((t, W_GRP), F32),
            jax.ShapeDtypeStruct((t, W_GRP), F32),
            jax.ShapeDtypeStruct((t, W_GRP), MXU_DT),
            jax.ShapeDtypeStruct((t, W_GRP), MXU_DT)]
    out_specs = [pl.BlockSpec((tm, nm), row)] + [pl.BlockSpec((tm, W_GRP), row)] * 5
    scratch = []
    if dils:
        tiles_per_seq = seq // tm
        for dil in dils:
            outs += [jax.ShapeDtypeStruct((t // seq, dil, seq // dil, W_GRP), MXU_DT)] * 3
            out_specs += [pl.BlockSpec((None, dil, tm // dil, W_GRP),
                                       lambda i: (i // tiles_per_seq, 0, i % tiles_per_seq, 0))] * 3
        scratch = [pltpu.VMEM((3, W_GRP // LANES, tm, LANES), F32)]
    return pl.pallas_call(
        functools.partial(_inproj_body, dils=tuple(dils)),
        grid=(t // tm,),
        in_specs=[pl.BlockSpec((tm, D_MODEL), row),
                  _mod_spec(mod, 1, tm, rows_per_mod),
                  _mod_spec(mod, 0, tm, rows_per_mod),
                  pl.BlockSpec((1, D_MODEL), const),
                  pl.BlockSpec((D_MODEL, n_in), const),
                  pl.BlockSpec((W_GRP, W_GRP), const),
                  pl.BlockSpec((1, W_GRP), const),
                  pl.BlockSpec((1, W_GRP), const)],
        out_specs=out_specs,
        out_shape=outs,
        scratch_shapes=scratch,
        compiler_params=_cparams(("arbitrary",)),
        name="in_proj",
    )(x, mod, mod, norm_g, w_in, bones, qg, kg)


def _softplus(x):
    return jnp.maximum(x, 0.0) + jnp.log(1.0 + jnp.exp(-jnp.abs(x)))


def _mixer_ab_body(xa_ref, ga_ref, gb_ref, gc_ref, xb_ref,
                   cw_ref, cb_ref, wa_ref, ba_ref, wx_ref, bx_ref, lam_ref, sw_ref, on_a_ref, on_b_ref,
                   h0_ref, conv0_ref, sconv0_ref,
                   oa_ref, ob_ref, hn_ref, convn_ref, sconvn_ref,
                   xe_ref, pe_ref, hc_ref):
    i = pl.program_id(1)
    tl = xa_ref.shape[0]

    @pl.when(i == 0)
    def _():
        xe_ref[8 - (LRU_CONV - 1):8, :] = conv0_ref[...]
        pe_ref[8 - (SCONV_W - 1):8, :] = sconv0_ref[...]
        hc_ref[...] = h0_ref[...]

    row = lax.broadcasted_iota(jnp.int32, (tl, 1), 0)
    xa = xa_ref[...]
    xe_ref[8:, :] = xa
    cw = cw_ref[...]
    xc = cw[LRU_CONV - 1:LRU_CONV, :] * xa
    for s in range(1, LRU_CONV):
        xc = xc + cw[LRU_CONV - 1 - s:LRU_CONV - s, :] * xe_ref[8 - s:8 - s + tl, :]
    xc = xc + cb_ref[...]
    convn_ref[...] = xa[tl - (LRU_CONV - 1):, :]
    xe_ref[0:8, :] = xa[tl - 8:, :]
    xcb = xc.astype(MXU_DT)
    r = _sigmoid(jnp.dot(xcb, wa_ref[...], preferred_element_type=F32) + ba_ref[...])
    ig = _sigmoid(jnp.dot(xcb, wx_ref[...], preferred_element_type=F32) + bx_ref[...])
    log_a = (-LRU_C * r) * _softplus(-lam_ref[...])
    a = jnp.exp(log_a)
    b = jnp.sqrt(-jnp.tanh(log_a) * (a * a + 1.0)) * (ig * xc)
    s = 1
    while s < tl:
        b = a * _shift_rows(b, s, 0.0, row) + b
        a = a * _shift_rows(a, s, 1.0, row)
        s *= 2
    h = b + a * hc_ref[...]
    hc_ref[...] = h[tl - 1:, :]
    hn_ref[...] = h[tl - 1:, :]
    out_a = h * _gelu(ga_ref[...])
    oa_ref[...] = (_rms_rows(out_a) * on_a_ref[...]).astype(oa_ref.dtype)
    p = gc_ref[...] * xb_ref[...]
    pe_ref[8:, :] = p
    sw = sw_ref[...]
    yb = sw[SCONV_W - 1:SCONV_W, :] * p
    for s in range(1, SCONV_W):
        yb = yb + sw[SCONV_W - 1 - s:SCONV_W - s, :] * pe_ref[8 - s:8 - s + tl, :]
    sconvn_ref[...] = p[tl - (SCONV_W - 1):, :]
    pe_ref[0:8, :] = p[tl - 8:, :]
    out_b = gb_ref[...] * yb
    ob_ref[...] = (_rms_rows(out_b) * on_b_ref[...]).astype(ob_ref.dtype)


def _mixer_ab(zm, lp, h0, conv0, sconv0, *, tl):
    bsz, seq, _ = zm.shape
    col = lambda c: pl.BlockSpec((None, tl, W_GRP), lambda b, i, c=c: (b, i, c))
    const = lambda shp: pl.BlockSpec(shp, lambda b, i: (0,) * len(shp))
    per_b = lambda n: pl.BlockSpec((None, n, W_GRP), lambda b, i: (b, 0, 0))
    outs = [jax.ShapeDtypeStruct((bsz, seq, W_GRP), MXU_DT),
            jax.ShapeDtypeStruct((bsz, seq, W_GRP), MXU_DT),
            jax.ShapeDtypeStruct((bsz, 1, W_GRP), F32),
            jax.ShapeDtypeStruct((bsz, LRU_CONV - 1, W_GRP), F32),
            jax.ShapeDtypeStruct((bsz, SCONV_W - 1, W_GRP), F32)]
    return pl.pallas_call(
        _mixer_ab_body,
        grid=(bsz, seq // tl),
        in_specs=[col(0), col(1), col(2), col(3), col(4),
                  const((LRU_CONV, W_GRP)), const((1, W_GRP)),
                  const((W_GRP, W_GRP)), const((1, W_GRP)),
                  const((W_GRP, W_GRP)), const((1, W_GRP)),
                  const((1, W_GRP)), const((SCONV_W, W_GRP)),
                  const((1, W_GRP)), const((1, W_GRP)),
                  per_b(1), per_b(LRU_CONV - 1), per_b(SCONV_W - 1)],
        out_specs=[pl.BlockSpec((None, tl, W_GRP), lambda b, i: (b, i, 0)),
                   pl.BlockSpec((None, tl, W_GRP), lambda b, i: (b, i, 0)),
                   per_b(1), per_b(LRU_CONV - 1), per_b(SCONV_W - 1)],
        out_shape=outs,
        scratch_shapes=[pltpu.VMEM((tl + 8, W_GRP), F32),
                        pltpu.VMEM((tl + 8, W_GRP), F32),
                        pltpu.VMEM((1, W_GRP), F32)],
        compiler_params=_cparams(("arbitrary", "arbitrary")),
        name="mixer_ab",
    )(zm, zm, zm, zm, zm,
      lp['lru_conv_w'], lp['lru_conv_b'], lp['lru_wa_blk'], lp['lru_ba'], lp['lru_wx_blk'], lp['lru_bx'],
      lp['lru_lambda'], lp['sconv_w'], lp['on_a'], lp['on_b'],
      h0, conv0, sconv0)


def _s5_body(u_ref, bb_ref, cre_ref, cim_ref, ar_ref, ai_ref, d_ref, gw_ref, gb_ref, on_ref,
             h0r_ref, h0i_ref,
             o_ref, hnr_ref, hni_ref,
             tr_ref, ti_ref, pr_ref, pi_ref, hr_ref, hi_ref, lr_ref, li_ref, stage_ref):
    b = pl.program_id(0)
    i = pl.program_id(1)
    tl = u_ref.shape[0]
    steps = tl // SEGS
    groups = S5_N // LANES
    halves = W_GRP // LANES
    ar = ar_ref[...]
    ai = ai_ref[...]

    def powers(base_r, base_i, n, first):
        row = lax.broadcasted_iota(jnp.int32, (n, 1), 0)
        tr = jnp.broadcast_to(base_r, (n, S5_N))
        ti = jnp.broadcast_to(base_i, (n, S5_N))
        s = first
        while s < n:
            sr = _shift_rows(tr, s, 1.0, row)
            si = _shift_rows(ti, s, 0.0, row)
            tr, ti = tr * sr - ti * si, tr * si + ti * sr
            s *= 2
        return tr, ti

    @pl.when((b == 0) & (i == 0))
    def _():
        tr, ti = powers(ar, ai, tl, SEGS)
        tr_ref[...] = tr
        ti_ref[...] = ti
        pr, pi = powers(tr[tl - 1:, :], ti[tl - 1:, :], SEGS, 1)
        pr_ref[...] = pr
        pi_ref[...] = pi

    @pl.when(i == 0)
    def _():
        hr_ref[...] = h0r_ref[...]
        hi_ref[...] = h0i_ref[...]

    u = u_ref[...]
    if steps > 1:
        for half in range(halves):
            stage_ref[half] = u[:, half * LANES:(half + 1) * LANES]
        u = jnp.concatenate(
            [jnp.concatenate([stage_ref.at[half][pl.ds(j, SEGS, stride=steps), :] for half in range(halves)], axis=1)
             for j in range(steps)], axis=0)
    bu = jnp.dot(u.astype(MXU_DT), bb_ref[...], preferred_element_type=F32)
    a_r = [jnp.broadcast_to(ar[:, c * LANES:(c + 1) * LANES], (SEGS, LANES)) for c in range(groups)]
    a_i = [jnp.broadcast_to(ai[:, c * LANES:(c + 1) * LANES], (SEGS, LANES)) for c in range(groups)]
    loc_r = [jnp.zeros((SEGS, LANES), F32) for _ in range(groups)]
    loc_i = [jnp.zeros((SEGS, LANES), F32) for _ in range(groups)]
    for j in range(steps):
        rows = slice(j * SEGS, (j + 1) * SEGS)
        for c in range(groups):
            cols = slice(c * LANES, (c + 1) * LANES)
            nr = (a_r[c] * loc_r[c] - a_i[c] * loc_i[c]) + bu[rows, c * LANES:(c + 1) * LANES]
            ni = (a_r[c] * loc_i[c] + a_i[c] * loc_r[c]) + bu[rows, S5_N + c * LANES:S5_N + (c + 1) * LANES]
            loc_r[c], loc_i[c] = nr, ni
            lr_ref[rows, cols] = nr
            li_ref[rows, cols] = ni
    er = jnp.concatenate(loc_r, axis=1)
    ei = jnp.concatenate(loc_i, axis=1)
    seg = lax.broadcasted_iota(jnp.int32, (SEGS, 1), 0)
    pr = pr_ref[...]
    pi = pi_ref[...]
    mr, mi = pr[0:1, :], pi[0:1, :]
    s = 1
    while s < SEGS:
        sr = _shift_rows(er, s, 0.0, seg)
        si = _shift_rows(ei, s, 0.0, seg)
        er, ei = er + (mr * sr - mi * si), ei + (mr * si + mi * sr)
        mr, mi = mr * mr - mi * mi, 2.0 * (mr * mi)
        s *= 2
    cr = hr_ref[...]
    ci = hi_ref[...]
    er, ei = er + (pr * cr - pi * ci), ei + (pr * ci + pi * cr)
    in_r = jnp.where(seg >= 1, pltpu.roll(er, 1, 0), cr)
    in_i = jnp.where(seg >= 1, pltpu.roll(ei, 1, 0), ci)
    hr_ref[...] = er[SEGS - 1:, :]
    hi_ref[...] = ei[SEGS - 1:, :]
    hnr_ref[...] = er[SEGS - 1:, :]
    hni_ref[...] = ei[SEGS - 1:, :]
    tr = tr_ref[...]
    ti = ti_ref[...]
    sr = jnp.tile(in_r, (steps, 1))
    si = jnp.tile(in_i, (steps, 1))
    hr = lr_ref[...] + (tr * sr - ti * si)
    hi = li_ref[...] + (tr * si + ti * sr)
    y = (jnp.dot(hr.astype(MXU_DT), cre_ref[...], preferred_element_type=F32)
         - jnp.dot(hi.astype(MXU_DT), cim_ref[...], preferred_element_type=F32)) + d_ref[...] * u
    g = jnp.dot(_gelu(y).astype(MXU_DT), gw_ref[...], preferred_element_type=F32) + gb_ref[...]
    out = g[:, :W_GRP] * _sigmoid(g[:, W_GRP:])
    out = _rms_rows(out) * on_ref[...]
    if steps > 1:
        for half in range(halves):
            stage_ref[half] = out[:, half * LANES:(half + 1) * LANES]
        out = jnp.concatenate(
            [jnp.concatenate([stage_ref.at[half][pl.ds(sg, steps, stride=SEGS), :] for half in range(halves)], axis=1)
             for sg in range(SEGS)], axis=0)
    o_ref[...] = out.astype(o_ref.dtype)


def _s5_mixer(zm, lp, h0r, h0i, *, tl):
    bsz, seq, _ = zm.shape
    const = lambda shp: pl.BlockSpec(shp, lambda b, i: (0,) * len(shp))
    per_b = pl.BlockSpec((None, 1, S5_N), lambda b, i: (b, 0, 0))
    outs = [jax.ShapeDtypeStruct((bsz, seq, W_GRP), MXU_DT),
            jax.ShapeDtypeStruct((bsz, 1, S5_N), F32),
            jax.ShapeDtypeStruct((bsz, 1, S5_N), F32)]
    return pl.pallas_call(
        _s5_body,
        grid=(bsz, seq // tl),
        in_specs=[pl.BlockSpec((None, tl, W_GRP), lambda b, i: (b, i, 5)),
                  const((W_GRP, 2 * S5_N)), const((S5_N, W_GRP)), const((S5_N, W_GRP)),
                  const((1, S5_N)), const((1, S5_N)), const((1, W_GRP)),
                  const((W_GRP, 2 * W_GRP)), const((1, 2 * W_GRP)), const((1, W_GRP)),
                  per_b, per_b],
        out_specs=[pl.BlockSpec((None, tl, W_GRP), lambda b, i: (b, i, 0)), per_b, per_b],
        out_shape=outs,
        scratch_shapes=[pltpu.VMEM((tl, S5_N), F32), pltpu.VMEM((tl, S5_N), F32),
                        pltpu.VMEM((SEGS, S5_N), F32), pltpu.VMEM((SEGS, S5_N), F32),
                        pltpu.VMEM((1, S5_N), F32), pltpu.VMEM((1, S5_N), F32),
                        pltpu.VMEM((tl, S5_N), F32), pltpu.VMEM((tl, S5_N), F32),
                        pltpu.VMEM((W_GRP // LANES, tl, LANES), F32)],
        compiler_params=_cparams(("arbitrary", "arbitrary")),
        name="s5_mixer",
    )(zm, lp['s5_bb'], lp['s5_cre'], lp['s5_cim'], lp['s5_abr'], lp['s5_abi'], lp['s5_d'],
      lp['s5_glu_w'], lp['s5_glu_b'], lp['on_c'], h0r, h0i)


CODE_MASKED = -1
CODE_ZERO = -2


def _bias_body(rb_ref, code_ref, o_ref):
    code = code_ref[...]
    acc = jnp.where(code == CODE_MASKED, NEG, 0.0).astype(F32)
    for c in range(REL_BUCKETS * ATT_HEADS):
        acc = jnp.where(code == c, rb_ref[c], acc)
    o_ref[...] = acc


def _bias_table(rel_bias, codes):
    rows, cols = codes.shape
    tr = max(t for t in range(8, 513, 8) if rows % t == 0)
    return pl.pallas_call(
        _bias_body,
        grid_spec=pltpu.PrefetchScalarGridSpec(
            num_scalar_prefetch=1,
            grid=(rows // tr,),
            in_specs=[pl.BlockSpec((tr, cols), lambda i, rb: (i, 0))],
            out_specs=pl.BlockSpec((tr, cols), lambda i, rb: (i, 0))),
        out_shape=jax.ShapeDtypeStruct((rows, cols), F32),
        compiler_params=_cparams(("arbitrary",)),
        name="bias_table",
    )(rel_bias.reshape(-1), jnp.asarray(codes))


def _t5_bucket(n):
    n = np.asarray(n).astype(np.int32)
    max_exact = REL_BUCKETS // 2
    nf = np.maximum(n, 1).astype(np.float32)
    large = max_exact + (np.log(nf / max_exact) / np.log(REL_MAX_DIST / max_exact)
                         * (REL_BUCKETS - max_exact)).astype(np.int32)
    large = np.minimum(large, REL_BUCKETS - 1)
    return np.where(n < max_exact, n, large).astype(np.int32)


def _prompt_bias_codes():
    i = np.arange(Q_BLK)[:, None]
    j = np.arange(2 * Q_BLK)[None, :]
    rel = Q_BLK + i - j
    out = np.zeros((len(PATTERNS), 2, ATT_HEADS, Q_BLK, 2 * Q_BLK), np.int32)
    for p, (win, dil) in enumerate(PATTERNS):
        span = win // dil
        valid = (rel >= 0) & (rel <= span)
        bucket = _t5_bucket(np.clip(rel, 0, None) * dil)
        for var in range(2):
            v = valid & ((j >= Q_BLK) | (var == 1))
            for h in range(ATT_HEADS):
                out[p, var, h] = np.where(v, bucket * ATT_HEADS + h, CODE_MASKED)
    return out.reshape(-1, 2 * Q_BLK)


S_MAX = 8
S_CACHE = 128
S_TAB = S_CACHE + S_MAX


def _sample_bias_codes(dec_seq):
    head = np.arange(ATT_HEADS)
    tab = np.full((len(PATTERNS), S_TAB, ATT_HEADS), CODE_MASKED, np.int32)
    new = np.full((len(PATTERNS), dec_seq, dec_seq, ATT_HEADS), CODE_MASKED, np.int32)
    for p, (win, dil) in enumerate(PATTERNS):
        span = win // dil
        for i in range(S_TAB):
            j = S_TAB - 1 - i
            if j <= span:
                tab[p, i] = _t5_bucket(np.array(j * dil)) * ATT_HEADS + head
        for s in range(dec_seq):
            for s2 in range(s + 1):
                if (s - s2) % dil == 0 and (s - s2) // dil <= span:
                    new[p, s, s2] = _t5_bucket(np.array(s - s2)) * ATT_HEADS + head
    return tab.reshape(-1, ATT_HEADS), new.reshape(-1, ATT_HEADS)


def _attn_p_body(q_ref, kp_ref, kc_ref, vp_ref, vc_ref, bias_ref, o_ref, lse_ref):
    q = q_ref[...]
    k2 = jnp.concatenate([kp_ref[...], kc_ref[...]], axis=0)
    v2 = jnp.concatenate([vp_ref[...], vc_ref[...]], axis=0)
    lane = lax.broadcasted_iota(jnp.int32, (1, W_GRP), 1)
    head_of_lane = [(lane >= h * HEAD_DIM) & (lane < (h + 1) * HEAD_DIM) for h in range(ATT_HEADS)]
    qs = jnp.concatenate([jnp.where(hm, q, jnp.zeros_like(q)) for hm in head_of_lane], axis=0)
    s = lax.dot_general(qs, k2, (((1,), (1,)), ((), ())), preferred_element_type=F32) + bias_ref[...]
    m = jnp.max(s, axis=-1, keepdims=True)
    pr = jnp.exp(s - m)
    den = jnp.sum(pr, axis=-1, keepdims=True)
    pv = jnp.dot(pr.astype(MXU_DT), v2, preferred_element_type=F32) / den
    lse_rows = m + jnp.log(den)
    o = jnp.zeros((Q_BLK, W_GRP), F32)
    lse = jnp.zeros((Q_BLK, W_GRP), F32)
    for h, hm in enumerate(head_of_lane):
        o = jnp.where(hm, pv[h * Q_BLK:(h + 1) * Q_BLK], o)
        lse = jnp.where(hm, lse_rows[h * Q_BLK:(h + 1) * Q_BLK], lse)
    o_ref[...] = o
    lse_ref[...] = lse


def _attn_prompt_pattern(q, k, v, bias):
    bsz, dil, md, _ = q.shape
    nb = md // Q_BLK
    cur = pl.BlockSpec((None, None, Q_BLK, W_GRP), lambda b, r, n: (b, r, n, 0))
    prv = pl.BlockSpec((None, None, Q_BLK, W_GRP), lambda b, r, n: (b, r, jnp.maximum(n - 1, 0), 0))
    bsp = pl.BlockSpec((None, ATT_HEADS * Q_BLK, 2 * Q_BLK), lambda b, r, n: (jnp.minimum(n, 1), 0, 0))
    return pl.pallas_call(
        _attn_p_body,
        grid=(bsz, dil, nb),
        in_specs=[cur, prv, cur, prv, cur, bsp],
        out_specs=[cur, cur],
        out_shape=[jax.ShapeDtypeStruct((bsz, dil, md, W_GRP), F32)] * 2,
        compiler_params=_cparams(("arbitrary", "arbitrary", "arbitrary")),
        name=f"attn_prompt_d{dil}",
    )(q, k, k, v, v, bias)


def _attn_merge_body(*refs, dils):
    n_pat = len(dils)
    on_ref, out_ref, stage_ref = refs[2 * n_pat:]
    tm = out_ref.shape[0]
    halves = W_GRP // LANES
    vals = []
    for t in range(2 * n_pat):
        dil = dils[t // 2]
        if dil == 1:
            vals.append(refs[t][...])
            continue
        for r in range(dil):
            blk = refs[t][r]
            for half in range(halves):
                stage_ref.at[t, half][pl.ds(r, tm // dil, stride=dil), :] = blk[:, half * LANES:(half + 1) * LANES]
        vals.append(jnp.concatenate([stage_ref[t, half] for half in range(halves)], axis=1))
    os_, ls_ = vals[0::2], vals[1::2]
    mx = functools.reduce(jnp.maximum, ls_)
    ws = [jnp.exp(l_ - mx) for l_ in ls_]
    num = functools.reduce(lambda a_, b_: a_ + b_, [w_ * o_ for w_, o_ in zip(ws, os_)])
    merged = num / functools.reduce(lambda a_, b_: a_ + b_, ws)
    out_ref[...] = (_rms_rows(merged) * on_ref[...]).astype(out_ref.dtype)


def _attn_merge(pattern_outs, on_d, *, tm):
    dils = tuple(o.shape[1] for o, _ in pattern_outs)
    bsz, _, seq, _ = pattern_outs[0][0].shape
    seq = seq * dils[0]
    specs, args = [], []
    for (o, lse), dil in zip(pattern_outs, dils):
        if dil == 1:
            sp = pl.BlockSpec((None, None, tm, W_GRP), lambda b, j: (b, 0, j, 0))
        else:
            sp = pl.BlockSpec((None, dil, tm // dil, W_GRP), lambda b, j: (b, 0, j, 0))
        specs += [sp, sp]
        args += [o, lse]
    return pl.pallas_call(
        functools.partial(_attn_merge_body, dils=dils),
        grid=(bsz, seq // tm),
        in_specs=specs + [pl.BlockSpec((1, W_GRP), lambda b, j: (0, 0))],
        out_specs=pl.BlockSpec((None, tm, W_GRP), lambda b, j: (b, j, 0)),
        out_shape=jax.ShapeDtypeStruct((bsz, seq, W_GRP), MXU_DT),
        scratch_shapes=[pltpu.VMEM((2 * len(dils), W_GRP // LANES, tm, LANES), F32)],
        compiler_params=_cparams(("arbitrary", "arbitrary")),
        name="attn_merge",
    )(*args, on_d)


def _attn_prompt(qkv_by_dil, bias_all, on_d, *, tm):
    outs = [_attn_prompt_pattern(q, k, v, bias_all[p]) for p, (q, k, v) in enumerate(qkv_by_dil)]
    return _attn_merge(outs, on_d, tm=tm)


def _attn_s_body(q_ref, kn_ref, vn_ref, k16_ref, k4_ref, v16_ref, v4_ref, tab_ref, ntab_ref, on_ref, o_ref):
    dec = q_ref.shape[0]
    kn = kn_ref[...]
    vn = vn_ref[...]
    lsum = lambda t: jnp.sum(t, axis=-1, keepdims=True)
    ksum = lambda t: jnp.sum(t, axis=0, keepdims=True)
    for s in range(dec):
        qs = q_ref[s]
        os_, ls_ = [], []
        for p, (_, dil) in enumerate(PATTERNS):
            if dil == 16:
                kc, vc = k16_ref[:, s // 4, s % 4], v16_ref[:, s // 4, s % 4]
            elif dil == 4:
                kc = k4_ref[:, :, s % 4].reshape(S_CACHE, ATT_HEADS, HEAD_DIM)
                vc = v4_ref[:, :, s % 4].reshape(S_CACHE, ATT_HEADS, HEAD_DIM)
            else:
                kc = k4_ref[24:32].reshape(S_CACHE, ATT_HEADS, HEAD_DIM)
                vc = v4_ref[24:32].reshape(S_CACHE, ATT_HEADS, HEAD_DIM)
            t0 = S_MAX - 1 - s // dil
            sc_c = lsum(kc * qs) + tab_ref[p, t0:t0 + S_CACHE]
            sc_n = lsum(kn * qs) + ntab_ref[p, s]
            m = jnp.maximum(jnp.max(sc_c, axis=0, keepdims=True), jnp.max(sc_n, axis=0, keepdims=True))
            pc = jnp.exp(sc_c - m)
            pn = jnp.exp(sc_n - m)
            den = ksum(pc) + ksum(pn)
            num = ksum(pc * vc) + ksum(pn * vn)
            os_.append(num / den)
            ls_.append(m + jnp.log(den))
        mx = functools.reduce(jnp.maximum, ls_)
        ws = [jnp.exp(l_ - mx) for l_ in ls_]
        num = functools.reduce(lambda a_, b_: a_ + b_, [w_ * o_ for w_, o_ in zip(ws, os_)])
        merged = num / functools.reduce(lambda a_, b_: a_ + b_, ws)
        ms = jnp.sum(lsum(merged * merged), axis=1, keepdims=True) * (1.0 / W_GRP)
        o_ref[s:s + 1] = merged * lax.rsqrt(ms + EPS) * on_ref[...]


def _attn_sample(q, kn, vn, cache_k, cache_v, l, tab, ntab, on_d):
    bsz, dec = q.shape[:2]
    depth, _, wb = cache_k.shape[:3]
    assert wb % (16 * S_CACHE) == 0 and dec <= S_MAX and PATTERNS == ((128, 1), (512, 4), (2048, 16))
    per_b = pl.BlockSpec((None, dec, ATT_HEADS, HEAD_DIM), lambda b: (b, 0, 0, 0))
    g = wb // 16
    view = lambda c: c.reshape(depth, bsz, g, 4, 4, ATT_HEADS, HEAD_DIM)
    far = pl.BlockSpec((None, None, S_CACHE, 2, 4, ATT_HEADS, HEAD_DIM),
                       lambda b: (l, b, g // S_CACHE - 1, 0, 0, 0, 0))
    near = pl.BlockSpec((None, None, 32, 4, 4, ATT_HEADS, HEAD_DIM), lambda b: (l, b, g // 32 - 1, 0, 0, 0, 0))
    full = lambda a: pl.BlockSpec(a.shape, lambda b: (0,) * a.ndim)
    on_he = on_d.reshape(1, ATT_HEADS, HEAD_DIM)
    return pl.pallas_call(
        _attn_s_body,
        grid=(bsz,),
        in_specs=[per_b, per_b, per_b, far, near, far, near, full(tab), full(ntab), full(on_he)],
        out_specs=per_b,
        out_shape=jax.ShapeDtypeStruct((bsz, dec, ATT_HEADS, HEAD_DIM), F32),
        compiler_params=_cparams(("arbitrary",)),
        name="attn_sample",
    )(q, kn, vn, view(cache_k), view(cache_k), view(cache_v), view(cache_v), tab, ntab, on_he)


ROUTER_LANES = 128
ROW_TILE = D_MODEL // LANES


def _store_row_tiles(ref, val):
    n = val.shape[0]
    for j in range(ROW_TILE):
        ref[pl.ds(j, n, stride=ROW_TILE), :] = val[:, j * LANES:(j + 1) * LANES]


def _load_row_tiles(ref):
    n = ref.shape[0] // ROW_TILE
    return jnp.concatenate([ref[pl.ds(j, n, stride=ROW_TILE), :] for j in range(ROW_TILE)], axis=1)


def _post_mix_body(x_ref, a_ref, b_ref, c_ref, d_ref, w_ref, g1_ref, sc_ref, sh_ref, g_ref,
                   wr_ref, br_ref, tri_ref, cnt_in_ref,
                   x1_ref, h2_ref, eid_ref, gate_ref, rank_ref, cnt_ref,
                   run_ref):
    i = pl.program_id(0)

    @pl.when(i == 0)
    def _():
        run_ref[...] = cnt_in_ref[...]

    mix = jnp.concatenate([a_ref[...], b_ref[...], c_ref[...], d_ref[...]], axis=1)
    y = jnp.dot(mix, w_ref[...], preferred_element_type=F32)
    x1 = x_ref[...] + g1_ref[...] * y
    x1_ref[...] = x1
    h2 = _rms_rows(x1) * g_ref[...]
    h2 = h2 * (1.0 + sc_ref[...]) + sh_ref[...]
    _store_row_tiles(h2_ref, h2)
    logits = jnp.dot(h2.astype(MXU_DT), wr_ref[...], preferred_element_type=F32) + br_ref[...]
    tm = logits.shape[0]
    lane = lax.broadcasted_iota(jnp.int32, (tm, ROUTER_LANES), 1)
    big = jnp.int32(10 ** 6)
    is_g = lane < N_GROUPS
    gl = jnp.where(is_g, logits, -jnp.inf)
    gmax = jnp.max(gl, axis=-1, keepdims=True)
    gsel = jnp.min(jnp.where(gl == gmax, lane, big), axis=-1, keepdims=True)
    gprob = 1.0 / jnp.sum(jnp.where(is_g, jnp.exp(logits - gmax), 0.0), axis=-1, keepdims=True)
    lo_lane = N_GROUPS + gsel * EXP_PER_GROUP
    in_grp = (lane >= lo_lane) & (lane < lo_lane + EXP_PER_GROUP)
    el = jnp.where(in_grp, logits, -jnp.inf)
    v1 = jnp.max(el, axis=-1, keepdims=True)
    i1 = jnp.min(jnp.where(el == v1, lane, big), axis=-1, keepdims=True)
    el2 = jnp.where(lane == i1, -jnp.inf, el)
    v2 = jnp.max(el2, axis=-1, keepdims=True)
    i2 = jnp.min(jnp.where(el2 == v2, lane, big), axis=-1, keepdims=True)
    e2w = jnp.exp(v2 - v1)
    gate1 = (1.0 / (1.0 + e2w)) * gprob
    gate2 = (e2w / (1.0 + e2w)) * gprob
    e1 = i1 - N_GROUPS
    e2 = i2 - N_GROUPS
    oh1 = lane == e1
    oh2 = lane == e2
    both = jnp.where(oh1 | oh2, 1.0, 0.0)
    before = jnp.dot(tri_ref[...], both.astype(jnp.bfloat16), preferred_element_type=F32) + run_ref[...]
    r1 = jnp.sum(jnp.where(oh1, before, 0.0), axis=-1, keepdims=True).astype(jnp.int32)
    r2 = jnp.sum(jnp.where(oh2, before, 0.0), axis=-1, keepdims=True).astype(jnp.int32)
    run = run_ref[...] + jnp.sum(both, axis=0, keepdims=True)
    run_ref[...] = run
    cnt_ref[...] = run
    eid_ref[...] = jnp.where(lane == 0, e1, jnp.where(lane == 1, e2, 0))
    gate_ref[...] = jnp.where(lane == 0, gate1, jnp.where(lane == 1, gate2, 0.0))
    rank_ref[...] = jnp.where(lane == 0, r1, jnp.where(lane == 1, r2, 0))


def _post_mix(x, pieces, mod, lp, cnt_in, *, tm, rows_per_mod):
    t = x.shape[0]
    row = lambda i: (i, 0)
    const = lambda i: (0, 0)
    piece = pl.BlockSpec((tm, W_GRP), row)
    wide = pl.BlockSpec((tm, D_MODEL), row)
    lanes = pl.BlockSpec((tm, ROUTER_LANES), row)
    outs = [jax.ShapeDtypeStruct((t, D_MODEL), F32), jax.ShapeDtypeStruct((t * ROW_TILE, LANES), F32),
            jax.ShapeDtypeStruct((t, ROUTER_LANES), jnp.int32), jax.ShapeDtypeStruct((t, ROUTER_LANES), F32),
            jax.ShapeDtypeStruct((t, ROUTER_LANES), jnp.int32), jax.ShapeDtypeStruct((1, ROUTER_LANES), F32)]
    return pl.pallas_call(
        _post_mix_body,
        grid=(t // tm,),
        in_specs=[wide, piece, piece, piece, piece,
                  pl.BlockSpec((D_MODEL, D_MODEL), const),
                  _mod_spec(mod, 2, tm, rows_per_mod), _mod_spec(mod, 4, tm, rows_per_mod),
                  _mod_spec(mod, 3, tm, rows_per_mod),
                  pl.BlockSpec((1, D_MODEL), const),
                  pl.BlockSpec((D_MODEL, ROUTER_LANES), const), pl.BlockSpec((1, ROUTER_LANES), const),
                  pl.BlockSpec((tm, tm), const), pl.BlockSpec((1, ROUTER_LANES), const)],
        out_specs=[wide, pl.BlockSpec((tm * ROW_TILE, LANES), row), lanes, lanes, lanes,
                   pl.BlockSpec((1, ROUTER_LANES), const)],
        out_shape=outs,
        scratch_shapes=[pltpu.VMEM((1, ROUTER_LANES), F32)],
        compiler_params=_cparams(("arbitrary",)),
        name="post_mix",
    )(x, *pieces, lp['w_out'], mod, mod, mod, lp['norm_ffn'], lp['router_w'], lp['router_b'], lp['tri'], cnt_in)


def _row_copy(src_ref, s, dst_ref, d, sem):
    return pltpu.make_async_copy(src_ref.at[pl.ds(pl.multiple_of(s * ROW_TILE, ROW_TILE), ROW_TILE)],
                                 dst_ref.at[pl.ds(pl.multiple_of(d * ROW_TILE, ROW_TILE), ROW_TILE)], sem)


def _dispatch_body(dest_ref, h_ref, xs_in_ref, xs_ref, sem):
    del xs_in_ref
    tm = h_ref.shape[0] // ROW_TILE
    base = pl.program_id(0) * (2 * tm)

    def issue(t, c):
        _row_copy(h_ref, t, xs_ref, dest_ref[base + 2 * t], sem).start()
        _row_copy(h_ref, t, xs_ref, dest_ref[base + 2 * t + 1], sem).start()
        return c

    lax.fori_loop(0, tm, issue, 0)
    for _ in range(2):
        pltpu.make_async_copy(h_ref, xs_ref.at[pl.ds(0, tm * ROW_TILE)], sem).wait()


def _dispatch(h2, dest_flat, xs, *, tm):
    t = h2.shape[0] // ROW_TILE
    return pl.pallas_call(
        _dispatch_body,
        grid_spec=pltpu.PrefetchScalarGridSpec(
            num_scalar_prefetch=1,
            grid=(t // tm,),
            in_specs=[pl.BlockSpec((tm * ROW_TILE, LANES), lambda i, d: (i, 0)),
                      pl.BlockSpec(memory_space=pl.ANY)],
            out_specs=pl.BlockSpec(memory_space=pl.ANY),
            scratch_shapes=[pltpu.SemaphoreType.DMA(())]),
        out_shape=jax.ShapeDtypeStruct(xs.shape, xs.dtype),
        input_output_aliases={2: 0},
        compiler_params=_cparams(("arbitrary",)),
        name="moe_dispatch",
    )(dest_flat, h2, xs)


def _experts_body(blk_e_ref, n_used_ref, xs_ref, wg_ref, wu_ref, wd_ref, o_ref, wgb_ref, wub_ref, wdb_ref):
    i = pl.program_id(0)
    e = blk_e_ref[i]
    e_prev = blk_e_ref[jnp.maximum(i - 1, 0)]

    @pl.when((i == 0) | (e != e_prev))
    def _():
        wgb_ref[...] = wg_ref[...].astype(MXU_DT)
        wub_ref[...] = wu_ref[...].astype(MXU_DT)
        wdb_ref[...] = wd_ref[...].astype(MXU_DT)

    @pl.when(i < n_used_ref[0])
    def _():
        x = _load_row_tiles(xs_ref).astype(MXU_DT)
        g = jnp.dot(x, wgb_ref[...], preferred_element_type=F32)
        u = jnp.dot(x, wub_ref[...], preferred_element_type=F32)
        hmid = (g * _sigmoid(g)) * u
        _store_row_tiles(o_ref, jnp.dot(hmid.astype(MXU_DT), wdb_ref[...], preferred_element_type=F32))

    @pl.when(i >= n_used_ref[0])
    def _():
        o_ref[...] = jnp.zeros_like(o_ref)


def _experts(xs, blk_e, n_used, wg, wu, wd, l):
    n_blk = xs.shape[0] // (MOE_ROWS * ROW_TILE)
    xmap = lambda i, be, nu: (jnp.minimum(i, nu[0] - 1), 0)
    return pl.pallas_call(
        _experts_body,
        grid_spec=pltpu.PrefetchScalarGridSpec(
            num_scalar_prefetch=2,
            grid=(n_blk,),
            in_specs=[pl.BlockSpec((MOE_ROWS * ROW_TILE, LANES), xmap),
                      pl.BlockSpec((None, None, D_MODEL, D_EXPERT), lambda i, be, nu: (l, be[i], 0, 0)),
                      pl.BlockSpec((None, None, D_MODEL, D_EXPERT), lambda i, be, nu: (l, be[i], 0, 0)),
                      pl.BlockSpec((None, None, D_EXPERT, D_MODEL), lambda i, be, nu: (l, be[i], 0, 0))],
            out_specs=pl.BlockSpec((MOE_ROWS * ROW_TILE, LANES), lambda i, be, nu: (i, 0)),
            scratch_shapes=[pltpu.VMEM((D_MODEL, D_EXPERT), MXU_DT), pltpu.VMEM((D_MODEL, D_EXPERT), MXU_DT),
                            pltpu.VMEM((D_EXPERT, D_MODEL), MXU_DT)]),
        out_shape=jax.ShapeDtypeStruct(xs.shape, F32),
        compiler_params=_cparams(("arbitrary",)),
        name="moe_experts",
    )(blk_e, n_used, xs, wg, wu, wd)


def _combine_body(dest_ref, ys_ref, x1_ref, gate_ref, g2_ref, x2_ref, buf0_ref, buf1_ref, sem):
    tm = x1_ref.shape[0]
    base = pl.program_id(0) * (2 * tm)

    def issue(t, c):
        _row_copy(ys_ref, dest_ref[base + 2 * t], buf0_ref, t, sem).start()
        _row_copy(ys_ref, dest_ref[base + 2 * t + 1], buf1_ref, t, sem).start()
        return c

    lax.fori_loop(0, tm, issue, 0)
    for buf_ref in (buf0_ref, buf1_ref):
        pltpu.make_async_copy(ys_ref.at[pl.ds(0, tm * ROW_TILE)], buf_ref, sem).wait()
    gate = gate_ref[...]
    y = _load_row_tiles(buf0_ref) * gate[:, 0:1] + _load_row_tiles(buf1_ref) * gate[:, 1:2]
    x2_ref[...] = x1_ref[...] + g2_ref[...] * y


def _combine(ys, dest_flat, x1, gate, mod, *, tm, rows_per_mod):
    t = x1.shape[0]
    if mod.ndim == 4:
        g2_spec = pl.BlockSpec((None, None, 1, D_MODEL), lambda i, d: (5, (i * tm) // rows_per_mod, 0, 0))
    else:
        g2_spec = pl.BlockSpec((None, tm, D_MODEL), lambda i, d: (5, i, 0))
    return pl.pallas_call(
        _combine_body,
        grid_spec=pltpu.PrefetchScalarGridSpec(
            num_scalar_prefetch=1,
            grid=(t // tm,),
            in_specs=[pl.BlockSpec(memory_space=pl.ANY),
                      pl.BlockSpec((tm, D_MODEL), lambda i, d: (i, 0)),
                      pl.BlockSpec((tm, ROUTER_LANES), lambda i, d: (i, 0)),
                      g2_spec],
            out_specs=pl.BlockSpec((tm, D_MODEL), lambda i, d: (i, 0)),
            scratch_shapes=[pltpu.VMEM((tm * ROW_TILE, LANES), F32), pltpu.VMEM((tm * ROW_TILE, LANES), F32),
                            pltpu.SemaphoreType.DMA(())]),
        out_shape=jax.ShapeDtypeStruct((t, D_MODEL), F32),
        compiler_params=_cparams(("arbitrary",)),
        name="moe_combine",
    )(dest_flat, ys, x1, gate, mod)


def _routing_tables(cnt, eids, ranks, n_blk):
    counts = cnt[0, :N_EXPERTS].astype(jnp.int32)
    padded = (counts + MOE_ROWS - 1) // MOE_ROWS * MOE_ROWS
    ends = jnp.cumsum(padded)
    starts = ends - padded
    experts = jnp.arange(N_EXPERTS, dtype=jnp.int32)
    start_of = lambda e: jnp.sum(jnp.where(e[..., None] == experts, starts, 0), axis=-1)
    dests = [(start_of(e[:, :2]) + r[:, :2]).reshape(-1) for e, r in zip(eids, ranks)]
    blk_start = jnp.arange(n_blk, dtype=jnp.int32) * MOE_ROWS
    blk_e = jnp.minimum(jnp.sum((ends[None, :] <= blk_start[:, None]).astype(jnp.int32), axis=1), N_EXPERTS - 1)
    n_used = (ends[-1] // MOE_ROWS).astype(jnp.int32).reshape(1)
    return dests, blk_e, n_used


def _block_diag(w):
    g, r, c = w.shape
    eye = jnp.eye(g, dtype=w.dtype)
    return (eye[:, None, :, None] * w[:, :, None, :]).reshape(g * r, g * c)


def _s5_discretise(log_dt, a_re, a_im, b_re, b_im):
    step = jnp.exp(log_dt)[:, None]
    mag = jnp.exp(a_re * step)
    ang = a_im * step
    abr = mag * jnp.cos(ang)
    abi = mag * jnp.sin(ang)
    den = a_re * a_re + a_im * a_im
    zr = ((abr - 1.0) * a_re + abi * a_im) / den
    zi = (abi * a_re - (abr - 1.0) * a_im) / den
    bbr = zr[..., None] * b_re - zi[..., None] * b_im
    bbi = zr[..., None] * b_im + zi[..., None] * b_re
    return abr, abi, bbr, bbi


def _prep_layer(P, l):
    row = lambda a: a.reshape(1, -1)
    abr, abi, bbr, bbi = _s5_discretise(P['s5_log_dt'][l], P['s5_a_re'][l], P['s5_a_im'][l],
                                        P['s5_b_re'][l], P['s5_b_im'][l])
    on = P['out_norm'][l]
    bones = _block_diag(jnp.full((ATT_HEADS, HEAD_DIM, HEAD_DIM), 1.0 / HEAD_DIM, F32)).astype(jnp.bfloat16)
    return {
        'norm_mix': row(P['norm_mix'][l]), 'norm_ffn': row(P['norm_ffn'][l]),
        'w_in': P['w_in'][l].astype(MXU_DT), 'w_out': P['w_out'][l].astype(MXU_DT),
        'bones': bones,
        'qg': row(jnp.tile(P['q_norm'][l], ATT_HEADS)), 'kg': row(jnp.tile(P['k_norm'][l], ATT_HEADS)),
        'lru_conv_w': P['lru_conv_w'][l], 'lru_conv_b': row(P['lru_conv_b'][l]),
        'lru_wa_blk': _block_diag(P['lru_wa'][l]).astype(MXU_DT), 'lru_ba': row(P['lru_ba'][l]),
        'lru_wx_blk': _block_diag(P['lru_wx'][l]).astype(MXU_DT), 'lru_bx': row(P['lru_bx'][l]),
        'lru_lambda': row(P['lru_lambda'][l]), 'sconv_w': P['sconv_w'][l],
        'on_a': row(on[0:W_GRP]), 'on_b': row(on[W_GRP:2 * W_GRP]),
        'on_c': row(on[2 * W_GRP:3 * W_GRP]), 'on_d': row(on[3 * W_GRP:]),
        's5_bb': jnp.concatenate([_block_diag(bbr.transpose(0, 2, 1)), _block_diag(bbi.transpose(0, 2, 1))],
                                 axis=1).astype(MXU_DT),
        's5_cre': _block_diag(P['s5_c_re'][l].transpose(0, 2, 1)).astype(MXU_DT),
        's5_cim': _block_diag(P['s5_c_im'][l].transpose(0, 2, 1)).astype(MXU_DT),
        's5_abr': row(abr), 's5_abi': row(abi), 's5_d': row(P['s5_d'][l]),
        's5_glu_w': P['s5_glu_w'][l].astype(MXU_DT), 's5_glu_b': row(P['s5_glu_b'][l]),
        'router_w': jnp.zeros((D_MODEL, ROUTER_LANES), F32)
                       .at[:, :N_GROUPS].set(P['router_g_w'][l])
                       .at[:, N_GROUPS:N_GROUPS + N_EXPERTS].set(P['router_e_w'][l]).astype(MXU_DT),
        'router_b': jnp.zeros((1, ROUTER_LANES), F32)
                       .at[0, :N_GROUPS].set(P['router_g_b'][l])
                       .at[0, N_GROUPS:N_GROUPS + N_EXPERTS].set(P['router_e_b'][l]),
    }


TOKEN_TILE = 256
SEQ_TILE = 256

_PARAM_NAMES = ('rel_bias', 'mod_w', 'mod_b', 'norm_mix', 'norm_ffn', 'w_in', 'lru_conv_w', 'lru_conv_b',
                'lru_wa', 'lru_ba', 'lru_wx', 'lru_bx', 'lru_lambda', 'sconv_w', 's5_log_dt', 's5_a_re',
                's5_a_im', 's5_b_re', 's5_b_im', 's5_c_re', 's5_c_im', 's5_d', 's5_glu_w', 's5_glu_b',
                'q_norm', 'k_norm', 'out_norm', 'w_out', 'router_g_w', 'router_g_b', 'router_e_w',
                'router_e_b', 'moe_w_gate', 'moe_w_up', 'moe_w_down')


def _mixers(x, mod, lp, st, attn_fn, *, batch, seq, tm, tl, rows_per_mod, dils=()):
    res = _in_proj(x, mod, lp['norm_mix'], lp['w_in'], lp['bones'], lp['qg'], lp['kg'],
                   tm=tm, rows_per_mod=rows_per_mod, dils=dils, seq=seq)
    zm, q, kf, vf, kb, vb = res[:6]
    zm3 = zm.reshape(batch, seq, 6 * W_GRP)
    oa, ob, lru_h, lru_conv, sconv = _mixer_ab(zm3, lp, st['lru_h'], st['lru_conv'], st['sconv'], tl=tl)
    oc, s5_re, s5_im = _s5_mixer(zm3, lp, st['s5_re'], st['s5_im'], tl=tl)
    r3 = lambda t: t.reshape(batch, seq, W_GRP)
    od = attn_fn(r3(q), r3(kf), r3(vf), r3(kb), r3(vb), res[6:])
    flat = lambda t: t.reshape(batch * seq, W_GRP)
    new_st = {'lru_h': lru_h[:, 0], 'lru_conv': lru_conv, 'sconv': sconv,
              's5_re': s5_re.reshape(batch, S5_GROUPS, S5_STATE), 's5_im': s5_im.reshape(batch, S5_GROUPS, S5_STATE),
              'win_k': kf.reshape(batch, seq, ATT_HEADS, HEAD_DIM), 'win_v': vf.reshape(batch, seq, ATT_HEADS, HEAD_DIM)}
    return [flat(oa), flat(ob), flat(oc), flat(od)], new_st


def kernel(x_prompt, x_sample, c_prompt, c_sample, state_lru_h, state_lru_conv, state_sconv, state_s5_re, state_s5_im, cache_win_k, cache_win_v, rel_bias, mod_w, mod_b, norm_mix, norm_ffn, w_in, lru_conv_w, lru_conv_b, lru_wa, lru_ba, lru_wx, lru_bx, lru_lambda, sconv_w, s5_log_dt, s5_a_re, s5_a_im, s5_b_re, s5_b_im, s5_c_re, s5_c_im, s5_d, s5_glu_w, s5_glu_b, q_norm, k_norm, out_norm, w_out, router_g_w, router_g_b, router_e_w, router_e_b, moe_w_gate, moe_w_up, moe_w_down):
    P = dict(zip(_PARAM_NAMES, (rel_bias, mod_w, mod_b, norm_mix, norm_ffn, w_in, lru_conv_w, lru_conv_b,
                                lru_wa, lru_ba, lru_wx, lru_bx, lru_lambda, sconv_w, s5_log_dt, s5_a_re,
                                s5_a_im, s5_b_re, s5_b_im, s5_c_re, s5_c_im, s5_d, s5_glu_w, s5_glu_b,
                                q_norm, k_norm, out_norm, w_out, router_g_w, router_g_b, router_e_w,
                                router_e_b, moe_w_gate, moe_w_up, moe_w_down)))
    bp, seq, d = x_prompt.shape
    bs, dec, _ = x_sample.shape
    depth = mod_w.shape[0]
    tp, ts = bp * seq, bs * dec
    wb = cache_win_k.shape[2]
    wp = min(PATTERNS[-1][0], seq)
    tm_p = min(TOKEN_TILE, tp)
    tm_s = min(TOKEN_TILE, ts)
    tl_p = min(SEQ_TILE, seq)

    nc = -(-(bp + bs) // 8) * 8
    c_all = jnp.zeros((nc, d), F32).at[:bp].set(c_prompt).at[bp:bp + bs].set(c_sample)
    mod_all = _modulation(c_all, mod_w, mod_b)
    bias_p = _bias_table(rel_bias, _prompt_bias_codes()).reshape(len(PATTERNS), 2, ATT_HEADS * Q_BLK, 2 * Q_BLK)
    dils_p = tuple(dil for _, dil in PATTERNS if dil > 1)
    tab_codes, ntab_codes = _sample_bias_codes(dec)
    tab_s = _bias_table(rel_bias, tab_codes).reshape(len(PATTERNS), S_TAB, ATT_HEADS, 1)
    ntab_s = _bias_table(rel_bias, ntab_codes).reshape(len(PATTERNS), dec, dec, ATT_HEADS, 1)
    he = lambda t: t.astype(F32).reshape(bs, dec, ATT_HEADS, HEAD_DIM)
    tri = jnp.asarray(np.tril(np.ones((TOKEN_TILE, TOKEN_TILE), np.float32), -1), jnp.bfloat16)
    n_blk = (2 * (tp + ts)) // MOE_ROWS + N_EXPERTS

    zero_st = {'lru_h': jnp.zeros((bp, 1, W_GRP), F32), 'lru_conv': jnp.zeros((bp, LRU_CONV - 1, W_GRP), F32),
               'sconv': jnp.zeros((bp, SCONV_W - 1, W_GRP), F32),
               's5_re': jnp.zeros((bp, 1, S5_N), F32), 's5_im': jnp.zeros((bp, 1, S5_N), F32)}
    names = ('lru_h', 'lru_conv', 'sconv', 's5_re', 's5_im', 'win_k', 'win_v')
    acc_p = {n: [] for n in names}
    acc_s = {n: [] for n in names}
    xp = x_prompt.reshape(tp, d)
    xs = x_sample.reshape(ts, d)
    for l in range(depth):
        lp = _prep_layer(P, l)
        lp['tri'] = tri
        m6 = mod_all[l].reshape(nc, 6, d).transpose(1, 0, 2)
        mod_p = m6[:, :bp].reshape(6, bp, 1, d)
        mod_s = jnp.repeat(m6[:, bp:bp + bs], dec, axis=1)
        attn_p = lambda q, kf, vf, kb, vb, ex: _attn_prompt(
            [(q[:, None], kb[:, None], vb[:, None])] + [tuple(ex[3 * t:3 * t + 3]) for t in range(len(dils_p))],
            bias_p, lp['on_d'], tm=tm_p)
        pieces_p, st_p = _mixers(xp, mod_p, lp, zero_st, attn_p, batch=bp, seq=seq, tm=tm_p, tl=tl_p,
                                 rows_per_mod=seq, dils=dils_p)
        st_p['win_k'] = st_p['win_k'][:, seq - wp:]
        st_p['win_v'] = st_p['win_v'][:, seq - wp:]
        cnt0 = jnp.zeros((1, ROUTER_LANES), F32)
        x1p, h2p, eid_p, gate_p, rank_p, cnt = _post_mix(xp, pieces_p, mod_p, lp, cnt0, tm=tm_p, rows_per_mod=seq)
        samp_st = {'lru_h': state_lru_h[l][:, None], 'lru_conv': state_lru_conv[l], 'sconv': state_sconv[l],
                   's5_re': state_s5_re[l].reshape(bs, 1, S5_N), 's5_im': state_s5_im[l].reshape(bs, 1, S5_N)}
        attn_s = lambda q, kf, vf, kb, vb, ex: _attn_sample(
            he(q), he(kf), he(vf), cache_win_k, cache_win_v, l, tab_s, ntab_s, lp['on_d']
        ).reshape(bs, dec, W_GRP).astype(MXU_DT)
        pieces_s, st_s = _mixers(xs, mod_s, lp, samp_st, attn_s, batch=bs, seq=dec, tm=tm_s, tl=dec,
                                 rows_per_mod=dec)
        x1s, h2s, eid_s, gate_s, rank_s, cnt = _post_mix(xs, pieces_s, mod_s, lp, cnt, tm=tm_s, rows_per_mod=dec)
        dests, blk_e, n_used = _routing_tables(cnt, [eid_p, eid_s], [rank_p, rank_s], n_blk)
        slots = jnp.zeros((n_blk * MOE_ROWS * ROW_TILE, LANES), F32)
        slots = _dispatch(h2p, dests[0], slots, tm=tm_p)
        slots = _dispatch(h2s, dests[1], slots, tm=tm_s)
        ys = _experts(slots, blk_e, n_used, moe_w_gate, moe_w_up, moe_w_down, l)
        xp = _combine(ys, dests[0], x1p, gate_p, mod_p, tm=tm_p, rows_per_mod=seq)
        xs = _combine(ys, dests[1], x1s, gate_s, mod_s, tm=tm_s, rows_per_mod=dec)
        for n in names:
            acc_p[n].append(st_p[n])
            acc_s[n].append(st_s[n])
    new_p = {n: jnp.stack(acc_p[n], axis=0) for n in names}
    new_s = {n: jnp.stack(acc_s[n], axis=0) for n in names}
    return (xp.reshape(bp, seq, d), xs.reshape(bs, dec, d),
            new_p['lru_h'], new_p['lru_conv'], new_p['sconv'], new_p['s5_re'], new_p['s5_im'],
            new_p['win_k'], new_p['win_v'],
            new_s['lru_h'], new_s['lru_conv'], new_s['sconv'], new_s['s5_re'], new_s['s5_im'],
            new_s['win_k'], new_s['win_v'])
```

```python
import functools
import math

import numpy as np
import jax
import jax.numpy as jnp
from jax import lax
from jax.experimental import pallas as pl
from jax.experimental.pallas import tpu as pltpu

F32 = jnp.float32
MXU_DT = jnp.bfloat16
HIGHEST = lax.Precision.HIGHEST

D_MODEL = 1024
DEPTH = 4
W_GRP = 256
N_Z = 9
LRU_HEADS = 4
LRU_CONV = 4
LRU_C = 8.0
SCONV_W = 3
S5_CH = 16
S5_GROUPS = 16
S5_STATE = 64
S5_N = S5_GROUPS * S5_STATE
ATT_HEADS = 4
HEAD_DIM = 64
PATTERNS = ((128, 1), (512, 4), (2048, 16))
Q_BLK = 128
REL_BUCKETS = 32
REL_MAX_DIST = 2048
N_GROUPS = 4
EXP_PER_GROUP = 8
N_EXPERTS = 32
D_EXPERT = 512
EPS = 1e-6
NEG = -1e30

VMEM_LIMIT = 56 * 1024 * 1024
LANES = 128
MOE_ROWS = 512
SEGS = 8
ATTN_SUB_BLOCKS = 2


def _cparams(sem):
    return pltpu.CompilerParams(dimension_semantics=sem, vmem_limit_bytes=VMEM_LIMIT)


def _gelu(x):
    return 0.5 * x * (1.0 + jnp.tanh(math.sqrt(2.0 / math.pi) * (x + 0.044715 * (x * x * x))))


def _sigmoid(x):
    return 1.0 / (1.0 + jnp.exp(-x))


def _rms_rows(x):
    return x * lax.rsqrt(jnp.mean(x * x, axis=-1, keepdims=True) + EPS)


def _shift_rows(x, s, fill, row):
    return jnp.where(row >= s, pltpu.roll(x, s, 0), fill)


def _mod_body(c_ref, w_ref, b_ref, o_ref):
    c = c_ref[...]
    s = c * _sigmoid(c)
    o_ref[...] = jnp.dot(s.astype(MXU_DT), w_ref[...].astype(MXU_DT), preferred_element_type=F32) + b_ref[...]


def _modulation(c_all, mod_w, mod_b):
    nb = c_all.shape[0]
    depth = mod_w.shape[0]
    n_out = mod_w.shape[2]
    tn = D_MODEL
    return pl.pallas_call(
        _mod_body,
        grid=(depth, n_out // tn),
        in_specs=[pl.BlockSpec((nb, D_MODEL), lambda l, j: (0, 0)),
                  pl.BlockSpec((None, D_MODEL, tn), lambda l, j: (l, 0, j)),
                  pl.BlockSpec((None, 1, tn), lambda l, j: (l, 0, j))],
        out_specs=pl.BlockSpec((None, nb, tn), lambda l, j: (l, 0, j)),
        out_shape=jax.ShapeDtypeStruct((depth, nb, n_out), F32),
        compiler_params=_cparams(("arbitrary", "arbitrary")),
        name="modulation",
    )(c_all, mod_w, mod_b.reshape(depth, 1, n_out))


def _head_mean_sq(t, bones):
    sq = t * t
    hi = sq.astype(jnp.bfloat16)
    lo = (sq - hi.astype(F32)).astype(jnp.bfloat16)
    return (jnp.dot(hi, bones, preferred_element_type=F32)
            + jnp.dot(lo, bones, preferred_element_type=F32))


def _inproj_body(*refs, dils):
    (x_ref, sc_ref, sh_ref, g_ref, w_ref, bones_ref, qg_ref, kg_ref,
     zm_ref, q_ref, kf_ref, vf_ref, kb_ref, vb_ref) = refs[:14]
    x = x_ref[...]
    h = _rms_rows(x) * g_ref[...]
    h = h * (1.0 + sc_ref[...]) + sh_ref[...]
    z = jnp.dot(h.astype(MXU_DT), w_ref[...], preferred_element_type=F32)
    nm = 6 * W_GRP
    zm_ref[...] = z[:, :nm]
    q = z[:, nm:nm + W_GRP]
    k = z[:, nm + W_GRP:nm + 2 * W_GRP]
    v = z[:, nm + 2 * W_GRP:]
    bones = bones_ref[...]
    qn = (q * lax.rsqrt(_head_mean_sq(q, bones) + EPS) * qg_ref[...]) * (HEAD_DIM ** -0.5)
    kn = k * lax.rsqrt(_head_mean_sq(k, bones) + EPS) * kg_ref[...]
    q_ref[...] = qn.astype(q_ref.dtype)
    kf_ref[...] = kn
    vf_ref[...] = v
    kb_ref[...] = kn.astype(kb_ref.dtype)
    vb_ref[...] = v.astype(vb_ref.dtype)
    if dils:
        stage_ref = refs[-1]
        tm = x_ref.shape[0]
        for a, val in enumerate((qn, kn, v)):
            for half in range(W_GRP // LANES):
                stage_ref[a, half] = val[:, half * LANES:(half + 1) * LANES]
        for di, dil in enumerate(dils):
            for a in range(3):
                out_ref = refs[14 + 3 * di + a]
                for r in range(dil):
                    for half in range(W_GRP // LANES):
                        out_ref[r, :, half * LANES:(half + 1) * LANES] = (
                            stage_ref.at[a, half][pl.ds(r, tm // dil, stride=dil), :].astype(out_ref.dtype))


def _mod_spec(mod, k, tm, rows_per_mod):
    if mod.ndim == 4:
        return pl.BlockSpec((None, None, 1, D_MODEL), lambda i: (k, (i * tm) // rows_per_mod, 0, 0))
    return pl.BlockSpec((None, tm, D_MODEL), lambda i: (k, i, 0))


def _in_proj(x, mod, norm_g, w_in, bones, qg, kg, *, tm, rows_per_mod, dils=(), seq=None):
    t = x.shape[0]
    n_in = w_in.shape[1]
    nm = 6 * W_GRP
    row = lambda i: (i, 0)
    const = lambda i: (0, 0)
    outs = [jax.ShapeDtypeStruct((t, nm), F32),
            jax.ShapeDtypeStruct((t, W_GRP), MXU_DT),
            jax.ShapeDtypeStruct((t, W_GRP), F32),
            jax.ShapeDtypeStruct((t, W_GRP), F32),
            jax.ShapeDtypeStruct((t, W_GRP), MXU_DT),
            jax.ShapeDtypeStruct((t, W_GRP), MXU_DT)]
    out_specs = [pl.BlockSpec((tm, nm), row)] + [pl.BlockSpec((tm, W_GRP), row)] * 5
    scratch = []
    if dils:
        tiles_per_seq = seq // tm
        for dil in dils:
            outs += [jax.ShapeDtypeStruct((t // seq, dil, seq // dil, W_GRP), MXU_DT)] * 3
            out_specs += [pl.BlockSpec((None, dil, tm // dil, W_GRP),
                                       lambda i: (i // tiles_per_seq, 0, i % tiles_per_seq, 0))] * 3
        scratch = [pltpu.VMEM((3, W_GRP // LANES, tm, LANES), F32)]
    return pl.pallas_call(
        functools.partial(_inproj_body, dils=tuple(dils)),
        grid=(t // tm,),
        in_specs=[pl.BlockSpec((tm, D_MODEL), row),
                  _mod_spec(mod, 1, tm, rows_per_mod),
                  _mod_spec(mod, 0, tm, rows_per_mod),
                  pl.BlockSpec((1, D_MODEL), const),
                  pl.BlockSpec((D_MODEL, n_in), const),
                  pl.BlockSpec((W_GRP, W_GRP), const),
                  pl.BlockSpec((1, W_GRP), const),
                  pl.BlockSpec((1, W_GRP), const)],
        out_specs=out_specs,
        out_shape=outs,
        scratch_shapes=scratch,
        compiler_params=_cparams(("arbitrary",)),
        name="in_proj",
    )(x, mod, mod, norm_g, w_in, bones, qg, kg)


def _softplus(x):
    return jnp.maximum(x, 0.0) + jnp.log(1.0 + jnp.exp(-jnp.abs(x)))


def _mixer_ab_body(xa_ref, ga_ref, gb_ref, gc_ref, xb_ref,
                   cw_ref, cb_ref, wa_ref, ba_ref, wx_ref, bx_ref, lam_ref, sw_ref, on_a_ref, on_b_ref,
                   h0_ref, conv0_ref, sconv0_ref,
                   oa_ref, ob_ref, hn_ref, convn_ref, sconvn_ref,
                   xe_ref, pe_ref, hc_ref):
    i = pl.program_id(1)
    tl = xa_ref.shape[0]

    @pl.when(i == 0)
    def _():
        xe_ref[8 - (LRU_CONV - 1):8, :] = conv0_ref[...]
        pe_ref[8 - (SCONV_W - 1):8, :] = sconv0_ref[...]
        hc_ref[...] = h0_ref[...]

    row = lax.broadcasted_iota(jnp.int32, (tl, 1), 0)
    xa = xa_ref[...]
    xe_ref[8:, :] = xa
    cw = cw_ref[...]
    xc = cw[LRU_CONV - 1:LRU_CONV, :] * xa
    for s in range(1, LRU_CONV):
        xc = xc + cw[LRU_CONV - 1 - s:LRU_CONV - s, :] * xe_ref[8 - s:8 - s + tl, :]
    xc = xc + cb_ref[...]
    convn_ref[...] = xa[tl - (LRU_CONV - 1):, :]
    xe_ref[0:8, :] = xa[tl - 8:, :]
    xcb = xc.astype(MXU_DT)
    r = _sigmoid(jnp.dot(xcb, wa_ref[...], preferred_element_type=F32) + ba_ref[...])
    ig = _sigmoid(jnp.dot(xcb, wx_ref[...], preferred_element_type=F32) + bx_ref[...])
    log_a = (-LRU_C * r) * _softplus(-lam_ref[...])
    a = jnp.exp(log_a)
    b = jnp.sqrt(-jnp.tanh(log_a) * (a * a + 1.0)) * (ig * xc)
    s = 1
    while s < tl:
        b = a * _shift_rows(b, s, 0.0, row) + b
        a = a * _shift_rows(a, s, 1.0, row)
        s *= 2
    h = b + a * hc_ref[...]
    hc_ref[...] = h[tl - 1:, :]
    hn_ref[...] = h[tl - 1:, :]
    out_a = h * _gelu(ga_ref[...])
    oa_ref[...] = (_rms_rows(out_a) * on_a_ref[...]).astype(oa_ref.dtype)
    p = gc_ref[...] * xb_ref[...]
    pe_ref[8:, :] = p
    sw = sw_ref[...]
    yb = sw[SCONV_W - 1:SCONV_W, :] * p
    for s in range(1, SCONV_W):
        yb = yb + sw[SCONV_W - 1 - s:SCONV_W - s, :] * pe_ref[8 - s:8 - s + tl, :]
    sconvn_ref[...] = p[tl - (SCONV_W - 1):, :]
    pe_ref[0:8, :] = p[tl - 8:, :]
    out_b = gb_ref[...] * yb
    ob_ref[...] = (_rms_rows(out_b) * on_b_ref[...]).astype(ob_ref.dtype)


def _mixer_ab(zm, lp, h0, conv0, sconv0, *, tl):
    bsz, seq, _ = zm.shape
    col = lambda c: pl.BlockSpec((None, tl, W_GRP), lambda b, i, c=c: (b, i, c))
    const = lambda shp: pl.BlockSpec(shp, lambda b, i: (0,) * len(shp))
    per_b = lambda n: pl.BlockSpec((None, n, W_GRP), lambda b, i: (b, 0, 0))
    outs = [jax.ShapeDtypeStruct((bsz, seq, W_GRP), MXU_DT),
            jax.ShapeDtypeStruct((bsz, seq, W_GRP), MXU_DT),
            jax.ShapeDtypeStruct((bsz, 1, W_GRP), F32),
            jax.ShapeDtypeStruct((bsz, LRU_CONV - 1, W_GRP), F32),
            jax.ShapeDtypeStruct((bsz, SCONV_W - 1, W_GRP), F32)]
    return pl.pallas_call(
        _mixer_ab_body,
        grid=(bsz, seq // tl),
        in_specs=[col(0), col(1), col(2), col(3), col(4),
                  const((LRU_CONV, W_GRP)), const((1, W_GRP)),
                  const((W_GRP, W_GRP)), const((1, W_GRP)),
                  const((W_GRP, W_GRP)), const((1, W_GRP)),
                  const((1, W_GRP)), const((SCONV_W, W_GRP)),
                  const((1, W_GRP)), const((1, W_GRP)),
                  per_b(1), per_b(LRU_CONV - 1), per_b(SCONV_W - 1)],
        out_specs=[pl.BlockSpec((None, tl, W_GRP), lambda b, i: (b, i, 0)),
                   pl.BlockSpec((None, tl, W_GRP), lambda b, i: (b, i, 0)),
                   per_b(1), per_b(LRU_CONV - 1), per_b(SCONV_W - 1)],
        out_shape=outs,
        scratch_shapes=[pltpu.VMEM((tl + 8, W_GRP), F32),
                        pltpu.VMEM((tl + 8, W_GRP), F32),
                        pltpu.VMEM((1, W_GRP), F32)],
        compiler_params=_cparams(("arbitrary", "arbitrary")),
        name="mixer_ab",
    )(zm, zm, zm, zm, zm,
      lp['lru_conv_w'], lp['lru_conv_b'], lp['lru_wa_blk'], lp['lru_ba'], lp['lru_wx_blk'], lp['lru_bx'],
      lp['lru_lambda'], lp['sconv_w'], lp['on_a'], lp['on_b'],
      h0, conv0, sconv0)


def _s5_body(u_ref, bb_ref, cre_ref, cim_ref, ar_ref, ai_ref, d_ref, gw_ref, gb_ref, on_ref,
             h0r_ref, h0i_ref,
             o_ref, hnr_ref, hni_ref,
             tr_ref, ti_ref, pr_ref, pi_ref, hr_ref, hi_ref, lr_ref, li_ref, stage_ref):
    b = pl.program_id(0)
    i = pl.program_id(1)
    tl = u_ref.shape[0]
    steps = tl // SEGS
    groups = S5_N // LANES
    halves = W_GRP // LANES
    ar = ar_ref[...]
    ai = ai_ref[...]

    def powers(base_r, base_i, n, first):
        row = lax.broadcasted_iota(jnp.int32, (n, 1), 0)
        tr = jnp.broadcast_to(base_r, (n, S5_N))
        ti = jnp.broadcast_to(base_i, (n, S5_N))
        s = first
        while s < n:
            sr = _shift_rows(tr, s, 1.0, row)
            si = _shift_rows(ti, s, 0.0, row)
            tr, ti = tr * sr - ti * si, tr * si + ti * sr
            s *= 2
        return tr, ti

    @pl.when((b == 0) & (i == 0))
    def _():
        tr, ti = powers(ar, ai, tl, SEGS)
        tr_ref[...] = tr
        ti_ref[...] = ti
        pr, pi = powers(tr[tl - 1:, :], ti[tl - 1:, :], SEGS, 1)
        pr_ref[...] = pr
        pi_ref[...] = pi

    @pl.when(i == 0)
    def _():
        hr_ref[...] = h0r_ref[...]
        hi_ref[...] = h0i_ref[...]

    u = u_ref[...]
    if steps > 1:
        for half in range(halves):
            stage_ref[half] = u[:, half * LANES:(half + 1) * LANES]
        u = jnp.concatenate(
            [jnp.concatenate([stage_ref.at[half][pl.ds(j, SEGS, stride=steps), :] for half in range(halves)], axis=1)
             for j in range(steps)], axis=0)
    bu = jnp.dot(u.astype(MXU_DT), bb_ref[...], preferred_element_type=F32)
    a_r = [jnp.broadcast_to(ar[:, c * LANES:(c + 1) * LANES], (SEGS, LANES)) for c in range(groups)]
    a_i = [jnp.broadcast_to(ai[:, c * LANES:(c + 1) * LANES], (SEGS, LANES)) for c in range(groups)]
    loc_r = [jnp.zeros((SEGS, LANES), F32) for _ in range(groups)]
    loc_i = [jnp.zeros((SEGS, LANES), F32) for _ in range(groups)]
    for j in range(steps):
        rows = slice(j * SEGS, (j + 1) * SEGS)
        for c in range(groups):
            cols = slice(c * LANES, (c + 1) * LANES)
            nr = (a_r[c] * loc_r[c] - a_i[c] * loc_i[c]) + bu[rows, c * LANES:(c + 1) * LANES]
            ni = (a_r[c] * loc_i[c] + a_i[c] * loc_r[c]) + bu[rows, S5_N + c * LANES:S5_N + (c + 1) * LANES]
            loc_r[c], loc_i[c] = nr, ni
            lr_ref[rows, cols] = nr
            li_ref[rows, cols] = ni
    er = jnp.concatenate(loc_r, axis=1)
    ei = jnp.concatenate(loc_i, axis=1)
    seg = lax.broadcasted_iota(jnp.int32, (SEGS, 1), 0)
    pr = pr_ref[...]
    pi = pi_ref[...]
    mr, mi = pr[0:1, :], pi[0:1, :]
    s = 1
    while s < SEGS:
        sr = _shift_rows(er, s, 0.0, seg)
        si = _shift_rows(ei, s, 0.0, seg)
        er, ei = er + (mr * sr - mi * si), ei + (mr * si + mi * sr)
        mr, mi = mr * mr - mi * mi, 2.0 * (mr * mi)
        s *= 2
    cr = hr_ref[...]
    ci = hi_ref[...]
    er, ei = er + (pr * cr - pi * ci), ei + (pr * ci + pi * cr)
    in_r = jnp.where(seg >= 1, pltpu.roll(er, 1, 0), cr)
    in_i = jnp.where(seg >= 1, pltpu.roll(ei, 1, 0), ci)
    hr_ref[...] = er[SEGS - 1:, :]
    hi_ref[...] = ei[SEGS - 1:, :]
    hnr_ref[...] = er[SEGS - 1:, :]
    hni_ref[...] = ei[SEGS - 1:, :]
    tr = tr_ref[...]
    ti = ti_ref[...]
    sr = jnp.tile(in_r, (steps, 1))
    si = jnp.tile(in_i, (steps, 1))
    hr = lr_ref[...] + (tr * sr - ti * si)
    hi = li_ref[...] + (tr * si + ti * sr)
    y = (jnp.dot(hr.astype(MXU_DT), cre_ref[...], preferred_element_type=F32)
         - jnp.dot(hi.astype(MXU_DT), cim_ref[...], preferred_element_type=F32)) + d_ref[...] * u
    g = jnp.dot(_gelu(y).astype(MXU_DT), gw_ref[...], preferred_element_type=F32) + gb_ref[...]
    out = g[:, :W_GRP] * _sigmoid(g[:, W_GRP:])
    out = _rms_rows(out) * on_ref[...]
    if steps > 1:
        for half in range(halves):
            stage_ref[half] = out[:, half * LANES:(half + 1) * LANES]
        out = jnp.concatenate(
            [jnp.concatenate([stage_ref.at[half][pl.ds(sg, steps, stride=SEGS), :] for half in range(halves)], axis=1)
             for sg in range(SEGS)], axis=0)
    o_ref[...] = out.astype(o_ref.dtype)


def _s5_mixer(zm, lp, h0r, h0i, *, tl):
    bsz, seq, _ = zm.shape
    const = lambda shp: pl.BlockSpec(shp, lambda b, i: (0,) * len(shp))
    per_b = pl.BlockSpec((None, 1, S5_N), lambda b, i: (b, 0, 0))
    outs = [jax.ShapeDtypeStruct((bsz, seq, W_GRP), MXU_DT),
            jax.ShapeDtypeStruct((bsz, 1, S5_N), F32),
            jax.ShapeDtypeStruct((bsz, 1, S5_N), F32)]
    return pl.pallas_call(
        _s5_body,
        grid=(bsz, seq // tl),
        in_specs=[pl.BlockSpec((None, tl, W_GRP), lambda b, i: (b, i, 5)),
                  const((W_GRP, 2 * S5_N)), const((S5_N, W_GRP)), const((S5_N, W_GRP)),
                  const((1, S5_N)), const((1, S5_N)), const((1, W_GRP)),
                  const((W_GRP, 2 * W_GRP)), const((1, 2 * W_GRP)), const((1, W_GRP)),
                  per_b, per_b],
        out_specs=[pl.BlockSpec((None, tl, W_GRP), lambda b, i: (b, i, 0)), per_b, per_b],
        out_shape=outs,
        scratch_shapes=[pltpu.VMEM((tl, S5_N), F32), pltpu.VMEM((tl, S5_N), F32),
                        pltpu.VMEM((SEGS, S5_N), F32), pltpu.VMEM((SEGS, S5_N), F32),
                        pltpu.VMEM((1, S5_N), F32), pltpu.VMEM((1, S5_N), F32),
                        pltpu.VMEM((tl, S5_N), F32), pltpu.VMEM((tl, S5_N), F32),
                        pltpu.VMEM((W_GRP // LANES, tl, LANES), F32)],
        compiler_params=_cparams(("arbitrary", "arbitrary")),
        name="s5_mixer",
    )(zm, lp['s5_bb'], lp['s5_cre'], lp['s5_cim'], lp['s5_abr'], lp['s5_abi'], lp['s5_d'],
      lp['s5_glu_w'], lp['s5_glu_b'], lp['on_c'], h0r, h0i)


CODE_MASKED = -1
CODE_ZERO = -2


def _bias_body(rb_ref, code_ref, o_ref):
    code = code_ref[...]
    acc = jnp.where(code == CODE_MASKED, NEG, 0.0).astype(F32)
    for c in range(REL_BUCKETS * ATT_HEADS):
        acc = jnp.where(code == c, rb_ref[c], acc)
    o_ref[...] = acc


def _bias_table(rel_bias, codes):
    rows, cols = codes.shape
    tr = max(t for t in range(8, 513, 8) if rows % t == 0)
    return pl.pallas_call(
        _bias_body,
        grid_spec=pltpu.PrefetchScalarGridSpec(
            num_scalar_prefetch=1,
            grid=(rows // tr,),
            in_specs=[pl.BlockSpec((tr, cols), lambda i, rb: (i, 0))],
            out_specs=pl.BlockSpec((tr, cols), lambda i, rb: (i, 0))),
        out_shape=jax.ShapeDtypeStruct((rows, cols), F32),
        compiler_params=_cparams(("arbitrary",)),
        name="bias_table",
    )(rel_bias.reshape(-1), jnp.asarray(codes))


def _t5_bucket(n):
    n = np.asarray(n).astype(np.int32)
    max_exact = REL_BUCKETS // 2
    nf = np.maximum(n, 1).astype(np.float32)
    large = max_exact + (np.log(nf / max_exact) / np.log(REL_MAX_DIST / max_exact)
                         * (REL_BUCKETS - max_exact)).astype(np.int32)
    large = np.minimum(large, REL_BUCKETS - 1)
    return np.where(n < max_exact, n, large).astype(np.int32)


def _prompt_bias_codes():
    i = np.arange(Q_BLK)[:, None]
    j = np.arange(2 * Q_BLK)[None, :]
    rel = Q_BLK + i - j
    out = np.zeros((len(PATTERNS), 2, ATT_HEADS, Q_BLK, 2 * Q_BLK), np.int32)
    for p, (win, dil) in enumerate(PATTERNS):
        span = win // dil
        valid = (rel >= 0) & (rel <= span)
        bucket = _t5_bucket(np.clip(rel, 0, None) * dil)
        for var in range(2):
            v = valid & ((j >= Q_BLK) | (var == 1))
            for h in range(ATT_HEADS):
                out[p, var, h] = np.where(v, bucket * ATT_HEADS + h, CODE_MASKED)
    return out.reshape(-1, 2 * Q_BLK)


def _sample_bias_codes(dec_seq, wb):
    def tables(n_keys, first_pos):
        codes = np.full((ATT_HEADS, dec_seq, n_keys), CODE_MASKED, np.int32)
        logm = np.zeros((ATT_HEADS, dec_seq, n_keys), np.float32)
        for s in range(dec_seq):
            dist = (wb + s) - (first_pos + np.arange(n_keys))
            mult = np.zeros(n_keys, np.int32)
            for win, dil in PATTERNS:
                mult += ((dist >= 0) & (dist % dil == 0) & (dist // dil <= win // dil)).astype(np.int32)
            bucket = _t5_bucket(np.clip(dist, 0, None))
            for h in range(ATT_HEADS):
                codes[h, s] = np.where(mult > 0, bucket * ATT_HEADS + h, CODE_MASKED)
                logm[h, s] = np.log(np.maximum(mult, 1))
        return codes.reshape(ATT_HEADS * dec_seq, n_keys), logm.reshape(ATT_HEADS * dec_seq, n_keys)
    return tables(wb, 0), tables(dec_seq, wb)


def _attn_p_body(q_ref, kp_ref, kc_ref, vp_ref, vc_ref, bias_ref, o_ref, lse_ref):
    lane = lax.broadcasted_iota(jnp.int32, (1, W_GRP), 1)
    head_of_lane = [(lane >= h * HEAD_DIM) & (lane < (h + 1) * HEAD_DIM) for h in range(ATT_HEADS)]
    n_sub = q_ref.shape[0] // Q_BLK
    first_variant = jnp.minimum(pl.program_id(2), 1)
    for sub in range(n_sub):
        rows = slice(sub * Q_BLK, (sub + 1) * Q_BLK)
        q = q_ref[rows, :]
        if sub == 0:
            k_prev, v_prev, bias = kp_ref[...], vp_ref[...], bias_ref[first_variant]
        else:
            prev_rows = slice((sub - 1) * Q_BLK, sub * Q_BLK)
            k_prev, v_prev, bias = kc_ref[prev_rows, :], vc_ref[prev_rows, :], bias_ref[1]
        k2 = jnp.concatenate([k_prev, kc_ref[rows, :]], axis=0)
        v2 = jnp.concatenate([v_prev, vc_ref[rows, :]], axis=0)
        qs = jnp.concatenate([jnp.where(hm, q, jnp.zeros_like(q)) for hm in head_of_lane], axis=0)
        s = lax.dot_general(qs, k2, (((1,), (1,)), ((), ())), preferred_element_type=F32) + bias
        m = jnp.max(s, axis=-1, keepdims=True)
        pr = jnp.exp(s - m)
        den = jnp.sum(pr, axis=-1, keepdims=True)
        pv = jnp.dot(pr.astype(MXU_DT), v2, preferred_element_type=F32) / den
        lse_rows = m + jnp.log(den)
        o = jnp.zeros((Q_BLK, W_GRP), F32)
        lse = jnp.zeros((Q_BLK, W_GRP), F32)
        for h, hm in enumerate(head_of_lane):
            o = jnp.where(hm, pv[h * Q_BLK:(h + 1) * Q_BLK], o)
            lse = jnp.where(hm, lse_rows[h * Q_BLK:(h + 1) * Q_BLK], lse)
        o_ref[rows, :] = o
        lse_ref[rows, :] = lse


def _attn_prompt_pattern(q, k, v, bias):
    bsz, dil, md, _ = q.shape
    n_sub = ATTN_SUB_BLOCKS
    assert md % (n_sub * Q_BLK) == 0
    cur = pl.BlockSpec((None, None, n_sub * Q_BLK, W_GRP), lambda b, r, n: (b, r, n, 0))
    prv = pl.BlockSpec((None, None, Q_BLK, W_GRP), lambda b, r, n: (b, r, jnp.maximum(n * n_sub - 1, 0), 0))
    bsp = pl.BlockSpec((2, ATT_HEADS * Q_BLK, 2 * Q_BLK), lambda b, r, n: (0, 0, 0))
    return pl.pallas_call(
        _attn_p_body,
        grid=(bsz, dil, md // (n_sub * Q_BLK)),
        in_specs=[cur, prv, cur, prv, cur, bsp],
        out_specs=[cur, cur],
        out_shape=[jax.ShapeDtypeStruct((bsz, dil, md, W_GRP), F32)] * 2,
        compiler_params=_cparams(("arbitrary", "arbitrary", "arbitrary")),
        name=f"attn_prompt_d{dil}",
    )(q, k, k, v, v, bias)


def _attn_merge_body(*refs, dils):
    n_pat = len(dils)
    on_ref, out_ref, stage_ref = refs[2 * n_pat:]
    tm = out_ref.shape[0]
    halves = W_GRP // LANES
    vals = []
    for t in range(2 * n_pat):
        dil = dils[t // 2]
        if dil == 1:
            vals.append(refs[t][...])
            continue
        for r in range(dil):
            blk = refs[t][r]
            for half in range(halves):
                stage_ref.at[t, half][pl.ds(r, tm // dil, stride=dil), :] = blk[:, half * LANES:(half + 1) * LANES]
        vals.append(jnp.concatenate([stage_ref[t, half] for half in range(halves)], axis=1))
    os_, ls_ = vals[0::2], vals[1::2]
    mx = functools.reduce(jnp.maximum, ls_)
    ws = [jnp.exp(l_ - mx) for l_ in ls_]
    num = functools.reduce(lambda a_, b_: a_ + b_, [w_ * o_ for w_, o_ in zip(ws, os_)])
    merged = num / functools.reduce(lambda a_, b_: a_ + b_, ws)
    out_ref[...] = (_rms_rows(merged) * on_ref[...]).astype(out_ref.dtype)


def _attn_merge(pattern_outs, on_d, *, tm):
    dils = tuple(o.shape[1] for o, _ in pattern_outs)
    bsz, _, seq, _ = pattern_outs[0][0].shape
    seq = seq * dils[0]
    specs, args = [], []
    for (o, lse), dil in zip(pattern_outs, dils):
        if dil == 1:
            sp = pl.BlockSpec((None, None, tm, W_GRP), lambda b, j: (b, 0, j, 0))
        else:
            sp = pl.BlockSpec((None, dil, tm // dil, W_GRP), lambda b, j: (b, 0, j, 0))
        specs += [sp, sp]
        args += [o, lse]
    return pl.pallas_call(
        functools.partial(_attn_merge_body, dils=dils),
        grid=(bsz, seq // tm),
        in_specs=specs + [pl.BlockSpec((1, W_GRP), lambda b, j: (0, 0))],
        out_specs=pl.BlockSpec((None, tm, W_GRP), lambda b, j: (b, j, 0)),
        out_shape=jax.ShapeDtypeStruct((bsz, seq, W_GRP), MXU_DT),
        scratch_shapes=[pltpu.VMEM((2 * len(dils), W_GRP // LANES, tm, LANES), F32)],
        compiler_params=_cparams(("arbitrary", "arbitrary")),
        name="attn_merge",
    )(*args, on_d)


def _attn_prompt(qkv_by_dil, bias_all, on_d, *, tm):
    outs = [_attn_prompt_pattern(q, k, v, bias_all[p]) for p, (q, k, v) in enumerate(qkv_by_dil)]
    return _attn_merge(outs, on_d, tm=tm)


def _attn_s_body(q_ref, kn_ref, vn_ref, kt_ref, vt_ref, lw_ref, lwn_ref, on_ref, o_ref):
    q = q_ref[...]
    kn = kn_ref[...].astype(MXU_DT)
    vn = vn_ref[...].astype(MXU_DT)
    nt = (((1,), (1,)), ((), ()))
    outs = []
    for h in range(ATT_HEADS):
        cols = slice(h * HEAD_DIM, (h + 1) * HEAD_DIM)
        qh = q[:, cols]
        sc = jnp.dot(qh, kt_ref[h].astype(MXU_DT), preferred_element_type=F32) + lw_ref[h]
        scn = lax.dot_general(qh, kn[:, cols], nt, preferred_element_type=F32) + lwn_ref[h]
        m = jnp.maximum(jnp.max(sc, axis=-1, keepdims=True), jnp.max(scn, axis=-1, keepdims=True))
        p = jnp.exp(sc - m)
        pn = jnp.exp(scn - m)
        den = jnp.sum(p, axis=-1, keepdims=True) + jnp.sum(pn, axis=-1, keepdims=True)
        num = (lax.dot_general(p.astype(MXU_DT), vt_ref[h].astype(MXU_DT), nt, preferred_element_type=F32)
               + jnp.dot(pn.astype(MXU_DT), vn[:, cols], preferred_element_type=F32))
        outs.append(num / den)
    merged = jnp.concatenate(outs, axis=1)
    o_ref[...] = (_rms_rows(merged) * on_ref[...]).astype(o_ref.dtype)


def _attn_sample(q, kn, vn, cache_k, cache_v, l, lw, lwn, on_d):
    bsz, dec, _ = q.shape
    wb = cache_k.shape[2]
    per_b = pl.BlockSpec((None, dec, W_GRP), lambda b: (b, 0, 0))
    pos_minor = lambda c: jnp.transpose(c, (0, 1, 3, 4, 2))
    cache_spec = pl.BlockSpec((None, None, ATT_HEADS, HEAD_DIM, wb), lambda b: (l, b, 0, 0, 0))
    full = lambda a: pl.BlockSpec(a.shape, lambda b: (0,) * a.ndim)
    return pl.pallas_call(
        _attn_s_body,
        grid=(bsz,),
        in_specs=[per_b, per_b, per_b, cache_spec, cache_spec, full(lw), full(lwn), full(on_d)],
        out_specs=per_b,
        out_shape=jax.ShapeDtypeStruct((bsz, dec, W_GRP), MXU_DT),
        compiler_params=_cparams(("arbitrary",)),
        name="attn_sample",
    )(q, kn, vn, pos_minor(cache_k), pos_minor(cache_v), lw, lwn, on_d)


ROUTER_LANES = 128
ROW_TILE = D_MODEL // LANES


def _store_row_tiles(ref, val):
    n = val.shape[0]
    for j in range(ROW_TILE):
        ref[pl.ds(j, n, stride=ROW_TILE), :] = val[:, j * LANES:(j + 1) * LANES]


def _load_row_tiles(ref):
    n = ref.shape[0] // ROW_TILE
    return jnp.concatenate([ref[pl.ds(j, n, stride=ROW_TILE), :] for j in range(ROW_TILE)], axis=1)


def _post_mix_body(x_ref, a_ref, b_ref, c_ref, d_ref, w_ref, g1_ref, sc_ref, sh_ref, g_ref,
                   wr_ref, br_ref, tri_ref, cnt_in_ref,
                   x1_ref, h2_ref, eid_ref, gate_ref, rank_ref, cnt_ref,
                   run_ref):
    i = pl.program_id(0)

    @pl.when(i == 0)
    def _():
        run_ref[...] = cnt_in_ref[...]

    mix = jnp.concatenate([a_ref[...], b_ref[...], c_ref[...], d_ref[...]], axis=1)
    y = jnp.dot(mix, w_ref[...], preferred_element_type=F32)
    x1 = x_ref[...] + g1_ref[...] * y
    x1_ref[...] = x1
    h2 = _rms_rows(x1) * g_ref[...]
    h2 = h2 * (1.0 + sc_ref[...]) + sh_ref[...]
    _store_row_tiles(h2_ref, h2)
    logits = jnp.dot(h2.astype(MXU_DT), wr_ref[...], preferred_element_type=F32) + br_ref[...]
    tm = logits.shape[0]
    lane = lax.broadcasted_iota(jnp.int32, (tm, ROUTER_LANES), 1)
    big = jnp.int32(10 ** 6)
    is_g = lane < N_GROUPS
    gl = jnp.where(is_g, logits, -jnp.inf)
    gmax = jnp.max(gl, axis=-1, keepdims=True)
    gsel = jnp.min(jnp.where(gl == gmax, lane, big), axis=-1, keepdims=True)
    gprob = 1.0 / jnp.sum(jnp.where(is_g, jnp.exp(logits - gmax), 0.0), axis=-1, keepdims=True)
    lo_lane = N_GROUPS + gsel * EXP_PER_GROUP
    in_grp = (lane >= lo_lane) & (lane < lo_lane + EXP_PER_GROUP)
    el = jnp.where(in_grp, logits, -jnp.inf)
    v1 = jnp.max(el, axis=-1, keepdims=True)
    i1 = jnp.min(jnp.where(el == v1, lane, big), axis=-1, keepdims=True)
    el2 = jnp.where(lane == i1, -jnp.inf, el)
    v2 = jnp.max(el2, axis=-1, keepdims=True)
    i2 = jnp.min(jnp.where(el2 == v2, lane, big), axis=-1, keepdims=True)
    e2w = jnp.exp(v2 - v1)
    gate1 = (1.0 / (1.0 + e2w)) * gprob
    gate2 = (e2w / (1.0 + e2w)) * gprob
    e1 = i1 - N_GROUPS
    e2 = i2 - N_GROUPS
    oh1 = lane == e1
    oh2 = lane == e2
    both = jnp.where(oh1 | oh2, 1.0, 0.0)
    before = jnp.dot(tri_ref[...], both.astype(jnp.bfloat16), preferred_element_type=F32) + run_ref[...]
    r1 = jnp.sum(jnp.where(oh1, before, 0.0), axis=-1, keepdims=True).astype(jnp.int32)
    r2 = jnp.sum(jnp.where(oh2, before, 0.0), axis=-1, keepdims=True).astype(jnp.int32)
    run = run_ref[...] + jnp.sum(both, axis=0, keepdims=True)
    run_ref[...] = run
    cnt_ref[...] = run
    eid_ref[...] = jnp.where(lane == 0, e1, jnp.where(lane == 1, e2, 0))
    gate_ref[...] = jnp.where(lane == 0, gate1, jnp.where(lane == 1, gate2, 0.0))
    rank_ref[...] = jnp.where(lane == 0, r1, jnp.where(lane == 1, r2, 0))


def _post_mix(x, pieces, mod, lp, cnt_in, *, tm, rows_per_mod):
    t = x.shape[0]
    row = lambda i: (i, 0)
    const = lambda i: (0, 0)
    piece = pl.BlockSpec((tm, W_GRP), row)
    wide = pl.BlockSpec((tm, D_MODEL), row)
    lanes = pl.BlockSpec((tm, ROUTER_LANES), row)
    outs = [jax.ShapeDtypeStruct((t, D_MODEL), F32), jax.ShapeDtypeStruct((t * ROW_TILE, LANES), F32),
            jax.ShapeDtypeStruct((t, ROUTER_LANES), jnp.int32), jax.ShapeDtypeStruct((t, ROUTER_LANES), F32),
            jax.ShapeDtypeStruct((t, ROUTER_LANES), jnp.int32), jax.ShapeDtypeStruct((1, ROUTER_LANES), F32)]
    return pl.pallas_call(
        _post_mix_body,
        grid=(t // tm,),
        in_specs=[wide, piece, piece, piece, piece,
                  pl.BlockSpec((D_MODEL, D_MODEL), const),
                  _mod_spec(mod, 2, tm, rows_per_mod), _mod_spec(mod, 4, tm, rows_per_mod),
                  _mod_spec(mod, 3, tm, rows_per_mod),
                  pl.BlockSpec((1, D_MODEL), const),
                  pl.BlockSpec((D_MODEL, ROUTER_LANES), const), pl.BlockSpec((1, ROUTER_LANES), const),
                  pl.BlockSpec((tm, tm), const), pl.BlockSpec((1, ROUTER_LANES), const)],
        out_specs=[wide, pl.BlockSpec((tm * ROW_TILE, LANES), row), lanes, lanes, lanes,
                   pl.BlockSpec((1, ROUTER_LANES), const)],
        out_shape=outs,
        scratch_shapes=[pltpu.VMEM((1, ROUTER_LANES), F32)],
        compiler_params=_cparams(("arbitrary",)),
        name="post_mix",
    )(x, *pieces, lp['w_out'], mod, mod, mod, lp['norm_ffn'], lp['router_w'], lp['router_b'], lp['tri'], cnt_in)


def _row_copy(src_ref, s, dst_ref, d, sem):
    return pltpu.make_async_copy(src_ref.at[pl.ds(pl.multiple_of(s * ROW_TILE, ROW_TILE), ROW_TILE)],
                                 dst_ref.at[pl.ds(pl.multiple_of(d * ROW_TILE, ROW_TILE), ROW_TILE)], sem)


def _dispatch_body(dest_ref, h_ref, xs_in_ref, xs_ref, sem):
    del xs_in_ref
    tm = h_ref.shape[0] // ROW_TILE
    base = pl.program_id(0) * (2 * tm)

    def issue(t, c):
        _row_copy(h_ref, t, xs_ref, dest_ref[base + 2 * t], sem).start()
        _row_copy(h_ref, t, xs_ref, dest_ref[base + 2 * t + 1], sem).start()
        return c

    lax.fori_loop(0, tm, issue, 0)
    for _ in range(2):
        pltpu.make_async_copy(h_ref, xs_ref.at[pl.ds(0, tm * ROW_TILE)], sem).wait()


def _dispatch(h2, dest_flat, xs, *, tm):
    t = h2.shape[0] // ROW_TILE
    return pl.pallas_call(
        _dispatch_body,
        grid_spec=pltpu.PrefetchScalarGridSpec(
            num_scalar_prefetch=1,
            grid=(t // tm,),
            in_specs=[pl.BlockSpec((tm * ROW_TILE, LANES), lambda i, d: (i, 0)),
                      pl.BlockSpec(memory_space=pl.ANY)],
            out_specs=pl.BlockSpec(memory_space=pl.ANY),
            scratch_shapes=[pltpu.SemaphoreType.DMA(())]),
        out_shape=jax.ShapeDtypeStruct(xs.shape, xs.dtype),
        input_output_aliases={2: 0},
        compiler_params=_cparams(("arbitrary",)),
        name="moe_dispatch",
    )(dest_flat, h2, xs)


def _experts_body(blk_e_ref, n_used_ref, xs_ref, wg_ref, wu_ref, wd_ref, o_ref, wgb_ref, wub_ref, wdb_ref):
    i = pl.program_id(0)
    e = blk_e_ref[i]
    e_prev = blk_e_ref[jnp.maximum(i - 1, 0)]

    @pl.when((i == 0) | (e != e_prev))
    def _():
        wgb_ref[...] = wg_ref[...].astype(MXU_DT)
        wub_ref[...] = wu_ref[...].astype(MXU_DT)
        wdb_ref[...] = wd_ref[...].astype(MXU_DT)

    @pl.when(i < n_used_ref[0])
    def _():
        x = _load_row_tiles(xs_ref).astype(MXU_DT)
        g = jnp.dot(x, wgb_ref[...], preferred_element_type=F32)
        u = jnp.dot(x, wub_ref[...], preferred_element_type=F32)
        hmid = (g * _sigmoid(g)) * u
        _store_row_tiles(o_ref, jnp.dot(hmid.astype(MXU_DT), wdb_ref[...], preferred_element_type=F32))

    @pl.when(i >= n_used_ref[0])
    def _():
        o_ref[...] = jnp.zeros_like(o_ref)


def _experts(xs, blk_e, n_used, wg, wu, wd, l):
    n_blk = xs.shape[0] // (MOE_ROWS * ROW_TILE)
    xmap = lambda i, be, nu: (jnp.minimum(i, nu[0] - 1), 0)
    return pl.pallas_call(
        _experts_body,
        grid_spec=pltpu.PrefetchScalarGridSpec(
            num_scalar_prefetch=2,
            grid=(n_blk,),
            in_specs=[pl.BlockSpec((MOE_ROWS * ROW_TILE, LANES), xmap),
                      pl.BlockSpec((None, None, D_MODEL, D_EXPERT), lambda i, be, nu: (l, be[i], 0, 0)),
                      pl.BlockSpec((None, None, D_MODEL, D_EXPERT), lambda i, be, nu: (l, be[i], 0, 0)),
                      pl.BlockSpec((None, None, D_EXPERT, D_MODEL), lambda i, be, nu: (l, be[i], 0, 0))],
            out_specs=pl.BlockSpec((MOE_ROWS * ROW_TILE, LANES), lambda i, be, nu: (i, 0)),
            scratch_shapes=[pltpu.VMEM((D_MODEL, D_EXPERT), MXU_DT), pltpu.VMEM((D_MODEL, D_EXPERT), MXU_DT),
                            pltpu.VMEM((D_EXPERT, D_MODEL), MXU_DT)]),
        out_shape=jax.ShapeDtypeStruct(xs.shape, F32),
        compiler_params=_cparams(("arbitrary",)),
        name="moe_experts",
    )(blk_e, n_used, xs, wg, wu, wd)


def _combine_body(dest_ref, ys_ref, x1_ref, gate_ref, g2_ref, x2_ref, buf0_ref, buf1_ref, sem):
    tm = x1_ref.shape[0]
    base = pl.program_id(0) * (2 * tm)

    def issue(t, c):
        _row_copy(ys_ref, dest_ref[base + 2 * t], buf0_ref, t, sem).start()
        _row_copy(ys_ref, dest_ref[base + 2 * t + 1], buf1_ref, t, sem).start()
        return c

    lax.fori_loop(0, tm, issue, 0)
    for buf_ref in (buf0_ref, buf1_ref):
        pltpu.make_async_copy(ys_ref.at[pl.ds(0, tm * ROW_TILE)], buf_ref, sem).wait()
    gate = gate_ref[...]
    y = _load_row_tiles(buf0_ref) * gate[:, 0:1] + _load_row_tiles(buf1_ref) * gate[:, 1:2]
    x2_ref[...] = x1_ref[...] + g2_ref[...] * y


def _combine(ys, dest_flat, x1, gate, mod, *, tm, rows_per_mod):
    t = x1.shape[0]
    if mod.ndim == 4:
        g2_spec = pl.BlockSpec((None, None, 1, D_MODEL), lambda i, d: (5, (i * tm) // rows_per_mod, 0, 0))
    else:
        g2_spec = pl.BlockSpec((None, tm, D_MODEL), lambda i, d: (5, i, 0))
    return pl.pallas_call(
        _combine_body,
        grid_spec=pltpu.PrefetchScalarGridSpec(
            num_scalar_prefetch=1,
            grid=(t // tm,),
            in_specs=[pl.BlockSpec(memory_space=pl.ANY),
                      pl.BlockSpec((tm, D_MODEL), lambda i, d: (i, 0)),
                      pl.BlockSpec((tm, ROUTER_LANES), lambda i, d: (i, 0)),
                      g2_spec],
            out_specs=pl.BlockSpec((tm, D_MODEL), lambda i, d: (i, 0)),
            scratch_shapes=[pltpu.VMEM((tm * ROW_TILE, LANES), F32), pltpu.VMEM((tm * ROW_TILE, LANES), F32),
                            pltpu.SemaphoreType.DMA(())]),
        out_shape=jax.ShapeDtypeStruct((t, D_MODEL), F32),
        compiler_params=_cparams(("arbitrary",)),
        name="moe_combine",
    )(dest_flat, ys, x1, gate, mod)


def _routing_tables(cnt, eids, ranks, n_blk):
    counts = cnt[0, :N_EXPERTS].astype(jnp.int32)
    padded = (counts + MOE_ROWS - 1) // MOE_ROWS * MOE_ROWS
    ends = jnp.cumsum(padded)
    starts = ends - padded
    experts = jnp.arange(N_EXPERTS, dtype=jnp.int32)
    start_of = lambda e: jnp.sum(jnp.where(e[..., None] == experts, starts, 0), axis=-1)
    dests = [(start_of(e[:, :2]) + r[:, :2]).reshape(-1) for e, r in zip(eids, ranks)]
    blk_start = jnp.arange(n_blk, dtype=jnp.int32) * MOE_ROWS
    blk_e = jnp.minimum(jnp.sum((ends[None, :] <= blk_start[:, None]).astype(jnp.int32), axis=1), N_EXPERTS - 1)
    n_used = (ends[-1] // MOE_ROWS).astype(jnp.int32).reshape(1)
    return dests, blk_e, n_used


def _block_diag(w):
    g, r, c = w.shape
    eye = jnp.eye(g, dtype=w.dtype)
    return (eye[:, None, :, None] * w[:, :, None, :]).reshape(g * r, g * c)


def _s5_discretise(log_dt, a_re, a_im, b_re, b_im):
    step = jnp.exp(log_dt)[:, None]
    mag = jnp.exp(a_re * step)
    ang = a_im * step
    abr = mag * jnp.cos(ang)
    abi = mag * jnp.sin(ang)
    den = a_re * a_re + a_im * a_im
    zr = ((abr - 1.0) * a_re + abi * a_im) / den
    zi = (abi * a_re - (abr - 1.0) * a_im) / den
    bbr = zr[..., None] * b_re - zi[..., None] * b_im
    bbi = zr[..., None] * b_im + zi[..., None] * b_re
    return abr, abi, bbr, bbi


def _prep_layer(P, l):
    row = lambda a: a.reshape(1, -1)
    abr, abi, bbr, bbi = _s5_discretise(P['s5_log_dt'][l], P['s5_a_re'][l], P['s5_a_im'][l],
                                        P['s5_b_re'][l], P['s5_b_im'][l])
    on = P['out_norm'][l]
    bones = _block_diag(jnp.full((ATT_HEADS, HEAD_DIM, HEAD_DIM), 1.0 / HEAD_DIM, F32)).astype(jnp.bfloat16)
    return {
        'norm_mix': row(P['norm_mix'][l]), 'norm_ffn': row(P['norm_ffn'][l]),
        'w_in': P['w_in'][l].astype(MXU_DT), 'w_out': P['w_out'][l].astype(MXU_DT),
        'bones': bones,
        'qg': row(jnp.tile(P['q_norm'][l], ATT_HEADS)), 'kg': row(jnp.tile(P['k_norm'][l], ATT_HEADS)),
        'lru_conv_w': P['lru_conv_w'][l], 'lru_conv_b': row(P['lru_conv_b'][l]),
        'lru_wa_blk': _block_diag(P['lru_wa'][l]).astype(MXU_DT), 'lru_ba': row(P['lru_ba'][l]),
        'lru_wx_blk': _block_diag(P['lru_wx'][l]).astype(MXU_DT), 'lru_bx': row(P['lru_bx'][l]),
        'lru_lambda': row(P['lru_lambda'][l]), 'sconv_w': P['sconv_w'][l],
        'on_a': row(on[0:W_GRP]), 'on_b': row(on[W_GRP:2 * W_GRP]),
        'on_c': row(on[2 * W_GRP:3 * W_GRP]), 'on_d': row(on[3 * W_GRP:]),
        's5_bb': jnp.concatenate([_block_diag(bbr.transpose(0, 2, 1)), _block_diag(bbi.transpose(0, 2, 1))],
                                 axis=1).astype(MXU_DT),
        's5_cre': _block_diag(P['s5_c_re'][l].transpose(0, 2, 1)).astype(MXU_DT),
        's5_cim': _block_diag(P['s5_c_im'][l].transpose(0, 2, 1)).astype(MXU_DT),
        's5_abr': row(abr), 's5_abi': row(abi), 's5_d': row(P['s5_d'][l]),
        's5_glu_w': P['s5_glu_w'][l].astype(MXU_DT), 's5_glu_b': row(P['s5_glu_b'][l]),
        'router_w': jnp.zeros((D_MODEL, ROUTER_LANES), F32)
                       .at[:, :N_GROUPS].set(P['router_g_w'][l])
                       .at[:, N_GROUPS:N_GROUPS + N_EXPERTS].set(P['router_e_w'][l]).astype(MXU_DT),
        'router_b': jnp.zeros((1, ROUTER_LANES), F32)
                       .at[0, :N_GROUPS].set(P['router_g_b'][l])
                       .at[0, N_GROUPS:N_GROUPS + N_EXPERTS].set(P['router_e_b'][l]),
    }


TOKEN_TILE = 512
SEQ_TILE = 256

_PARAM_NAMES = ('rel_bias', 'mod_w', 'mod_b', 'norm_mix', 'norm_ffn', 'w_in', 'lru_conv_w', 'lru_conv_b',
                'lru_wa', 'lru_ba', 'lru_wx', 'lru_bx', 'lru_lambda', 'sconv_w', 's5_log_dt', 's5_a_re',
                's5_a_im', 's5_b_re', 's5_b_im', 's5_c_re', 's5_c_im', 's5_d', 's5_glu_w', 's5_glu_b',
                'q_norm', 'k_norm', 'out_norm', 'w_out', 'router_g_w', 'router_g_b', 'router_e_w',
                'router_e_b', 'moe_w_gate', 'moe_w_up', 'moe_w_down')


def _mixers(x, mod, lp, st, attn_fn, *, batch, seq, tm, tl, rows_per_mod, dils=()):
    res = _in_proj(x, mod, lp['norm_mix'], lp['w_in'], lp['bones'], lp['qg'], lp['kg'],
                   tm=tm, rows_per_mod=rows_per_mod, dils=dils, seq=seq)
    zm, q, kf, vf, kb, vb = res[:6]
    zm3 = zm.reshape(batch, seq, 6 * W_GRP)
    oa, ob, lru_h, lru_conv, sconv = _mixer_ab(zm3, lp, st['lru_h'], st['lru_conv'], st['sconv'], tl=tl)
    oc, s5_re, s5_im = _s5_mixer(zm3, lp, st['s5_re'], st['s5_im'], tl=tl)
    r3 = lambda t: t.reshape(batch, seq, W_GRP)
    od = attn_fn(r3(q), r3(kf), r3(vf), r3(kb), r3(vb), res[6:])
    flat = lambda t: t.reshape(batch * seq, W_GRP)
    new_st = {'lru_h': lru_h[:, 0], 'lru_conv': lru_conv, 'sconv': sconv,
              's5_re': s5_re.reshape(batch, S5_GROUPS, S5_STATE), 's5_im': s5_im.reshape(batch, S5_GROUPS, S5_STATE),
              'win_k': kf.reshape(batch, seq, ATT_HEADS, HEAD_DIM), 'win_v': vf.reshape(batch, seq, ATT_HEADS, HEAD_DIM)}
    return [flat(oa), flat(ob), flat(oc), flat(od)], new_st


def kernel(x_prompt, x_sample, c_prompt, c_sample, state_lru_h, state_lru_conv, state_sconv, state_s5_re, state_s5_im, cache_win_k, cache_win_v, rel_bias, mod_w, mod_b, norm_mix, norm_ffn, w_in, lru_conv_w, lru_conv_b, lru_wa, lru_ba, lru_wx, lru_bx, lru_lambda, sconv_w, s5_log_dt, s5_a_re, s5_a_im, s5_b_re, s5_b_im, s5_c_re, s5_c_im, s5_d, s5_glu_w, s5_glu_b, q_norm, k_norm, out_norm, w_out, router_g_w, router_g_b, router_e_w, router_e_b, moe_w_gate, moe_w_up, moe_w_down):
    P = dict(zip(_PARAM_NAMES, (rel_bias, mod_w, mod_b, norm_mix, norm_ffn, w_in, lru_conv_w, lru_conv_b,
                                lru_wa, lru_ba, lru_wx, lru_bx, lru_lambda, sconv_w, s5_log_dt, s5_a_re,
                                s5_a_im, s5_b_re, s5_b_im, s5_c_re, s5_c_im, s5_d, s5_glu_w, s5_glu_b,
                                q_norm, k_norm, out_norm, w_out, router_g_w, router_g_b, router_e_w,
                                router_e_b, moe_w_gate, moe_w_up, moe_w_down)))
    bp, seq, d = x_prompt.shape
    bs, dec, _ = x_sample.shape
    depth = mod_w.shape[0]
    tp, ts = bp * seq, bs * dec
    wb = cache_win_k.shape[2]
    wp = min(PATTERNS[-1][0], seq)
    tm_p = min(TOKEN_TILE, tp)
    tm_s = min(TOKEN_TILE, ts)
    tl_p = min(SEQ_TILE, seq)

    nc = -(-(bp + bs) // 8) * 8
    c_all = jnp.zeros((nc, d), F32).at[:bp].set(c_prompt).at[bp:bp + bs].set(c_sample)
    mod_all = _modulation(c_all, mod_w, mod_b)
    bias_p = _bias_table(rel_bias, _prompt_bias_codes()).reshape(len(PATTERNS), 2, ATT_HEADS * Q_BLK, 2 * Q_BLK)
    dils_p = tuple(dil for _, dil in PATTERNS if dil > 1)
    (lw_codes, lw_logm), (lwn_codes, lwn_logm) = _sample_bias_codes(dec, wb)
    lw_s = (_bias_table(rel_bias, lw_codes) + lw_logm).reshape(ATT_HEADS, dec, wb)
    lwn_s = (_bias_table(rel_bias, lwn_codes) + lwn_logm).reshape(ATT_HEADS, dec, dec)
    tri = jnp.asarray(np.tril(np.ones((TOKEN_TILE, TOKEN_TILE), np.float32), -1), jnp.bfloat16)
    n_blk = (2 * (tp + ts)) // MOE_ROWS + N_EXPERTS

    zero_st = {'lru_h': jnp.zeros((bp, 1, W_GRP), F32), 'lru_conv': jnp.zeros((bp, LRU_CONV - 1, W_GRP), F32),
               'sconv': jnp.zeros((bp, SCONV_W - 1, W_GRP), F32),
               's5_re': jnp.zeros((bp, 1, S5_N), F32), 's5_im': jnp.zeros((bp, 1, S5_N), F32)}
    names = ('lru_h', 'lru_conv', 'sconv', 's5_re', 's5_im', 'win_k', 'win_v')
    acc_p = {n: [] for n in names}
    acc_s = {n: [] for n in names}
    xp = x_prompt.reshape(tp, d)
    xs = x_sample.reshape(ts, d)
    for l in range(depth):
        lp = _prep_layer(P, l)
        lp['tri'] = tri
        m6 = mod_all[l].reshape(nc, 6, d).transpose(1, 0, 2)
        mod_p = m6[:, :bp].reshape(6, bp, 1, d)
        mod_s = jnp.repeat(m6[:, bp:bp + bs], dec, axis=1)
        attn_p = lambda q, kf, vf, kb, vb, ex: _attn_prompt(
            [(q[:, None], kb[:, None], vb[:, None])] + [tuple(ex[3 * t:3 * t + 3]) for t in range(len(dils_p))],
            bias_p, lp['on_d'], tm=tm_p)
        pieces_p, st_p = _mixers(xp, mod_p, lp, zero_st, attn_p, batch=bp, seq=seq, tm=tm_p, tl=tl_p,
                                 rows_per_mod=seq, dils=dils_p)
        st_p['win_k'] = st_p['win_k'][:, seq - wp:]
        st_p['win_v'] = st_p['win_v'][:, seq - wp:]
        cnt0 = jnp.zeros((1, ROUTER_LANES), F32)
        x1p, h2p, eid_p, gate_p, rank_p, cnt = _post_mix(xp, pieces_p, mod_p, lp, cnt0, tm=tm_p, rows_per_mod=seq)
        samp_st = {'lru_h': state_lru_h[l][:, None], 'lru_conv': state_lru_conv[l], 'sconv': state_sconv[l],
                   's5_re': state_s5_re[l].reshape(bs, 1, S5_N), 's5_im': state_s5_im[l].reshape(bs, 1, S5_N)}
        attn_s = lambda q, kf, vf, kb, vb, ex: _attn_sample(
            q, kf, vf, cache_win_k, cache_win_v, l, lw_s, lwn_s, lp['on_d'])
        pieces_s, st_s = _mixers(xs, mod_s, lp, samp_st, attn_s, batch=bs, seq=dec, tm=tm_s, tl=dec,
                                 rows_per_mod=dec)
        x1s, h2s, eid_s, gate_s, rank_s, cnt = _post_mix(xs, pieces_s, mod_s, lp, cnt, tm=tm_s, rows_per_mod=dec)
        dests, blk_e, n_used = _routing_tables(cnt, [eid_p, eid_s], [rank_p, rank_s], n_blk)
        slots = jnp.zeros((n_blk * MOE_ROWS * ROW_TILE, LANES), F32)
        slots = _dispatch(h2p, dests[0], slots, tm=tm_p)
        slots = _dispatch(h2s, dests[1], slots, tm=tm_s)
        ys = _experts(slots, blk_e, n_used, moe_w_gate, moe_w_up, moe_w_down, l)
        xp = _combine(ys, dests[0], x1p, gate_p, mod_p, tm=tm_p, rows_per_mod=seq)
        xs = _combine(ys, dests[1], x1s, gate_s, mod_s, tm=tm_s, rows_per_mod=dec)
        for n in names:
            acc_p[n].append(st_p[n])
            acc_s[n].append(st_s[n])
    new_p = {n: jnp.stack(acc_p[n], axis=0) for n in names}
    new_s = {n: jnp.stack(acc_s[n], axis=0) for n in names}
    return (xp.reshape(bp, seq, d), xs.reshape(bs, dec, d),
            new_p['lru_h'], new_p['lru_conv'], new_p['sconv'], new_p['s5_re'], new_p['s5_im'],
            new_p['win_k'], new_p['win_v'],
            new_s['lru_h'], new_s['lru_conv'], new_s['sconv'], new_s['s5_re'], new_s['s5_im'],
            new_s['win_k'], new_s['win_v'])
```

```python
import functools
import math

import numpy as np
import jax
import jax.numpy as jnp
from jax import lax
from jax.experimental import pallas as pl
from jax.experimental.pallas import tpu as pltpu

F32 = jnp.float32
MXU_DT = jnp.bfloat16
HIGHEST = lax.Precision.HIGHEST

D_MODEL = 1024
DEPTH = 4
W_GRP = 256
N_Z = 9
LRU_HEADS = 4
LRU_CONV = 4
LRU_C = 8.0
SCONV_W = 3
S5_CH = 16
S5_GROUPS = 16
S5_STATE = 64
S5_N = S5_GROUPS * S5_STATE
ATT_HEADS = 4
HEAD_DIM = 64
PATTERNS = ((128, 1), (512, 4), (2048, 16))
Q_BLK = 128
REL_BUCKETS = 32
REL_MAX_DIST = 2048
N_GROUPS = 4
EXP_PER_GROUP = 8
N_EXPERTS = 32
D_EXPERT = 512
EPS = 1e-6
NEG = -1e30

VMEM_LIMIT = 56 * 1024 * 1024
LANES = 128
MOE_ROWS = 512
SEGS = 8
ATTN_SUB_BLOCKS = 4


def _cparams(sem):
    return pltpu.CompilerParams(dimension_semantics=sem, vmem_limit_bytes=VMEM_LIMIT)


def _gelu(x):
    return 0.5 * x * (1.0 + jnp.tanh(math.sqrt(2.0 / math.pi) * (x + 0.044715 * (x * x * x))))


def _sigmoid(x):
    return 1.0 / (1.0 + jnp.exp(-x))


def _rms_rows(x):
    return x * lax.rsqrt(jnp.mean(x * x, axis=-1, keepdims=True) + EPS)


def _shift_rows(x, s, fill, row):
    return jnp.where(row >= s, pltpu.roll(x, s, 0), fill)


def _mod_body(c_ref, w_ref, b_ref, o_ref):
    c = c_ref[...]
    s = c * _sigmoid(c)
    o_ref[...] = jnp.dot(s.astype(MXU_DT), w_ref[...].astype(MXU_DT), preferred_element_type=F32) + b_ref[...]


def _modulation(c_all, mod_w, mod_b):
    nb = c_all.shape[0]
    depth = mod_w.shape[0]
    n_out = mod_w.shape[2]
    tn = D_MODEL
    return pl.pallas_call(
        _mod_body,
        grid=(depth, n_out // tn),
        in_specs=[pl.BlockSpec((nb, D_MODEL), lambda l, j: (0, 0)),
                  pl.BlockSpec((None, D_MODEL, tn), lambda l, j: (l, 0, j)),
                  pl.BlockSpec((None, 1, tn), lambda l, j: (l, 0, j))],
        out_specs=pl.BlockSpec((None, nb, tn), lambda l, j: (l, 0, j)),
        out_shape=jax.ShapeDtypeStruct((depth, nb, n_out), F32),
        compiler_params=_cparams(("arbitrary", "arbitrary")),
        name="modulation",
    )(c_all, mod_w, mod_b.reshape(depth, 1, n_out))


def _head_mean_sq(t, bones):
    sq = t * t
    hi = sq.astype(jnp.bfloat16)
    lo = (sq - hi.astype(F32)).astype(jnp.bfloat16)
    return (jnp.dot(hi, bones, preferred_element_type=F32)
            + jnp.dot(lo, bones, preferred_element_type=F32))


def _inproj_body(*refs, dils):
    (x_ref, sc_ref, sh_ref, g_ref, w_ref, bones_ref, qg_ref, kg_ref,
     zm_ref, q_ref, kf_ref, vf_ref, kb_ref, vb_ref) = refs[:14]
    x = x_ref[...]
    h = _rms_rows(x) * g_ref[...]
    h = h * (1.0 + sc_ref[...]) + sh_ref[...]
    z = jnp.dot(h.astype(MXU_DT), w_ref[...], preferred_element_type=F32)
    nm = 6 * W_GRP
    zm_ref[...] = z[:, :nm]
    q = z[:, nm:nm + W_GRP]
    k = z[:, nm + W_GRP:nm + 2 * W_GRP]
    v = z[:, nm + 2 * W_GRP:]
    bones = bones_ref[...]
    qn = (q * lax.rsqrt(_head_mean_sq(q, bones) + EPS) * qg_ref[...]) * (HEAD_DIM ** -0.5)
    kn = k * lax.rsqrt(_head_mean_sq(k, bones) + EPS) * kg_ref[...]
    q_ref[...] = qn.astype(q_ref.dtype)
    kf_ref[...] = kn
    vf_ref[...] = v
    kb_ref[...] = kn.astype(kb_ref.dtype)
    vb_ref[...] = v.astype(vb_ref.dtype)
    if dils:
        stage_ref = refs[-1]
        tm = x_ref.shape[0]
        for a, val in enumerate((qn, kn, v)):
            for half in range(W_GRP // LANES):
                stage_ref[a, half] = val[:, half * LANES:(half + 1) * LANES]
        for di, dil in enumerate(dils):
            for a in range(3):
                out_ref = refs[14 + 3 * di + a]
                for r in range(dil):
                    for half in range(W_GRP // LANES):
                        out_ref[r, :, half * LANES:(half + 1) * LANES] = (
                            stage_ref.at[a, half][pl.ds(r, tm // dil, stride=dil), :].astype(out_ref.dtype))


def _mod_spec(mod, k, tm, rows_per_mod):
    if mod.ndim == 4:
        return pl.BlockSpec((None, None, 1, D_MODEL), lambda i: (k, (i * tm) // rows_per_mod, 0, 0))
    return pl.BlockSpec((None, tm, D_MODEL), lambda i: (k, i, 0))


def _in_proj(x, mod, norm_g, w_in, bones, qg, kg, *, tm, rows_per_mod, dils=(), seq=None):
    t = x.shape[0]
    n_in = w_in.shape[1]
    nm = 6 * W_GRP
    row = lambda i: (i, 0)
    const = lambda i: (0, 0)
    outs = [jax.ShapeDtypeStruct((t, nm), F32),
            jax.ShapeDtypeStruct((t, W_GRP), MXU_DT),
            jax.ShapeDtypeStruct((t, W_GRP), F32),
            jax.ShapeDtypeStruct((t, W_GRP), F32),
            jax.ShapeDtypeStruct((t, W_GRP), MXU_DT),
            jax.ShapeDtypeStruct((t, W_GRP), MXU_DT)]
    out_specs = [pl.BlockSpec((tm, nm), row)] + [pl.BlockSpec((tm, W_GRP), row)] * 5
    scratch = []
    if dils:
        tiles_per_seq = seq // tm
        for dil in dils:
            outs += [jax.ShapeDtypeStruct((t // seq, dil, seq // dil, W_GRP), MXU_DT)] * 3
            out_specs += [pl.BlockSpec((None, dil, tm // dil, W_GRP),
                                       lambda i: (i // tiles_per_seq, 0, i % tiles_per_seq, 0))] * 3
        scratch = [pltpu.VMEM((3, W_GRP // LANES, tm, LANES), F32)]
    return pl.pallas_call(
        functools.partial(_inproj_body, dils=tuple(dils)),
        grid=(t // tm,),
        in_specs=[pl.BlockSpec((tm, D_MODEL), row),
                  _mod_spec(mod, 1, tm, rows_per_mod),
                  _mod_spec(mod, 0, tm, rows_per_mod),
                  pl.BlockSpec((1, D_MODEL), const),
                  pl.BlockSpec((D_MODEL, n_in), const),
                  pl.BlockSpec((W_GRP, W_GRP), const),
                  pl.BlockSpec((1, W_GRP), const),
                  pl.BlockSpec((1, W_GRP), const)],
        out_specs=out_specs,
        out_shape=outs,
        scratch_shapes=scratch,
        compiler_params=_cparams(("arbitrary",)),
        name="in_proj",
    )(x, mod, mod, norm_g, w_in, bones, qg, kg)


def _softplus(x):
    return jnp.maximum(x, 0.0) + jnp.log(1.0 + jnp.exp(-jnp.abs(x)))


def _mixer_ab_body(xa_ref, ga_ref, gb_ref, gc_ref, xb_ref,
                   cw_ref, cb_ref, wa_ref, ba_ref, wx_ref, bx_ref, lam_ref, sw_ref, on_a_ref, on_b_ref,
                   h0_ref, conv0_ref, sconv0_ref,
                   oa_ref, ob_ref, hn_ref, convn_ref, sconvn_ref,
                   xe_ref, pe_ref, hc_ref):
    i = pl.program_id(1)
    tl = xa_ref.shape[0]

    @pl.when(i == 0)
    def _():
        xe_ref[8 - (LRU_CONV - 1):8, :] = conv0_ref[...]
        pe_ref[8 - (SCONV_W - 1):8, :] = sconv0_ref[...]
        hc_ref[...] = h0_ref[...]

    row = lax.broadcasted_iota(jnp.int32, (tl, 1), 0)
    xa = xa_ref[...]
    xe_ref[8:, :] = xa
    cw = cw_ref[...]
    xc = cw[LRU_CONV - 1:LRU_CONV, :] * xa
    for s in range(1, LRU_CONV):
        xc = xc + cw[LRU_CONV - 1 - s:LRU_CONV - s, :] * xe_ref[8 - s:8 - s + tl, :]
    xc = xc + cb_ref[...]
    convn_ref[...] = xa[tl - (LRU_CONV - 1):, :]
    xe_ref[0:8, :] = xa[tl - 8:, :]
    xcb = xc.astype(MXU_DT)
    r = _sigmoid(jnp.dot(xcb, wa_ref[...], preferred_element_type=F32) + ba_ref[...])
    ig = _sigmoid(jnp.dot(xcb, wx_ref[...], preferred_element_type=F32) + bx_ref[...])
    log_a = (-LRU_C * r) * _softplus(-lam_ref[...])
    a = jnp.exp(log_a)
    b = jnp.sqrt(-jnp.tanh(log_a) * (a * a + 1.0)) * (ig * xc)
    s = 1
    while s < tl:
        b = a * _shift_rows(b, s, 0.0, row) + b
        a = a * _shift_rows(a, s, 1.0, row)
        s *= 2
    h = b + a * hc_ref[...]
    hc_ref[...] = h[tl - 1:, :]
    hn_ref[...] = h[tl - 1:, :]
    out_a = h * _gelu(ga_ref[...])
    oa_ref[...] = (_rms_rows(out_a) * on_a_ref[...]).astype(oa_ref.dtype)
    p = gc_ref[...] * xb_ref[...]
    pe_ref[8:, :] = p
    sw = sw_ref[...]
    yb = sw[SCONV_W - 1:SCONV_W, :] * p
    for s in range(1, SCONV_W):
        yb = yb + sw[SCONV_W - 1 - s:SCONV_W - s, :] * pe_ref[8 - s:8 - s + tl, :]
    sconvn_ref[...] = p[tl - (SCONV_W - 1):, :]
    pe_ref[0:8, :] = p[tl - 8:, :]
    out_b = gb_ref[...] * yb
    ob_ref[...] = (_rms_rows(out_b) * on_b_ref[...]).astype(ob_ref.dtype)


def _mixer_ab(zm, lp, h0, conv0, sconv0, *, tl):
    bsz, seq, _ = zm.shape
    col = lambda c: pl.BlockSpec((None, tl, W_GRP), lambda b, i, c=c: (b, i, c))
    const = lambda shp: pl.BlockSpec(shp, lambda b, i: (0,) * len(shp))
    per_b = lambda n: pl.BlockSpec((None, n, W_GRP), lambda b, i: (b, 0, 0))
    outs = [jax.ShapeDtypeStruct((bsz, seq, W_GRP), MXU_DT),
            jax.ShapeDtypeStruct((bsz, seq, W_GRP), MXU_DT),
            jax.ShapeDtypeStruct((bsz, 1, W_GRP), F32),
            jax.ShapeDtypeStruct((bsz, LRU_CONV - 1, W_GRP), F32),
            jax.ShapeDtypeStruct((bsz, SCONV_W - 1, W_GRP), F32)]
    return pl.pallas_call(
        _mixer_ab_body,
        grid=(bsz, seq // tl),
        in_specs=[col(0), col(1), col(2), col(3), col(4),
                  const((LRU_CONV, W_GRP)), const((1, W_GRP)),
                  const((W_GRP, W_GRP)), const((1, W_GRP)),
                  const((W_GRP, W_GRP)), const((1, W_GRP)),
                  const((1, W_GRP)), const((SCONV_W, W_GRP)),
                  const((1, W_GRP)), const((1, W_GRP)),
                  per_b(1), per_b(LRU_CONV - 1), per_b(SCONV_W - 1)],
        out_specs=[pl.BlockSpec((None, tl, W_GRP), lambda b, i: (b, i, 0)),
                   pl.BlockSpec((None, tl, W_GRP), lambda b, i: (b, i, 0)),
                   per_b(1), per_b(LRU_CONV - 1), per_b(SCONV_W - 1)],
        out_shape=outs,
        scratch_shapes=[pltpu.VMEM((tl + 8, W_GRP), F32),
                        pltpu.VMEM((tl + 8, W_GRP), F32),
                        pltpu.VMEM((1, W_GRP), F32)],
        compiler_params=_cparams(("arbitrary", "arbitrary")),
        name="mixer_ab",
    )(zm, zm, zm, zm, zm,
      lp['lru_conv_w'], lp['lru_conv_b'], lp['lru_wa_blk'], lp['lru_ba'], lp['lru_wx_blk'], lp['lru_bx'],
      lp['lru_lambda'], lp['sconv_w'], lp['on_a'], lp['on_b'],
      h0, conv0, sconv0)


def _s5_body(u_ref, bb_ref, cre_ref, cim_ref, ar_ref, ai_ref, d_ref, gw_ref, gb_ref, on_ref,
             h0r_ref, h0i_ref,
             o_ref, hnr_ref, hni_ref,
             tr_ref, ti_ref, pr_ref, pi_ref, hr_ref, hi_ref, lr_ref, li_ref, stage_ref):
    b = pl.program_id(0)
    i = pl.program_id(1)
    tl = u_ref.shape[0]
    steps = tl // SEGS
    groups = S5_N // LANES
    halves = W_GRP // LANES
    ar = ar_ref[...]
    ai = ai_ref[...]

    def powers(base_r, base_i, n, first):
        row = lax.broadcasted_iota(jnp.int32, (n, 1), 0)
        tr = jnp.broadcast_to(base_r, (n, S5_N))
        ti = jnp.broadcast_to(base_i, (n, S5_N))
        s = first
        while s < n:
            sr = _shift_rows(tr, s, 1.0, row)
            si = _shift_rows(ti, s, 0.0, row)
            tr, ti = tr * sr - ti * si, tr * si + ti * sr
            s *= 2
        return tr, ti

    @pl.when((b == 0) & (i == 0))
    def _():
        tr, ti = powers(ar, ai, tl, SEGS)
        tr_ref[...] = tr
        ti_ref[...] = ti
        pr, pi = powers(tr[tl - 1:, :], ti[tl - 1:, :], SEGS, 1)
        pr_ref[...] = pr
        pi_ref[...] = pi

    @pl.when(i == 0)
    def _():
        hr_ref[...] = h0r_ref[...]
        hi_ref[...] = h0i_ref[...]

    u = u_ref[...]
    if steps > 1:
        for half in range(halves):
            stage_ref[half] = u[:, half * LANES:(half + 1) * LANES]
        u = jnp.concatenate(
            [jnp.concatenate([stage_ref.at[half][pl.ds(j, SEGS, stride=steps), :] for half in range(halves)], axis=1)
             for j in range(steps)], axis=0)
    bu = jnp.dot(u.astype(MXU_DT), bb_ref[...], preferred_element_type=F32)
    a_r = [jnp.broadcast_to(ar[:, c * LANES:(c + 1) * LANES], (SEGS, LANES)) for c in range(groups)]
    a_i = [jnp.broadcast_to(ai[:, c * LANES:(c + 1) * LANES], (SEGS, LANES)) for c in range(groups)]
    loc_r = [jnp.zeros((SEGS, LANES), F32) for _ in range(groups)]
    loc_i = [jnp.zeros((SEGS, LANES), F32) for _ in range(groups)]
    for j in range(steps):
        rows = slice(j * SEGS, (j + 1) * SEGS)
        for c in range(groups):
            cols = slice(c * LANES, (c + 1) * LANES)
            nr = (a_r[c] * loc_r[c] - a_i[c] * loc_i[c]) + bu[rows, c * LANES:(c + 1) * LANES]
            ni = (a_r[c] * loc_i[c] + a_i[c] * loc_r[c]) + bu[rows, S5_N + c * LANES:S5_N + (c + 1) * LANES]
            loc_r[c], loc_i[c] = nr, ni
            lr_ref[rows, cols] = nr
            li_ref[rows, cols] = ni
    er = jnp.concatenate(loc_r, axis=1)
    ei = jnp.concatenate(loc_i, axis=1)
    seg = lax.broadcasted_iota(jnp.int32, (SEGS, 1), 0)
    pr = pr_ref[...]
    pi = pi_ref[...]
    mr, mi = pr[0:1, :], pi[0:1, :]
    s = 1
    while s < SEGS:
        sr = _shift_rows(er, s, 0.0, seg)
        si = _shift_rows(ei, s, 0.0, seg)
        er, ei = er + (mr * sr - mi * si), ei + (mr * si + mi * sr)
        mr, mi = mr * mr - mi * mi, 2.0 * (mr * mi)
        s *= 2
    cr = hr_ref[...]
    ci = hi_ref[...]
    er, ei = er + (pr * cr - pi * ci), ei + (pr * ci + pi * cr)
    in_r = jnp.where(seg >= 1, pltpu.roll(er, 1, 0), cr)
    in_i = jnp.where(seg >= 1, pltpu.roll(ei, 1, 0), ci)
    hr_ref[...] = er[SEGS - 1:, :]
    hi_ref[...] = ei[SEGS - 1:, :]
    hnr_ref[...] = er[SEGS - 1:, :]
    hni_ref[...] = ei[SEGS - 1:, :]
    tr = tr_ref[...]
    ti = ti_ref[...]
    sr = jnp.tile(in_r, (steps, 1))
    si = jnp.tile(in_i, (steps, 1))
    hr = lr_ref[...] + (tr * sr - ti * si)
    hi = li_ref[...] + (tr * si + ti * sr)
    y = (jnp.dot(hr.astype(MXU_DT), cre_ref[...], preferred_element_type=F32)
         - jnp.dot(hi.astype(MXU_DT), cim_ref[...], preferred_element_type=F32)) + d_ref[...] * u
    g = jnp.dot(_gelu(y).astype(MXU_DT), gw_ref[...], preferred_element_type=F32) + gb_ref[...]
    out = g[:, :W_GRP] * _sigmoid(g[:, W_GRP:])
    out = _rms_rows(out) * on_ref[...]
    if steps > 1:
        for half in range(halves):
            stage_ref[half] = out[:, half * LANES:(half + 1) * LANES]
        out = jnp.concatenate(
            [jnp.concatenate([stage_ref.at[half][pl.ds(sg, steps, stride=SEGS), :] for half in range(halves)], axis=1)
             for sg in range(SEGS)], axis=0)
    o_ref[...] = out.astype(o_ref.dtype)


def _s5_mixer(zm, lp, h0r, h0i, *, tl):
    bsz, seq, _ = zm.shape
    const = lambda shp: pl.BlockSpec(shp, lambda b, i: (0,) * len(shp))
    per_b = pl.BlockSpec((None, 1, S5_N), lambda b, i: (b, 0, 0))
    outs = [jax.ShapeDtypeStruct((bsz, seq, W_GRP), MXU_DT),
            jax.ShapeDtypeStruct((bsz, 1, S5_N), F32),
            jax.ShapeDtypeStruct((bsz, 1, S5_N), F32)]
    return pl.pallas_call(
        _s5_body,
        grid=(bsz, seq // tl),
        in_specs=[pl.BlockSpec((None, tl, W_GRP), lambda b, i: (b, i, 5)),
                  const((W_GRP, 2 * S5_N)), const((S5_N, W_GRP)), const((S5_N, W_GRP)),
                  const((1, S5_N)), const((1, S5_N)), const((1, W_GRP)),
                  const((W_GRP, 2 * W_GRP)), const((1, 2 * W_GRP)), const((1, W_GRP)),
                  per_b, per_b],
        out_specs=[pl.BlockSpec((None, tl, W_GRP), lambda b, i: (b, i, 0)), per_b, per_b],
        out_shape=outs,
        scratch_shapes=[pltpu.VMEM((tl, S5_N), F32), pltpu.VMEM((tl, S5_N), F32),
                        pltpu.VMEM((SEGS, S5_N), F32), pltpu.VMEM((SEGS, S5_N), F32),
                        pltpu.VMEM((1, S5_N), F32), pltpu.VMEM((1, S5_N), F32),
                        pltpu.VMEM((tl, S5_N), F32), pltpu.VMEM((tl, S5_N), F32),
                        pltpu.VMEM((W_GRP // LANES, tl, LANES), F32)],
        compiler_params=_cparams(("arbitrary", "arbitrary")),
        name="s5_mixer",
    )(zm, lp['s5_bb'], lp['s5_cre'], lp['s5_cim'], lp['s5_abr'], lp['s5_abi'], lp['s5_d'],
      lp['s5_glu_w'], lp['s5_glu_b'], lp['on_c'], h0r, h0i)


CODE_MASKED = -1
CODE_ZERO = -2


def _bias_body(rb_ref, code_ref, o_ref):
    code = code_ref[...]
    acc = jnp.where(code == CODE_MASKED, NEG, 0.0).astype(F32)
    for c in range(REL_BUCKETS * ATT_HEADS):
        acc = jnp.where(code == c, rb_ref[c], acc)
    o_ref[...] = acc


def _bias_table(rel_bias, codes):
    rows, cols = codes.shape
    tr = max(t for t in range(8, 513, 8) if rows % t == 0)
    return pl.pallas_call(
        _bias_body,
        grid_spec=pltpu.PrefetchScalarGridSpec(
            num_scalar_prefetch=1,
            grid=(rows // tr,),
            in_specs=[pl.BlockSpec((tr, cols), lambda i, rb: (i, 0))],
            out_specs=pl.BlockSpec((tr, cols), lambda i, rb: (i, 0))),
        out_shape=jax.ShapeDtypeStruct((rows, cols), F32),
        compiler_params=_cparams(("arbitrary",)),
        name="bias_table",
    )(rel_bias.reshape(-1), jnp.asarray(codes))


def _t5_bucket(n):
    n = np.asarray(n).astype(np.int32)
    max_exact = REL_BUCKETS // 2
    nf = np.maximum(n, 1).astype(np.float32)
    large = max_exact + (np.log(nf / max_exact) / np.log(REL_MAX_DIST / max_exact)
                         * (REL_BUCKETS - max_exact)).astype(np.int32)
    large = np.minimum(large, REL_BUCKETS - 1)
    return np.where(n < max_exact, n, large).astype(np.int32)


def _prompt_bias_codes():
    i = np.arange(Q_BLK)[:, None]
    j = np.arange(2 * Q_BLK)[None, :]
    rel = Q_BLK + i - j
    out = np.zeros((len(PATTERNS), 2, ATT_HEADS, Q_BLK, 2 * Q_BLK), np.int32)
    for p, (win, dil) in enumerate(PATTERNS):
        span = win // dil
        valid = (rel >= 0) & (rel <= span)
        bucket = _t5_bucket(np.clip(rel, 0, None) * dil)
        for var in range(2):
            v = valid & ((j >= Q_BLK) | (var == 1))
            for h in range(ATT_HEADS):
                out[p, var, h] = np.where(v, bucket * ATT_HEADS + h, CODE_MASKED)
    return out.reshape(-1, 2 * Q_BLK)


def _sample_bias_codes(dec_seq, wb):
    def tables(n_keys, first_pos):
        codes = np.full((ATT_HEADS, dec_seq, n_keys), CODE_MASKED, np.int32)
        logm = np.zeros((ATT_HEADS, dec_seq, n_keys), np.float32)
        for s in range(dec_seq):
            dist = (wb + s) - (first_pos + np.arange(n_keys))
            mult = np.zeros(n_keys, np.int32)
            for win, dil in PATTERNS:
                mult += ((dist >= 0) & (dist % dil == 0) & (dist // dil <= win // dil)).astype(np.int32)
            bucket = _t5_bucket(np.clip(dist, 0, None))
            for h in range(ATT_HEADS):
                codes[h, s] = np.where(mult > 0, bucket * ATT_HEADS + h, CODE_MASKED)
                logm[h, s] = np.log(np.maximum(mult, 1))
        return codes.reshape(ATT_HEADS * dec_seq, n_keys), logm.reshape(ATT_HEADS * dec_seq, n_keys)
    return tables(wb, 0), tables(dec_seq, wb)


def _attn_p_body(q_ref, kp_ref, kc_ref, vp_ref, vc_ref, bias_ref, o_ref, lse_ref):
    lane = lax.broadcasted_iota(jnp.int32, (1, W_GRP), 1)
    head_of_lane = [(lane >= h * HEAD_DIM) & (lane < (h + 1) * HEAD_DIM) for h in range(ATT_HEADS)]
    n_sub = q_ref.shape[0] // Q_BLK
    first_variant = jnp.minimum(pl.program_id(2), 1)
    for sub in range(n_sub):
        rows = slice(sub * Q_BLK, (sub + 1) * Q_BLK)
        q = q_ref[rows, :]
        if sub == 0:
            k_prev, v_prev, bias = kp_ref[...], vp_ref[...], bias_ref[first_variant]
        else:
            prev_rows = slice((sub - 1) * Q_BLK, sub * Q_BLK)
            k_prev, v_prev, bias = kc_ref[prev_rows, :], vc_ref[prev_rows, :], bias_ref[1]
        k2 = jnp.concatenate([k_prev, kc_ref[rows, :]], axis=0)
        v2 = jnp.concatenate([v_prev, vc_ref[rows, :]], axis=0)
        qs = jnp.concatenate([jnp.where(hm, q, jnp.zeros_like(q)) for hm in head_of_lane], axis=0)
        s = lax.dot_general(qs, k2, (((1,), (1,)), ((), ())), preferred_element_type=F32) + bias
        m = jnp.max(s, axis=-1, keepdims=True)
        pr = jnp.exp(s - m)
        den = jnp.sum(pr, axis=-1, keepdims=True)
        pv = jnp.dot(pr.astype(MXU_DT), v2, preferred_element_type=F32) / den
        lse_rows = m + jnp.log(den)
        o = jnp.zeros((Q_BLK, W_GRP), F32)
        lse = jnp.zeros((Q_BLK, W_GRP), F32)
        for h, hm in enumerate(head_of_lane):
            o = jnp.where(hm, pv[h * Q_BLK:(h + 1) * Q_BLK], o)
            lse = jnp.where(hm, lse_rows[h * Q_BLK:(h + 1) * Q_BLK], lse)
        o_ref[rows, :] = o
        lse_ref[rows, :] = lse


def _attn_prompt_pattern(q, k, v, bias):
    bsz, dil, md, _ = q.shape
    n_sub = math.gcd(ATTN_SUB_BLOCKS, md // Q_BLK)
    cur = pl.BlockSpec((None, None, n_sub * Q_BLK, W_GRP), lambda b, r, n: (b, r, n, 0))
    prv = pl.BlockSpec((None, None, Q_BLK, W_GRP), lambda b, r, n: (b, r, jnp.maximum(n * n_sub - 1, 0), 0))
    bsp = pl.BlockSpec((2, ATT_HEADS * Q_BLK, 2 * Q_BLK), lambda b, r, n: (0, 0, 0))
    return pl.pallas_call(
        _attn_p_body,
        grid=(bsz, dil, md // (n_sub * Q_BLK)),
        in_specs=[cur, prv, cur, prv, cur, bsp],
        out_specs=[cur, cur],
        out_shape=[jax.ShapeDtypeStruct((bsz, dil, md, W_GRP), F32)] * 2,
        compiler_params=_cparams(("arbitrary", "arbitrary", "arbitrary")),
        name=f"attn_prompt_d{dil}",
    )(q, k, k, v, v, bias)


def _attn_merge_body(*refs, dils):
    n_pat = len(dils)
    on_ref, out_ref, stage_ref = refs[2 * n_pat:]
    tm = out_ref.shape[0]
    halves = W_GRP // LANES
    vals = []
    for t in range(2 * n_pat):
        dil = dils[t // 2]
        if dil == 1:
            vals.append(refs[t][...])
            continue
        for r in range(dil):
            blk = refs[t][r]
            for half in range(halves):
                stage_ref.at[t, half][pl.ds(r, tm // dil, stride=dil), :] = blk[:, half * LANES:(half + 1) * LANES]
        vals.append(jnp.concatenate([stage_ref[t, half] for half in range(halves)], axis=1))
    os_, ls_ = vals[0::2], vals[1::2]
    mx = functools.reduce(jnp.maximum, ls_)
    ws = [jnp.exp(l_ - mx) for l_ in ls_]
    num = functools.reduce(lambda a_, b_: a_ + b_, [w_ * o_ for w_, o_ in zip(ws, os_)])
    merged = num / functools.reduce(lambda a_, b_: a_ + b_, ws)
    out_ref[...] = (_rms_rows(merged) * on_ref[...]).astype(out_ref.dtype)


def _attn_merge(pattern_outs, on_d, *, tm):
    dils = tuple(o.shape[1] for o, _ in pattern_outs)
    bsz, _, seq, _ = pattern_outs[0][0].shape
    seq = seq * dils[0]
    specs, args = [], []
    for (o, lse), dil in zip(pattern_outs, dils):
        if dil == 1:
            sp = pl.BlockSpec((None, None, tm, W_GRP), lambda b, j: (b, 0, j, 0))
        else:
            sp = pl.BlockSpec((None, dil, tm // dil, W_GRP), lambda b, j: (b, 0, j, 0))
        specs += [sp, sp]
        args += [o, lse]
    return pl.pallas_call(
        functools.partial(_attn_merge_body, dils=dils),
        grid=(bsz, seq // tm),
        in_specs=specs + [pl.BlockSpec((1, W_GRP), lambda b, j: (0, 0))],
        out_specs=pl.BlockSpec((None, tm, W_GRP), lambda b, j: (b, j, 0)),
        out_shape=jax.ShapeDtypeStruct((bsz, seq, W_GRP), MXU_DT),
        scratch_shapes=[pltpu.VMEM((2 * len(dils), W_GRP // LANES, tm, LANES), F32)],
        compiler_params=_cparams(("arbitrary", "arbitrary")),
        name="attn_merge",
    )(*args, on_d)


def _attn_prompt(qkv_by_dil, bias_all, on_d, *, tm):
    outs = [_attn_prompt_pattern(q, k, v, bias_all[p]) for p, (q, k, v) in enumerate(qkv_by_dil)]
    return _attn_merge(outs, on_d, tm=tm)


def _attn_s_body(q_ref, kn_ref, vn_ref, kt_ref, vt_ref, lw_ref, lwn_ref, on_ref, o_ref):
    q = q_ref[...]
    kn = kn_ref[...].astype(MXU_DT)
    vn = vn_ref[...].astype(MXU_DT)
    nt = (((1,), (1,)), ((), ()))
    outs = []
    for h in range(ATT_HEADS):
        cols = slice(h * HEAD_DIM, (h + 1) * HEAD_DIM)
        qh = q[:, cols]
        sc = jnp.dot(qh, kt_ref[h].astype(MXU_DT), preferred_element_type=F32) + lw_ref[h]
        scn = lax.dot_general(qh, kn[:, cols], nt, preferred_element_type=F32) + lwn_ref[h]
        m = jnp.maximum(jnp.max(sc, axis=-1, keepdims=True), jnp.max(scn, axis=-1, keepdims=True))
        p = jnp.exp(sc - m)
        pn = jnp.exp(scn - m)
        den = jnp.sum(p, axis=-1, keepdims=True) + jnp.sum(pn, axis=-1, keepdims=True)
        num = (lax.dot_general(p.astype(MXU_DT), vt_ref[h].astype(MXU_DT), nt, preferred_element_type=F32)
               + jnp.dot(pn.astype(MXU_DT), vn[:, cols], preferred_element_type=F32))
        outs.append(num / den)
    merged = jnp.concatenate(outs, axis=1)
    o_ref[...] = (_rms_rows(merged) * on_ref[...]).astype(o_ref.dtype)


def _attn_sample(q, kn, vn, cache_k, cache_v, l, lw, lwn, on_d):
    bsz, dec, _ = q.shape
    wb = cache_k.shape[2]
    per_b = pl.BlockSpec((None, dec, W_GRP), lambda b: (b, 0, 0))
    pos_minor = lambda c: jnp.transpose(c, (0, 1, 3, 4, 2))
    cache_spec = pl.BlockSpec((None, None, ATT_HEADS, HEAD_DIM, wb), lambda b: (l, b, 0, 0, 0))
    full = lambda a: pl.BlockSpec(a.shape, lambda b: (0,) * a.ndim)
    return pl.pallas_call(
        _attn_s_body,
        grid=(bsz,),
        in_specs=[per_b, per_b, per_b, cache_spec, cache_spec, full(lw), full(lwn), full(on_d)],
        out_specs=per_b,
        out_shape=jax.ShapeDtypeStruct((bsz, dec, W_GRP), MXU_DT),
        compiler_params=_cparams(("arbitrary",)),
        name="attn_sample",
    )(q, kn, vn, pos_minor(cache_k), pos_minor(cache_v), lw, lwn, on_d)


ROUTER_LANES = 128
HALF_D = D_MODEL // 2
ROW_TILE = HALF_D // LANES
ROW_DT = jnp.uint32


def _store_row_tiles(ref, val):
    n = val.shape[0]
    bits = lambda t: lax.bitcast_convert_type(t.astype(jnp.bfloat16).astype(F32), ROW_DT)
    packed = (bits(val[:, HALF_D:]) & jnp.uint32(0xFFFF0000)) | (bits(val[:, :HALF_D]) >> 16)
    for j in range(ROW_TILE):
        ref[pl.ds(j, n, stride=ROW_TILE), :] = packed[:, j * LANES:(j + 1) * LANES]


def _load_row_tiles(ref):
    n = ref.shape[0] // ROW_TILE
    packed = jnp.concatenate([ref[pl.ds(j, n, stride=ROW_TILE), :] for j in range(ROW_TILE)], axis=1)
    return (lax.bitcast_convert_type(packed << 16, F32),
            lax.bitcast_convert_type(packed & jnp.uint32(0xFFFF0000), F32))


def _post_mix_body(x_ref, a_ref, b_ref, c_ref, d_ref, w_ref, g1_ref, sc_ref, sh_ref, g_ref,
                   wr_ref, br_ref, tri_ref, cnt_in_ref,
                   x1_ref, h2_ref, eid_ref, gate_ref, rank_ref, cnt_ref,
                   run_ref):
    i = pl.program_id(0)

    @pl.when(i == 0)
    def _():
        run_ref[...] = cnt_in_ref[...]

    mix = jnp.concatenate([a_ref[...], b_ref[...], c_ref[...], d_ref[...]], axis=1)
    y = jnp.dot(mix, w_ref[...], preferred_element_type=F32)
    x1 = x_ref[...] + g1_ref[...] * y
    x1_ref[...] = x1
    h2 = _rms_rows(x1) * g_ref[...]
    h2 = h2 * (1.0 + sc_ref[...]) + sh_ref[...]
    _store_row_tiles(h2_ref, h2)
    logits = jnp.dot(h2.astype(MXU_DT), wr_ref[...], preferred_element_type=F32) + br_ref[...]
    tm = logits.shape[0]
    lane = lax.broadcasted_iota(jnp.int32, (tm, ROUTER_LANES), 1)
    big = jnp.int32(10 ** 6)
    is_g = lane < N_GROUPS
    gl = jnp.where(is_g, logits, -jnp.inf)
    gmax = jnp.max(gl, axis=-1, keepdims=True)
    gsel = jnp.min(jnp.where(gl == gmax, lane, big), axis=-1, keepdims=True)
    gprob = 1.0 / jnp.sum(jnp.where(is_g, jnp.exp(logits - gmax), 0.0), axis=-1, keepdims=True)
    lo_lane = N_GROUPS + gsel * EXP_PER_GROUP
    in_grp = (lane >= lo_lane) & (lane < lo_lane + EXP_PER_GROUP)
    el = jnp.where(in_grp, logits, -jnp.inf)
    v1 = jnp.max(el, axis=-1, keepdims=True)
    i1 = jnp.min(jnp.where(el == v1, lane, big), axis=-1, keepdims=True)
    el2 = jnp.where(lane == i1, -jnp.inf, el)
    v2 = jnp.max(el2, axis=-1, keepdims=True)
    i2 = jnp.min(jnp.where(el2 == v2, lane, big), axis=-1, keepdims=True)
    e2w = jnp.exp(v2 - v1)
    gate1 = (1.0 / (1.0 + e2w)) * gprob
    gate2 = (e2w / (1.0 + e2w)) * gprob
    e1 = i1 - N_GROUPS
    e2 = i2 - N_GROUPS
    oh1 = lane == e1
    oh2 = lane == e2
    both = jnp.where(oh1 | oh2, 1.0, 0.0)
    before = jnp.dot(tri_ref[...], both.astype(jnp.bfloat16), preferred_element_type=F32) + run_ref[...]
    r1 = jnp.sum(jnp.where(oh1, before, 0.0), axis=-1, keepdims=True).astype(jnp.int32)
    r2 = jnp.sum(jnp.where(oh2, before, 0.0), axis=-1, keepdims=True).astype(jnp.int32)
    run = run_ref[...] + jnp.sum(both, axis=0, keepdims=True)
    run_ref[...] = run
    cnt_ref[...] = run
    eid_ref[...] = jnp.where(lane == 0, e1, jnp.where(lane == 1, e2, 0))
    gate_ref[...] = jnp.where(lane == 0, gate1, jnp.where(lane == 1, gate2, 0.0))
    rank_ref[...] = jnp.where(lane == 0, r1, jnp.where(lane == 1, r2, 0))


def _post_mix(x, pieces, mod, lp, cnt_in, *, tm, rows_per_mod):
    t = x.shape[0]
    row = lambda i: (i, 0)
    const = lambda i: (0, 0)
    piece = pl.BlockSpec((tm, W_GRP), row)
    wide = pl.BlockSpec((tm, D_MODEL), row)
    lanes = pl.BlockSpec((tm, ROUTER_LANES), row)
    outs = [jax.ShapeDtypeStruct((t, D_MODEL), F32), jax.ShapeDtypeStruct((t * ROW_TILE, LANES), ROW_DT),
            jax.ShapeDtypeStruct((t, ROUTER_LANES), jnp.int32), jax.ShapeDtypeStruct((t, ROUTER_LANES), F32),
            jax.ShapeDtypeStruct((t, ROUTER_LANES), jnp.int32), jax.ShapeDtypeStruct((1, ROUTER_LANES), F32)]
    return pl.pallas_call(
        _post_mix_body,
        grid=(t // tm,),
        in_specs=[wide, piece, piece, piece, piece,
                  pl.BlockSpec((D_MODEL, D_MODEL), const),
                  _mod_spec(mod, 2, tm, rows_per_mod), _mod_spec(mod, 4, tm, rows_per_mod),
                  _mod_spec(mod, 3, tm, rows_per_mod),
                  pl.BlockSpec((1, D_MODEL), const),
                  pl.BlockSpec((D_MODEL, ROUTER_LANES), const), pl.BlockSpec((1, ROUTER_LANES), const),
                  pl.BlockSpec((tm, tm), const), pl.BlockSpec((1, ROUTER_LANES), const)],
        out_specs=[wide, pl.BlockSpec((tm * ROW_TILE, LANES), row), lanes, lanes, lanes,
                   pl.BlockSpec((1, ROUTER_LANES), const)],
        out_shape=outs,
        scratch_shapes=[pltpu.VMEM((1, ROUTER_LANES), F32)],
        compiler_params=_cparams(("arbitrary",)),
        name="post_mix",
    )(x, *pieces, lp['w_out'], mod, mod, mod, lp['norm_ffn'], lp['router_w'], lp['router_b'], lp['tri'], cnt_in)


def _row_copy(src_ref, s, dst_ref, d, sem):
    return pltpu.make_async_copy(src_ref.at[pl.ds(pl.multiple_of(s * ROW_TILE, ROW_TILE), ROW_TILE)],
                                 dst_ref.at[pl.ds(pl.multiple_of(d * ROW_TILE, ROW_TILE), ROW_TILE)], sem)


def _dispatch_body(dest_ref, h_ref, xs_in_ref, xs_ref, sem):
    del xs_in_ref
    tm = h_ref.shape[0] // ROW_TILE
    base = pl.program_id(0) * (2 * tm)

    def issue(t, c):
        _row_copy(h_ref, t, xs_ref, dest_ref[base + 2 * t], sem).start()
        _row_copy(h_ref, t, xs_ref, dest_ref[base + 2 * t + 1], sem).start()
        return c

    lax.fori_loop(0, tm, issue, 0)
    for _ in range(2):
        pltpu.make_async_copy(h_ref, xs_ref.at[pl.ds(0, tm * ROW_TILE)], sem).wait()


def _dispatch(h2, dest_flat, xs, *, tm):
    t = h2.shape[0] // ROW_TILE
    return pl.pallas_call(
        _dispatch_body,
        grid_spec=pltpu.PrefetchScalarGridSpec(
            num_scalar_prefetch=1,
            grid=(t // tm,),
            in_specs=[pl.BlockSpec((tm * ROW_TILE, LANES), lambda i, d: (i, 0)),
                      pl.BlockSpec(memory_space=pl.ANY)],
            out_specs=pl.BlockSpec(memory_space=pl.ANY),
            scratch_shapes=[pltpu.SemaphoreType.DMA(())]),
        out_shape=jax.ShapeDtypeStruct(xs.shape, xs.dtype),
        input_output_aliases={2: 0},
        compiler_params=_cparams(("arbitrary",)),
        name="moe_dispatch",
    )(dest_flat, h2, xs)


def _experts_body(blk_e_ref, n_used_ref, xs_ref, wg_ref, wu_ref, wd_ref, o_ref, wgb_ref, wub_ref, wdb_ref):
    i = pl.program_id(0)
    e = blk_e_ref[i]
    e_prev = blk_e_ref[jnp.maximum(i - 1, 0)]

    @pl.when((i == 0) | (e != e_prev))
    def _():
        wgb_ref[...] = wg_ref[...].astype(MXU_DT)
        wub_ref[...] = wu_ref[...].astype(MXU_DT)
        wdb_ref[...] = wd_ref[...].astype(MXU_DT)

    @pl.when(i < n_used_ref[0])
    def _():
        x_lo, x_hi = [t.astype(MXU_DT) for t in _load_row_tiles(xs_ref)]
        g = (jnp.dot(x_lo, wgb_ref[:HALF_D, :], preferred_element_type=F32)
             + jnp.dot(x_hi, wgb_ref[HALF_D:, :], preferred_element_type=F32))
        u = (jnp.dot(x_lo, wub_ref[:HALF_D, :], preferred_element_type=F32)
             + jnp.dot(x_hi, wub_ref[HALF_D:, :], preferred_element_type=F32))
        hmid = (g * _sigmoid(g)) * u
        _store_row_tiles(o_ref, jnp.dot(hmid.astype(MXU_DT), wdb_ref[...], preferred_element_type=F32))

    @pl.when(i >= n_used_ref[0])
    def _():
        o_ref[...] = jnp.zeros_like(o_ref)


def _experts(xs, blk_e, n_used, wg, wu, wd, l):
    n_blk = xs.shape[0] // (MOE_ROWS * ROW_TILE)
    xmap = lambda i, be, nu: (jnp.minimum(i, nu[0] - 1), 0)
    return pl.pallas_call(
        _experts_body,
        grid_spec=pltpu.PrefetchScalarGridSpec(
            num_scalar_prefetch=2,
            grid=(n_blk,),
            in_specs=[pl.BlockSpec((MOE_ROWS * ROW_TILE, LANES), xmap),
                      pl.BlockSpec((None, None, D_MODEL, D_EXPERT), lambda i, be, nu: (l, be[i], 0, 0)),
                      pl.BlockSpec((None, None, D_MODEL, D_EXPERT), lambda i, be, nu: (l, be[i], 0, 0)),
                      pl.BlockSpec((None, None, D_EXPERT, D_MODEL), lambda i, be, nu: (l, be[i], 0, 0))],
            out_specs=pl.BlockSpec((MOE_ROWS * ROW_TILE, LANES), lambda i, be, nu: (i, 0)),
            scratch_shapes=[pltpu.VMEM((D_MODEL, D_EXPERT), MXU_DT), pltpu.VMEM((D_MODEL, D_EXPERT), MXU_DT),
                            pltpu.VMEM((D_EXPERT, D_MODEL), MXU_DT)]),
        out_shape=jax.ShapeDtypeStruct(xs.shape, ROW_DT),
        compiler_params=_cparams(("arbitrary",)),
        name="moe_experts",
    )(blk_e, n_used, xs, wg, wu, wd)


def _combine_body(dest_ref, ys_ref, x1_ref, gate_ref, g2_ref, x2_ref, buf0_ref, buf1_ref, sem):
    tm = x1_ref.shape[0]
    base = pl.program_id(0) * (2 * tm)

    def issue(t, c):
        _row_copy(ys_ref, dest_ref[base + 2 * t], buf0_ref, t, sem).start()
        _row_copy(ys_ref, dest_ref[base + 2 * t + 1], buf1_ref, t, sem).start()
        return c

    lax.fori_loop(0, tm, issue, 0)
    for buf_ref in (buf0_ref, buf1_ref):
        pltpu.make_async_copy(ys_ref.at[pl.ds(0, tm * ROW_TILE)], buf_ref, sem).wait()
    gate = gate_ref[...]
    lo0, hi0 = _load_row_tiles(buf0_ref)
    lo1, hi1 = _load_row_tiles(buf1_ref)
    g0, g1 = gate[:, 0:1], gate[:, 1:2]
    y = jnp.concatenate([lo0 * g0 + lo1 * g1, hi0 * g0 + hi1 * g1], axis=1)
    x2_ref[...] = x1_ref[...] + g2_ref[...] * y


def _combine(ys, dest_flat, x1, gate, mod, *, tm, rows_per_mod):
    t = x1.shape[0]
    if mod.ndim == 4:
        g2_spec = pl.BlockSpec((None, None, 1, D_MODEL), lambda i, d: (5, (i * tm) // rows_per_mod, 0, 0))
    else:
        g2_spec = pl.BlockSpec((None, tm, D_MODEL), lambda i, d: (5, i, 0))
    return pl.pallas_call(
        _combine_body,
        grid_spec=pltpu.PrefetchScalarGridSpec(
            num_scalar_prefetch=1,
            grid=(t // tm,),
            in_specs=[pl.BlockSpec(memory_space=pl.ANY),
                      pl.BlockSpec((tm, D_MODEL), lambda i, d: (i, 0)),
                      pl.BlockSpec((tm, ROUTER_LANES), lambda i, d: (i, 0)),
                      g2_spec],
            out_specs=pl.BlockSpec((tm, D_MODEL), lambda i, d: (i, 0)),
            scratch_shapes=[pltpu.VMEM((tm * ROW_TILE, LANES), ROW_DT), pltpu.VMEM((tm * ROW_TILE, LANES), ROW_DT),
                            pltpu.SemaphoreType.DMA(())]),
        out_shape=jax.ShapeDtypeStruct((t, D_MODEL), F32),
        compiler_params=_cparams(("arbitrary",)),
        name="moe_combine",
    )(dest_flat, ys, x1, gate, mod)


def _routing_tables(cnt, eids, ranks, n_blk):
    counts = cnt[0, :N_EXPERTS].astype(jnp.int32)
    padded = (counts + MOE_ROWS - 1) // MOE_ROWS * MOE_ROWS
    ends = jnp.cumsum(padded)
    starts = ends - padded
    experts = jnp.arange(N_EXPERTS, dtype=jnp.int32)
    start_of = lambda e: jnp.sum(jnp.where(e[..., None] == experts, starts, 0), axis=-1)
    dests = [(start_of(e[:, :2]) + r[:, :2]).reshape(-1) for e, r in zip(eids, ranks)]
    blk_start = jnp.arange(n_blk, dtype=jnp.int32) * MOE_ROWS
    blk_e = jnp.minimum(jnp.sum((ends[None, :] <= blk_start[:, None]).astype(jnp.int32), axis=1), N_EXPERTS - 1)
    n_used = (ends[-1] // MOE_ROWS).astype(jnp.int32).reshape(1)
    return dests, blk_e, n_used


def _block_diag(w):
    g, r, c = w.shape
    eye = jnp.eye(g, dtype=w.dtype)
    return (eye[:, None, :, None] * w[:, :, None, :]).reshape(g * r, g * c)


def _s5_discretise(log_dt, a_re, a_im, b_re, b_im):
    step = jnp.exp(log_dt)[:, None]
    mag = jnp.exp(a_re * step)
    ang = a_im * step
    abr = mag * jnp.cos(ang)
    abi = mag * jnp.sin(ang)
    den = a_re * a_re + a_im * a_im
    zr = ((abr - 1.0) * a_re + abi * a_im) / den
    zi = (abi * a_re - (abr - 1.0) * a_im) / den
    bbr = zr[..., None] * b_re - zi[..., None] * b_im
    bbi = zr[..., None] * b_im + zi[..., None] * b_re
    return abr, abi, bbr, bbi


def _prep_layer(P, l):
    row = lambda a: a.reshape(1, -1)
    abr, abi, bbr, bbi = _s5_discretise(P['s5_log_dt'][l], P['s5_a_re'][l], P['s5_a_im'][l],
                                        P['s5_b_re'][l], P['s5_b_im'][l])
    on = P['out_norm'][l]
    bones = _block_diag(jnp.full((ATT_HEADS, HEAD_DIM, HEAD_DIM), 1.0 / HEAD_DIM, F32)).astype(jnp.bfloat16)
    return {
        'norm_mix': row(P['norm_mix'][l]), 'norm_ffn': row(P['norm_ffn'][l]),
        'w_in': P['w_in'][l].astype(MXU_DT), 'w_out': P['w_out'][l].astype(MXU_DT),
        'bones': bones,
        'qg': row(jnp.tile(P['q_norm'][l], ATT_HEADS)), 'kg': row(jnp.tile(P['k_norm'][l], ATT_HEADS)),
        'lru_conv_w': P['lru_conv_w'][l], 'lru_conv_b': row(P['lru_conv_b'][l]),
        'lru_wa_blk': _block_diag(P['lru_wa'][l]).astype(MXU_DT), 'lru_ba': row(P['lru_ba'][l]),
        'lru_wx_blk': _block_diag(P['lru_wx'][l]).astype(MXU_DT), 'lru_bx': row(P['lru_bx'][l]),
        'lru_lambda': row(P['lru_lambda'][l]), 'sconv_w': P['sconv_w'][l],
        'on_a': row(on[0:W_GRP]), 'on_b': row(on[W_GRP:2 * W_GRP]),
        'on_c': row(on[2 * W_GRP:3 * W_GRP]), 'on_d': row(on[3 * W_GRP:]),
        's5_bb': jnp.concatenate([_block_diag(bbr.transpose(0, 2, 1)), _block_diag(bbi.transpose(0, 2, 1))],
                                 axis=1).astype(MXU_DT),
        's5_cre': _block_diag(P['s5_c_re'][l].transpose(0, 2, 1)).astype(MXU_DT),
        's5_cim': _block_diag(P['s5_c_im'][l].transpose(0, 2, 1)).astype(MXU_DT),
        's5_abr': row(abr), 's5_abi': row(abi), 's5_d': row(P['s5_d'][l]),
        's5_glu_w': P['s5_glu_w'][l].astype(MXU_DT), 's5_glu_b': row(P['s5_glu_b'][l]),
        'router_w': jnp.zeros((D_MODEL, ROUTER_LANES), F32)
                       .at[:, :N_GROUPS].set(P['router_g_w'][l])
                       .at[:, N_GROUPS:N_GROUPS + N_EXPERTS].set(P['router_e_w'][l]).astype(MXU_DT),
        'router_b': jnp.zeros((1, ROUTER_LANES), F32)
                       .at[0, :N_GROUPS].set(P['router_g_b'][l])
                       .at[0, N_GROUPS:N_GROUPS + N_EXPERTS].set(P['router_e_b'][l]),
    }


TOKEN_TILE = 512
SEQ_TILE = 256

_PARAM_NAMES = ('rel_bias', 'mod_w', 'mod_b', 'norm_mix', 'norm_ffn', 'w_in', 'lru_conv_w', 'lru_conv_b',
                'lru_wa', 'lru_ba', 'lru_wx', 'lru_bx', 'lru_lambda', 'sconv_w', 's5_log_dt', 's5_a_re',
                's5_a_im', 's5_b_re', 's5_b_im', 's5_c_re', 's5_c_im', 's5_d', 's5_glu_w', 's5_glu_b',
                'q_norm', 'k_norm', 'out_norm', 'w_out', 'router_g_w', 'router_g_b', 'router_e_w',
                'router_e_b', 'moe_w_gate', 'moe_w_up', 'moe_w_down')


def _mixers(x, mod, lp, st, attn_fn, *, batch, seq, tm, tl, rows_per_mod, dils=()):
    res = _in_proj(x, mod, lp['norm_mix'], lp['w_in'], lp['bones'], lp['qg'], lp['kg'],
                   tm=tm, rows_per_mod=rows_per_mod, dils=dils, seq=seq)
    zm, q, kf, vf, kb, vb = res[:6]
    zm3 = zm.reshape(batch, seq, 6 * W_GRP)
    oa, ob, lru_h, lru_conv, sconv = _mixer_ab(zm3, lp, st['lru_h'], st['lru_conv'], st['sconv'], tl=tl)
    oc, s5_re, s5_im = _s5_mixer(zm3, lp, st['s5_re'], st['s5_im'], tl=tl)
    r3 = lambda t: t.reshape(batch, seq, W_GRP)
    od = attn_fn(r3(q), r3(kf), r3(vf), r3(kb), r3(vb), res[6:])
    flat = lambda t: t.reshape(batch * seq, W_GRP)
    new_st = {'lru_h': lru_h[:, 0], 'lru_conv': lru_conv, 'sconv': sconv,
              's5_re': s5_re.reshape(batch, S5_GROUPS, S5_STATE), 's5_im': s5_im.reshape(batch, S5_GROUPS, S5_STATE),
              'win_k': kf.reshape(batch, seq, ATT_HEADS, HEAD_DIM), 'win_v': vf.reshape(batch, seq, ATT_HEADS, HEAD_DIM)}
    return [flat(oa), flat(ob), flat(oc), flat(od)], new_st


def kernel(x_prompt, x_sample, c_prompt, c_sample, state_lru_h, state_lru_conv, state_sconv, state_s5_re, state_s5_im, cache_win_k, cache_win_v, rel_bias, mod_w, mod_b, norm_mix, norm_ffn, w_in, lru_conv_w, lru_conv_b, lru_wa, lru_ba, lru_wx, lru_bx, lru_lambda, sconv_w, s5_log_dt, s5_a_re, s5_a_im, s5_b_re, s5_b_im, s5_c_re, s5_c_im, s5_d, s5_glu_w, s5_glu_b, q_norm, k_norm, out_norm, w_out, router_g_w, router_g_b, router_e_w, router_e_b, moe_w_gate, moe_w_up, moe_w_down):
    P = dict(zip(_PARAM_NAMES, (rel_bias, mod_w, mod_b, norm_mix, norm_ffn, w_in, lru_conv_w, lru_conv_b,
                                lru_wa, lru_ba, lru_wx, lru_bx, lru_lambda, sconv_w, s5_log_dt, s5_a_re,
                                s5_a_im, s5_b_re, s5_b_im, s5_c_re, s5_c_im, s5_d, s5_glu_w, s5_glu_b,
                                q_norm, k_norm, out_norm, w_out, router_g_w, router_g_b, router_e_w,
                                router_e_b, moe_w_gate, moe_w_up, moe_w_down)))
    bp, seq, d = x_prompt.shape
    bs, dec, _ = x_sample.shape
    depth = mod_w.shape[0]
    tp, ts = bp * seq, bs * dec
    wb = cache_win_k.shape[2]
    wp = min(PATTERNS[-1][0], seq)
    tm_p = min(TOKEN_TILE, tp)
    tm_s = min(TOKEN_TILE, ts)
    tl_p = min(SEQ_TILE, seq)

    nc = -(-(bp + bs) // 8) * 8
    c_all = jnp.zeros((nc, d), F32).at[:bp].set(c_prompt).at[bp:bp + bs].set(c_sample)
    mod_all = _modulation(c_all, mod_w, mod_b)
    bias_p = _bias_table(rel_bias, _prompt_bias_codes()).reshape(len(PATTERNS), 2, ATT_HEADS * Q_BLK, 2 * Q_BLK)
    dils_p = tuple(dil for _, dil in PATTERNS if dil > 1)
    (lw_codes, lw_logm), (lwn_codes, lwn_logm) = _sample_bias_codes(dec, wb)
    lw_s = (_bias_table(rel_bias, lw_codes) + lw_logm).reshape(ATT_HEADS, dec, wb)
    lwn_s = (_bias_table(rel_bias, lwn_codes) + lwn_logm).reshape(ATT_HEADS, dec, dec)
    tri = jnp.asarray(np.tril(np.ones((TOKEN_TILE, TOKEN_TILE), np.float32), -1), jnp.bfloat16)
    n_blk = (2 * (tp + ts)) // MOE_ROWS + N_EXPERTS

    zero_st = {'lru_h': jnp.zeros((bp, 1, W_GRP), F32), 'lru_conv': jnp.zeros((bp, LRU_CONV - 1, W_GRP), F32),
               'sconv': jnp.zeros((bp, SCONV_W - 1, W_GRP), F32),
               's5_re': jnp.zeros((bp, 1, S5_N), F32), 's5_im': jnp.zeros((bp, 1, S5_N), F32)}
    names = ('lru_h', 'lru_conv', 'sconv', 's5_re', 's5_im', 'win_k', 'win_v')
    acc_p = {n: [] for n in names}
    acc_s = {n: [] for n in names}
    xp = x_prompt.reshape(tp, d)
    xs = x_sample.reshape(ts, d)
    for l in range(depth):
        lp = _prep_layer(P, l)
        lp['tri'] = tri
        m6 = mod_all[l].reshape(nc, 6, d).transpose(1, 0, 2)
        mod_p = m6[:, :bp].reshape(6, bp, 1, d)
        mod_s = jnp.repeat(m6[:, bp:bp + bs], dec, axis=1)
        attn_p = lambda q, kf, vf, kb, vb, ex: _attn_prompt(
            [(q[:, None], kb[:, None], vb[:, None])] + [tuple(ex[3 * t:3 * t + 3]) for t in range(len(dils_p))],
            bias_p, lp['on_d'], tm=tm_p)
        pieces_p, st_p = _mixers(xp, mod_p, lp, zero_st, attn_p, batch=bp, seq=seq, tm=tm_p, tl=tl_p,
                                 rows_per_mod=seq, dils=dils_p)
        st_p['win_k'] = st_p['win_k'][:, seq - wp:]
        st_p['win_v'] = st_p['win_v'][:, seq - wp:]
        cnt0 = jnp.zeros((1, ROUTER_LANES), F32)
        x1p, h2p, eid_p, gate_p, rank_p, cnt = _post_mix(xp, pieces_p, mod_p, lp, cnt0, tm=tm_p, rows_per_mod=seq)
        samp_st = {'lru_h': state_lru_h[l][:, None], 'lru_conv': state_lru_conv[l], 'sconv': state_sconv[l],
                   's5_re': state_s5_re[l].reshape(bs, 1, S5_N), 's5_im': state_s5_im[l].reshape(bs, 1, S5_N)}
        attn_s = lambda q, kf, vf, kb, vb, ex: _attn_sample(
            q, kf, vf, cache_win_k, cache_win_v, l, lw_s, lwn_s, lp['on_d'])
        pieces_s, st_s = _mixers(xs, mod_s, lp, samp_st, attn_s, batch=bs, seq=dec, tm=tm_s, tl=dec,
                                 rows_per_mod=dec)
        x1s, h2s, eid_s, gate_s, rank_s, cnt = _post_mix(xs, pieces_s, mod_s, lp, cnt, tm=tm_s, rows_per_mod=dec)
        dests, blk_e, n_used = _routing_tables(cnt, [eid_p, eid_s], [rank_p, rank_s], n_blk)
        slots = jnp.zeros((n_blk * MOE_ROWS * ROW_TILE, LANES), ROW_DT)
        slots = _dispatch(h2p, dests[0], slots, tm=tm_p)
        slots = _dispatch(h2s, dests[1], slots, tm=tm_s)
        ys = _experts(slots, blk_e, n_used, moe_w_gate, moe_w_up, moe_w_down, l)
        xp = _combine(ys, dests[0], x1p, gate_p, mod_p, tm=tm_p, rows_per_mod=seq)
        xs = _combine(ys, dests[1], x1s, gate_s, mod_s, tm=tm_s, rows_per_mod=dec)
        for n in names:
            acc_p[n].append(st_p[n])
            acc_s[n].append(st_s[n])
    new_p = {n: jnp.stack(acc_p[n], axis=0) for n in names}
    new_s = {n: jnp.stack(acc_s[n], axis=0) for n in names}
    return (xp.reshape(bp, seq, d), xs.reshape(bs, dec, d),
            new_p['lru_h'], new_p['lru_conv'], new_p['sconv'], new_p['s5_re'], new_p['s5_im'],
            new_p['win_k'], new_p['win_v'],
            new_s['lru_h'], new_s['lru_conv'], new_s['sconv'], new_s['s5_re'], new_s['s5_im'],
            new_s['win_k'], new_s['win_v'])
```

```python
import functools
import math

import numpy as np
import jax
import jax.numpy as jnp
from jax import lax
from jax.experimental import pallas as pl
from jax.experimental.pallas import tpu as pltpu

F32 = jnp.float32
MXU_DT = jnp.bfloat16
HIGHEST = lax.Precision.HIGHEST

D_MODEL = 1024
DEPTH = 4
W_GRP = 256
N_Z = 9
LRU_HEADS = 4
LRU_CONV = 4
LRU_C = 8.0
SCONV_W = 3
S5_CH = 16
S5_GROUPS = 16
S5_STATE = 64
S5_N = S5_GROUPS * S5_STATE
ATT_HEADS = 4
HEAD_DIM = 64
PATTERNS = ((128, 1), (512, 4), (2048, 16))
Q_BLK = 128
REL_BUCKETS = 32
REL_MAX_DIST = 2048
N_GROUPS = 4
EXP_PER_GROUP = 8
N_EXPERTS = 32
D_EXPERT = 512
EPS = 1e-6
NEG = -1e30

VMEM_LIMIT = 56 * 1024 * 1024
LANES = 128
MOE_ROWS = 512
SEGS = 8
ATTN_SUB_BLOCKS = 4


def _cparams(sem):
    return pltpu.CompilerParams(dimension_semantics=sem, vmem_limit_bytes=VMEM_LIMIT)


def _gelu(x):
    return 0.5 * x * (1.0 + jnp.tanh(math.sqrt(2.0 / math.pi) * (x + 0.044715 * (x * x * x))))


def _sigmoid(x):
    return 1.0 / (1.0 + jnp.exp(-x))


def _rms_rows(x):
    return x * lax.rsqrt(jnp.mean(x * x, axis=-1, keepdims=True) + EPS)


def _shift_rows(x, s, fill, row):
    return jnp.where(row >= s, pltpu.roll(x, s, 0), fill)


def _mod_body(c_ref, w_ref, b_ref, o_ref):
    c = c_ref[...]
    s = c * _sigmoid(c)
    o_ref[...] = jnp.dot(s.astype(MXU_DT), w_ref[...].astype(MXU_DT), preferred_element_type=F32) + b_ref[...]


def _modulation(c_all, mod_w, mod_b):
    nb = c_all.shape[0]
    depth = mod_w.shape[0]
    n_out = mod_w.shape[2]
    tn = D_MODEL
    return pl.pallas_call(
        _mod_body,
        grid=(depth, n_out // tn),
        in_specs=[pl.BlockSpec((nb, D_MODEL), lambda l, j: (0, 0)),
                  pl.BlockSpec((None, D_MODEL, tn), lambda l, j: (l, 0, j)),
                  pl.BlockSpec((None, 1, tn), lambda l, j: (l, 0, j))],
        out_specs=pl.BlockSpec((None, nb, tn), lambda l, j: (l, 0, j)),
        out_shape=jax.ShapeDtypeStruct((depth, nb, n_out), F32),
        compiler_params=_cparams(("arbitrary", "arbitrary")),
        name="modulation",
    )(c_all, mod_w, mod_b.reshape(depth, 1, n_out))


def _head_mean_sq(t, bones):
    sq = t * t
    hi = sq.astype(jnp.bfloat16)
    lo = (sq - hi.astype(F32)).astype(jnp.bfloat16)
    return (jnp.dot(hi, bones, preferred_element_type=F32)
            + jnp.dot(lo, bones, preferred_element_type=F32))


def _inproj_body(*refs, dils, kv_pos_minor):
    (x_ref, sc_ref, sh_ref, g_ref, w_ref, bones_ref, qg_ref, kg_ref,
     zm_ref, q_ref, kf_ref, vf_ref, kb_ref, vb_ref) = refs[:14]
    x = x_ref[...]
    h = _rms_rows(x) * g_ref[...]
    h = h * (1.0 + sc_ref[...]) + sh_ref[...]
    z = jnp.dot(h.astype(MXU_DT), w_ref[...], preferred_element_type=F32)
    nm = 6 * W_GRP
    zm_ref[...] = z[:, :nm]
    q = z[:, nm:nm + W_GRP]
    k = z[:, nm + W_GRP:nm + 2 * W_GRP]
    v = z[:, nm + 2 * W_GRP:]
    bones = bones_ref[...]
    qn = (q * lax.rsqrt(_head_mean_sq(q, bones) + EPS) * qg_ref[...]) * (HEAD_DIM ** -0.5)
    kn = k * lax.rsqrt(_head_mean_sq(k, bones) + EPS) * kg_ref[...]
    q_ref[...] = qn.astype(q_ref.dtype)
    if kv_pos_minor:
        kf_ref[...] = kn.T
        vf_ref[...] = v.T
    else:
        kf_ref[...] = kn
        vf_ref[...] = v
    kb_ref[...] = kn.astype(kb_ref.dtype)
    vb_ref[...] = v.astype(vb_ref.dtype)
    if dils:
        stage_ref = refs[-1]
        tm = x_ref.shape[0]
        for a, val in enumerate((qn, kn, v)):
            for half in range(W_GRP // LANES):
                stage_ref[a, half] = val[:, half * LANES:(half + 1) * LANES]
        for di, dil in enumerate(dils):
            for a in range(3):
                out_ref = refs[14 + 3 * di + a]
                for r in range(dil):
                    for half in range(W_GRP // LANES):
                        out_ref[r, :, half * LANES:(half + 1) * LANES] = (
                            stage_ref.at[a, half][pl.ds(r, tm // dil, stride=dil), :].astype(out_ref.dtype))


def _mod_spec(mod, k, tm, rows_per_mod):
    if mod.ndim == 4:
        return pl.BlockSpec((None, None, 1, D_MODEL), lambda i: (k, (i * tm) // rows_per_mod, 0, 0))
    return pl.BlockSpec((None, tm, D_MODEL), lambda i: (k, i, 0))


def _in_proj(x, mod, norm_g, w_in, bones, qg, kg, *, tm, rows_per_mod, dils=(), seq=None, kv_window=None):
    t = x.shape[0]
    n_in = w_in.shape[1]
    nm = 6 * W_GRP
    row = lambda i: (i, 0)
    const = lambda i: (0, 0)
    outs = [jax.ShapeDtypeStruct((t, nm), F32),
            jax.ShapeDtypeStruct((t, W_GRP), MXU_DT),
            jax.ShapeDtypeStruct((t, W_GRP), F32),
            jax.ShapeDtypeStruct((t, W_GRP), F32),
            jax.ShapeDtypeStruct((t, W_GRP), MXU_DT),
            jax.ShapeDtypeStruct((t, W_GRP), MXU_DT)]
    out_specs = [pl.BlockSpec((tm, nm), row)] + [pl.BlockSpec((tm, W_GRP), row)] * 5
    scratch = []
    if kv_window is not None:
        tiles_per_seq = seq // tm
        skip = (seq - kv_window) // tm
        assert (seq - kv_window) % tm == 0 and kv_window % tm == 0
        win = pl.BlockSpec((None, W_GRP, tm),
                           lambda i: (i // tiles_per_seq, 0, jnp.maximum(i % tiles_per_seq - skip, 0)))
        outs[2] = outs[3] = jax.ShapeDtypeStruct((t // seq, W_GRP, kv_window), F32)
        out_specs[2] = out_specs[3] = win
    if dils:
        tiles_per_seq = seq // tm
        for dil in dils:
            outs += [jax.ShapeDtypeStruct((t // seq, dil, seq // dil, W_GRP), MXU_DT)] * 3
            out_specs += [pl.BlockSpec((None, dil, tm // dil, W_GRP),
                                       lambda i: (i // tiles_per_seq, 0, i % tiles_per_seq, 0))] * 3
        scratch = [pltpu.VMEM((3, W_GRP // LANES, tm, LANES), F32)]
    return pl.pallas_call(
        functools.partial(_inproj_body, dils=tuple(dils), kv_pos_minor=kv_window is not None),
        grid=(t // tm,),
        in_specs=[pl.BlockSpec((tm, D_MODEL), row),
                  _mod_spec(mod, 1, tm, rows_per_mod),
                  _mod_spec(mod, 0, tm, rows_per_mod),
                  pl.BlockSpec((1, D_MODEL), const),
                  pl.BlockSpec((D_MODEL, n_in), const),
                  pl.BlockSpec((W_GRP, W_GRP), const),
                  pl.BlockSpec((1, W_GRP), const),
                  pl.BlockSpec((1, W_GRP), const)],
        out_specs=out_specs,
        out_shape=outs,
        scratch_shapes=scratch,
        compiler_params=_cparams(("arbitrary",)),
        name="in_proj",
    )(x, mod, mod, norm_g, w_in, bones, qg, kg)


def _softplus(x):
    return jnp.maximum(x, 0.0) + jnp.log(1.0 + jnp.exp(-jnp.abs(x)))


def _mixer_ab_body(xa_ref, ga_ref, gb_ref, gc_ref, xb_ref,
                   cw_ref, cb_ref, wa_ref, ba_ref, wx_ref, bx_ref, lam_ref, sw_ref, on_a_ref, on_b_ref,
                   h0_ref, conv0_ref, sconv0_ref,
                   oa_ref, ob_ref, hn_ref, convn_ref, sconvn_ref,
                   xe_ref, pe_ref, hc_ref):
    i = pl.program_id(1)
    tl = xa_ref.shape[0]

    @pl.when(i == 0)
    def _():
        xe_ref[8 - (LRU_CONV - 1):8, :] = conv0_ref[...]
        pe_ref[8 - (SCONV_W - 1):8, :] = sconv0_ref[...]
        hc_ref[...] = h0_ref[...]

    row = lax.broadcasted_iota(jnp.int32, (tl, 1), 0)
    xa = xa_ref[...]
    xe_ref[8:, :] = xa
    cw = cw_ref[...]
    xc = cw[LRU_CONV - 1:LRU_CONV, :] * xa
    for s in range(1, LRU_CONV):
        xc = xc + cw[LRU_CONV - 1 - s:LRU_CONV - s, :] * xe_ref[8 - s:8 - s + tl, :]
    xc = xc + cb_ref[...]
    convn_ref[...] = xa[tl - (LRU_CONV - 1):, :]
    xe_ref[0:8, :] = xa[tl - 8:, :]
    xcb = xc.astype(MXU_DT)
    r = _sigmoid(jnp.dot(xcb, wa_ref[...], preferred_element_type=F32) + ba_ref[...])
    ig = _sigmoid(jnp.dot(xcb, wx_ref[...], preferred_element_type=F32) + bx_ref[...])
    log_a = (-LRU_C * r) * _softplus(-lam_ref[...])
    a = jnp.exp(log_a)
    b = jnp.sqrt(-jnp.tanh(log_a) * (a * a + 1.0)) * (ig * xc)
    s = 1
    while s < tl:
        b = a * _shift_rows(b, s, 0.0, row) + b
        a = a * _shift_rows(a, s, 1.0, row)
        s *= 2
    h = b + a * hc_ref[...]
    hc_ref[...] = h[tl - 1:, :]
    hn_ref[...] = h[tl - 1:, :]
    out_a = h * _gelu(ga_ref[...])
    oa_ref[...] = (_rms_rows(out_a) * on_a_ref[...]).astype(oa_ref.dtype)
    p = gc_ref[...] * xb_ref[...]
    pe_ref[8:, :] = p
    sw = sw_ref[...]
    yb = sw[SCONV_W - 1:SCONV_W, :] * p
    for s in range(1, SCONV_W):
        yb = yb + sw[SCONV_W - 1 - s:SCONV_W - s, :] * pe_ref[8 - s:8 - s + tl, :]
    sconvn_ref[...] = p[tl - (SCONV_W - 1):, :]
    pe_ref[0:8, :] = p[tl - 8:, :]
    out_b = gb_ref[...] * yb
    ob_ref[...] = (_rms_rows(out_b) * on_b_ref[...]).astype(ob_ref.dtype)


def _mixer_ab(zm, lp, h0, conv0, sconv0, *, tl):
    bsz, seq, _ = zm.shape
    col = lambda c: pl.BlockSpec((None, tl, W_GRP), lambda b, i, c=c: (b, i, c))
    const = lambda shp: pl.BlockSpec(shp, lambda b, i: (0,) * len(shp))
    per_b = lambda n: pl.BlockSpec((None, n, W_GRP), lambda b, i: (b, 0, 0))
    outs = [jax.ShapeDtypeStruct((bsz, seq, W_GRP), MXU_DT),
            jax.ShapeDtypeStruct((bsz, seq, W_GRP), MXU_DT),
            jax.ShapeDtypeStruct((bsz, 1, W_GRP), F32),
            jax.ShapeDtypeStruct((bsz, LRU_CONV - 1, W_GRP), F32),
            jax.ShapeDtypeStruct((bsz, SCONV_W - 1, W_GRP), F32)]
    return pl.pallas_call(
        _mixer_ab_body,
        grid=(bsz, seq // tl),
        in_specs=[col(0), col(1), col(2), col(3), col(4),
                  const((LRU_CONV, W_GRP)), const((1, W_GRP)),
                  const((W_GRP, W_GRP)), const((1, W_GRP)),
                  const((W_GRP, W_GRP)), const((1, W_GRP)),
                  const((1, W_GRP)), const((SCONV_W, W_GRP)),
                  const((1, W_GRP)), const((1, W_GRP)),
                  per_b(1), per_b(LRU_CONV - 1), per_b(SCONV_W - 1)],
        out_specs=[pl.BlockSpec((None, tl, W_GRP), lambda b, i: (b, i, 0)),
                   pl.BlockSpec((None, tl, W_GRP), lambda b, i: (b, i, 0)),
                   per_b(1), per_b(LRU_CONV - 1), per_b(SCONV_W - 1)],
        out_shape=outs,
        scratch_shapes=[pltpu.VMEM((tl + 8, W_GRP), F32),
                        pltpu.VMEM((tl + 8, W_GRP), F32),
                        pltpu.VMEM((1, W_GRP), F32)],
        compiler_params=_cparams(("arbitrary", "arbitrary")),
        name="mixer_ab",
    )(zm, zm, zm, zm, zm,
      lp['lru_conv_w'], lp['lru_conv_b'], lp['lru_wa_blk'], lp['lru_ba'], lp['lru_wx_blk'], lp['lru_bx'],
      lp['lru_lambda'], lp['sconv_w'], lp['on_a'], lp['on_b'],
      h0, conv0, sconv0)


def _s5_body(u_ref, bb_ref, cre_ref, cim_ref, ar_ref, ai_ref, d_ref, gw_ref, gb_ref, on_ref,
             h0r_ref, h0i_ref,
             o_ref, hnr_ref, hni_ref,
             tr_ref, ti_ref, pr_ref, pi_ref, hr_ref, hi_ref, lr_ref, li_ref, stage_ref):
    b = pl.program_id(0)
    i = pl.program_id(1)
    tl = u_ref.shape[0]
    steps = tl // SEGS
    groups = S5_N // LANES
    halves = W_GRP // LANES
    ar = ar_ref[...]
    ai = ai_ref[...]

    def powers(base_r, base_i, n, first):
        row = lax.broadcasted_iota(jnp.int32, (n, 1), 0)
        tr = jnp.broadcast_to(base_r, (n, S5_N))
        ti = jnp.broadcast_to(base_i, (n, S5_N))
        s = first
        while s < n:
            sr = _shift_rows(tr, s, 1.0, row)
            si = _shift_rows(ti, s, 0.0, row)
            tr, ti = tr * sr - ti * si, tr * si + ti * sr
            s *= 2
        return tr, ti

    @pl.when((b == 0) & (i == 0))
    def _():
        tr, ti = powers(ar, ai, tl, SEGS)
        tr_ref[...] = tr
        ti_ref[...] = ti
        pr, pi = powers(tr[tl - 1:, :], ti[tl - 1:, :], SEGS, 1)
        pr_ref[...] = pr
        pi_ref[...] = pi

    @pl.when(i == 0)
    def _():
        hr_ref[...] = h0r_ref[...]
        hi_ref[...] = h0i_ref[...]

    u = u_ref[...]
    if steps > 1:
        for half in range(halves):
            stage_ref[half] = u[:, half * LANES:(half + 1) * LANES]
        u = jnp.concatenate(
            [jnp.concatenate([stage_ref.at[half][pl.ds(j, SEGS, stride=steps), :] for half in range(halves)], axis=1)
             for j in range(steps)], axis=0)
    bu = jnp.dot(u.astype(MXU_DT), bb_ref[...], preferred_element_type=F32)
    a_r = [jnp.broadcast_to(ar[:, c * LANES:(c + 1) * LANES], (SEGS, LANES)) for c in range(groups)]
    a_i = [jnp.broadcast_to(ai[:, c * LANES:(c + 1) * LANES], (SEGS, LANES)) for c in range(groups)]
    loc_r = [jnp.zeros((SEGS, LANES), F32) for _ in range(groups)]
    loc_i = [jnp.zeros((SEGS, LANES), F32) for _ in range(groups)]
    for j in range(steps):
        rows = slice(j * SEGS, (j + 1) * SEGS)
        for c in range(groups):
            cols = slice(c * LANES, (c + 1) * LANES)
            nr = (a_r[c] * loc_r[c] - a_i[c] * loc_i[c]) + bu[rows, c * LANES:(c + 1) * LANES]
            ni = (a_r[c] * loc_i[c] + a_i[c] * loc_r[c]) + bu[rows, S5_N + c * LANES:S5_N + (c + 1) * LANES]
            loc_r[c], loc_i[c] = nr, ni
            lr_ref[rows, cols] = nr
            li_ref[rows, cols] = ni
    er = jnp.concatenate(loc_r, axis=1)
    ei = jnp.concatenate(loc_i, axis=1)
    seg = lax.broadcasted_iota(jnp.int32, (SEGS, 1), 0)
    pr = pr_ref[...]
    pi = pi_ref[...]
    mr, mi = pr[0:1, :], pi[0:1, :]
    s = 1
    while s < SEGS:
        sr = _shift_rows(er, s, 0.0, seg)
        si = _shift_rows(ei, s, 0.0, seg)
        er, ei = er + (mr * sr - mi * si), ei + (mr * si + mi * sr)
        mr, mi = mr * mr - mi * mi, 2.0 * (mr * mi)
        s *= 2
    cr = hr_ref[...]
    ci = hi_ref[...]
    er, ei = er + (pr * cr - pi * ci), ei + (pr * ci + pi * cr)
    in_r = jnp.where(seg >= 1, pltpu.roll(er, 1, 0), cr)
    in_i = jnp.where(seg >= 1, pltpu.roll(ei, 1, 0), ci)
    hr_ref[...] = er[SEGS - 1:, :]
    hi_ref[...] = ei[SEGS - 1:, :]
    hnr_ref[...] = er[SEGS - 1:, :]
    hni_ref[...] = ei[SEGS - 1:, :]
    tr = tr_ref[...]
    ti = ti_ref[...]
    sr = jnp.tile(in_r, (steps, 1))
    si = jnp.tile(in_i, (steps, 1))
    hr = lr_ref[...] + (tr * sr - ti * si)
    hi = li_ref[...] + (tr * si + ti * sr)
    y = (jnp.dot(hr.astype(MXU_DT), cre_ref[...], preferred_element_type=F32)
         - jnp.dot(hi.astype(MXU_DT), cim_ref[...], preferred_element_type=F32)) + d_ref[...] * u
    g = jnp.dot(_gelu(y).astype(MXU_DT), gw_ref[...], preferred_element_type=F32) + gb_ref[...]
    out = g[:, :W_GRP] * _sigmoid(g[:, W_GRP:])
    out = _rms_rows(out) * on_ref[...]
    if steps > 1:
        for half in range(halves):
            stage_ref[half] = out[:, half * LANES:(half + 1) * LANES]
        out = jnp.concatenate(
            [jnp.concatenate([stage_ref.at[half][pl.ds(sg, steps, stride=SEGS), :] for half in range(halves)], axis=1)
             for sg in range(SEGS)], axis=0)
    o_ref[...] = out.astype(o_ref.dtype)


def _s5_mixer(zm, lp, h0r, h0i, *, tl):
    bsz, seq, _ = zm.shape
    const = lambda shp: pl.BlockSpec(shp, lambda b, i: (0,) * len(shp))
    per_b = pl.BlockSpec((None, 1, S5_N), lambda b, i: (b, 0, 0))
    outs = [jax.ShapeDtypeStruct((bsz, seq, W_GRP), MXU_DT),
            jax.ShapeDtypeStruct((bsz, 1, S5_N), F32),
            jax.ShapeDtypeStruct((bsz, 1, S5_N), F32)]
    return pl.pallas_call(
        _s5_body,
        grid=(bsz, seq // tl),
        in_specs=[pl.BlockSpec((None, tl, W_GRP), lambda b, i: (b, i, 5)),
                  const((W_GRP, 2 * S5_N)), const((S5_N, W_GRP)), const((S5_N, W_GRP)),
                  const((1, S5_N)), const((1, S5_N)), const((1, W_GRP)),
                  const((W_GRP, 2 * W_GRP)), const((1, 2 * W_GRP)), const((1, W_GRP)),
                  per_b, per_b],
        out_specs=[pl.BlockSpec((None, tl, W_GRP), lambda b, i: (b, i, 0)), per_b, per_b],
        out_shape=outs,
        scratch_shapes=[pltpu.VMEM((tl, S5_N), F32), pltpu.VMEM((tl, S5_N), F32),
                        pltpu.VMEM((SEGS, S5_N), F32), pltpu.VMEM((SEGS, S5_N), F32),
                        pltpu.VMEM((1, S5_N), F32), pltpu.VMEM((1, S5_N), F32),
                        pltpu.VMEM((tl, S5_N), F32), pltpu.VMEM((tl, S5_N), F32),
                        pltpu.VMEM((W_GRP // LANES, tl, LANES), F32)],
        compiler_params=_cparams(("arbitrary", "arbitrary")),
        name="s5_mixer",
    )(zm, lp['s5_bb'], lp['s5_cre'], lp['s5_cim'], lp['s5_abr'], lp['s5_abi'], lp['s5_d'],
      lp['s5_glu_w'], lp['s5_glu_b'], lp['on_c'], h0r, h0i)


CODE_MASKED = -1
CODE_ZERO = -2


def _bias_body(rb_ref, code_ref, o_ref):
    code = code_ref[...]
    acc = jnp.where(code == CODE_MASKED, NEG, 0.0).astype(F32)
    for c in range(REL_BUCKETS * ATT_HEADS):
        acc = jnp.where(code == c, rb_ref[c], acc)
    o_ref[...] = acc


def _bias_table(rel_bias, codes):
    rows, cols = codes.shape
    tr = max(t for t in range(8, 513, 8) if rows % t == 0)
    return pl.pallas_call(
        _bias_body,
        grid_spec=pltpu.PrefetchScalarGridSpec(
            num_scalar_prefetch=1,
            grid=(rows // tr,),
            in_specs=[pl.BlockSpec((tr, cols), lambda i, rb: (i, 0))],
            out_specs=pl.BlockSpec((tr, cols), lambda i, rb: (i, 0))),
        out_shape=jax.ShapeDtypeStruct((rows, cols), F32),
        compiler_params=_cparams(("arbitrary",)),
        name="bias_table",
    )(rel_bias.reshape(-1), jnp.asarray(codes))


def _t5_bucket(n):
    n = np.asarray(n).astype(np.int32)
    max_exact = REL_BUCKETS // 2
    nf = np.maximum(n, 1).astype(np.float32)
    large = max_exact + (np.log(nf / max_exact) / np.log(REL_MAX_DIST / max_exact)
                         * (REL_BUCKETS - max_exact)).astype(np.int32)
    large = np.minimum(large, REL_BUCKETS - 1)
    return np.where(n < max_exact, n, large).astype(np.int32)


def _prompt_bias_codes():
    i = np.arange(Q_BLK)[:, None]
    j = np.arange(2 * Q_BLK)[None, :]
    rel = Q_BLK + i - j
    out = np.zeros((len(PATTERNS), 2, ATT_HEADS, Q_BLK, 2 * Q_BLK), np.int32)
    for p, (win, dil) in enumerate(PATTERNS):
        span = win // dil
        valid = (rel >= 0) & (rel <= span)
        bucket = _t5_bucket(np.clip(rel, 0, None) * dil)
        for var in range(2):
            v = valid & ((j >= Q_BLK) | (var == 1))
            for h in range(ATT_HEADS):
                out[p, var, h] = np.where(v, bucket * ATT_HEADS + h, CODE_MASKED)
    return out.reshape(-1, 2 * Q_BLK)


def _sample_bias_codes(dec_seq, wb):
    def tables(n_keys, first_pos):
        codes = np.full((ATT_HEADS, dec_seq, n_keys), CODE_MASKED, np.int32)
        logm = np.zeros((ATT_HEADS, dec_seq, n_keys), np.float32)
        for s in range(dec_seq):
            dist = (wb + s) - (first_pos + np.arange(n_keys))
            mult = np.zeros(n_keys, np.int32)
            for win, dil in PATTERNS:
                mult += ((dist >= 0) & (dist % dil == 0) & (dist // dil <= win // dil)).astype(np.int32)
            bucket = _t5_bucket(np.clip(dist, 0, None))
            for h in range(ATT_HEADS):
                codes[h, s] = np.where(mult > 0, bucket * ATT_HEADS + h, CODE_MASKED)
                logm[h, s] = np.log(np.maximum(mult, 1))
        return codes.reshape(ATT_HEADS * dec_seq, n_keys), logm.reshape(ATT_HEADS * dec_seq, n_keys)
    return tables(wb, 0), tables(dec_seq, wb)


def _attn_p_body(q_ref, kp_ref, kc_ref, vp_ref, vc_ref, bias_ref, o_ref, lse_ref):
    lane = lax.broadcasted_iota(jnp.int32, (1, W_GRP), 1)
    head_of_lane = [(lane >= h * HEAD_DIM) & (lane < (h + 1) * HEAD_DIM) for h in range(ATT_HEADS)]
    n_sub = q_ref.shape[0] // Q_BLK
    first_variant = jnp.minimum(pl.program_id(2), 1)
    for sub in range(n_sub):
        rows = slice(sub * Q_BLK, (sub + 1) * Q_BLK)
        q = q_ref[rows, :]
        if sub == 0:
            k_prev, v_prev, bias = kp_ref[...], vp_ref[...], bias_ref[first_variant]
        else:
            prev_rows = slice((sub - 1) * Q_BLK, sub * Q_BLK)
            k_prev, v_prev, bias = kc_ref[prev_rows, :], vc_ref[prev_rows, :], bias_ref[1]
        k2 = jnp.concatenate([k_prev, kc_ref[rows, :]], axis=0)
        v2 = jnp.concatenate([v_prev, vc_ref[rows, :]], axis=0)
        qs = jnp.concatenate([jnp.where(hm, q, jnp.zeros_like(q)) for hm in head_of_lane], axis=0)
        s = lax.dot_general(qs, k2, (((1,), (1,)), ((), ())), preferred_element_type=F32) + bias
        m = jnp.max(s, axis=-1, keepdims=True)
        pr = jnp.exp(s - m)
        den = jnp.sum(pr, axis=-1, keepdims=True)
        pv = jnp.dot(pr.astype(MXU_DT), v2, preferred_element_type=F32) / den
        lse_rows = m + jnp.log(den)
        o = jnp.zeros((Q_BLK, W_GRP), F32)
        lse = jnp.zeros((Q_BLK, W_GRP), F32)
        for h, hm in enumerate(head_of_lane):
            o = jnp.where(hm, pv[h * Q_BLK:(h + 1) * Q_BLK], o)
            lse = jnp.where(hm, lse_rows[h * Q_BLK:(h + 1) * Q_BLK], lse)
        o_ref[rows, :] = o
        lse_ref[rows, :] = lse


def _attn_prompt_pattern(q, k, v, bias):
    bsz, dil, md, _ = q.shape
    n_sub = math.gcd(ATTN_SUB_BLOCKS, md // Q_BLK)
    cur = pl.BlockSpec((None, None, n_sub * Q_BLK, W_GRP), lambda b, r, n: (b, r, n, 0))
    prv = pl.BlockSpec((None, None, Q_BLK, W_GRP), lambda b, r, n: (b, r, jnp.maximum(n * n_sub - 1, 0), 0))
    bsp = pl.BlockSpec((2, ATT_HEADS * Q_BLK, 2 * Q_BLK), lambda b, r, n: (0, 0, 0))
    return pl.pallas_call(
        _attn_p_body,
        grid=(bsz, dil, md // (n_sub * Q_BLK)),
        in_specs=[cur, prv, cur, prv, cur, bsp],
        out_specs=[cur, cur],
        out_shape=[jax.ShapeDtypeStruct((bsz, dil, md, W_GRP), F32)] * 2,
        compiler_params=_cparams(("arbitrary", "arbitrary", "arbitrary")),
        name=f"attn_prompt_d{dil}",
    )(q, k, k, v, v, bias)


def _attn_merge_body(*refs, dils):
    n_pat = len(dils)
    on_ref, out_ref, stage_ref = refs[2 * n_pat:]
    tm = out_ref.shape[0]
    halves = W_GRP // LANES
    vals = []
    for t in range(2 * n_pat):
        dil = dils[t // 2]
        if dil == 1:
            vals.append(refs[t][...])
            continue
        for r in range(dil):
            blk = refs[t][r]
            for half in range(halves):
                stage_ref.at[t, half][pl.ds(r, tm // dil, stride=dil), :] = blk[:, half * LANES:(half + 1) * LANES]
        vals.append(jnp.concatenate([stage_ref[t, half] for half in range(halves)], axis=1))
    os_, ls_ = vals[0::2], vals[1::2]
    mx = functools.reduce(jnp.maximum, ls_)
    ws = [jnp.exp(l_ - mx) for l_ in ls_]
    num = functools.reduce(lambda a_, b_: a_ + b_, [w_ * o_ for w_, o_ in zip(ws, os_)])
    merged = num / functools.reduce(lambda a_, b_: a_ + b_, ws)
    out_ref[...] = (_rms_rows(merged) * on_ref[...]).astype(out_ref.dtype)


def _attn_merge(pattern_outs, on_d, *, tm):
    dils = tuple(o.shape[1] for o, _ in pattern_outs)
    bsz, _, seq, _ = pattern_outs[0][0].shape
    seq = seq * dils[0]
    specs, args = [], []
    for (o, lse), dil in zip(pattern_outs, dils):
        if dil == 1:
            sp = pl.BlockSpec((None, None, tm, W_GRP), lambda b, j: (b, 0, j, 0))
        else:
            sp = pl.BlockSpec((None, dil, tm // dil, W_GRP), lambda b, j: (b, 0, j, 0))
        specs += [sp, sp]
        args += [o, lse]
    return pl.pallas_call(
        functools.partial(_attn_merge_body, dils=dils),
        grid=(bsz, seq // tm),
        in_specs=specs + [pl.BlockSpec((1, W_GRP), lambda b, j: (0, 0))],
        out_specs=pl.BlockSpec((None, tm, W_GRP), lambda b, j: (b, j, 0)),
        out_shape=jax.ShapeDtypeStruct((bsz, seq, W_GRP), MXU_DT),
        scratch_shapes=[pltpu.VMEM((2 * len(dils), W_GRP // LANES, tm, LANES), F32)],
        compiler_params=_cparams(("arbitrary", "arbitrary")),
        name="attn_merge",
    )(*args, on_d)


def _attn_prompt(qkv_by_dil, bias_all, on_d, *, tm):
    outs = [_attn_prompt_pattern(q, k, v, bias_all[p]) for p, (q, k, v) in enumerate(qkv_by_dil)]
    return _attn_merge(outs, on_d, tm=tm)


def _attn_s_body(q_ref, kn_ref, vn_ref, kt_ref, vt_ref, lw_ref, lwn_ref, on_ref, o_ref):
    q = q_ref[...]
    kn = kn_ref[...].astype(MXU_DT)
    vn = vn_ref[...].astype(MXU_DT)
    nt = (((1,), (1,)), ((), ()))
    outs = []
    for h in range(ATT_HEADS):
        cols = slice(h * HEAD_DIM, (h + 1) * HEAD_DIM)
        qh = q[:, cols]
        sc = jnp.dot(qh, kt_ref[h].astype(MXU_DT), preferred_element_type=F32) + lw_ref[h]
        scn = lax.dot_general(qh, kn[:, cols], nt, preferred_element_type=F32) + lwn_ref[h]
        m = jnp.maximum(jnp.max(sc, axis=-1, keepdims=True), jnp.max(scn, axis=-1, keepdims=True))
        p = jnp.exp(sc - m)
        pn = jnp.exp(scn - m)
        den = jnp.sum(p, axis=-1, keepdims=True) + jnp.sum(pn, axis=-1, keepdims=True)
        num = (lax.dot_general(p.astype(MXU_DT), vt_ref[h].astype(MXU_DT), nt, preferred_element_type=F32)
               + jnp.dot(pn.astype(MXU_DT), vn[:, cols], preferred_element_type=F32))
        outs.append(num / den)
    merged = jnp.concatenate(outs, axis=1)
    o_ref[...] = (_rms_rows(merged) * on_ref[...]).astype(o_ref.dtype)


def _attn_sample(q, kn, vn, cache_k, cache_v, l, lw, lwn, on_d):
    bsz, dec, _ = q.shape
    wb = cache_k.shape[2]
    per_b = pl.BlockSpec((None, dec, W_GRP), lambda b: (b, 0, 0))
    pos_minor = lambda c: jnp.transpose(c, (0, 1, 3, 4, 2))
    cache_spec = pl.BlockSpec((None, None, ATT_HEADS, HEAD_DIM, wb), lambda b: (l, b, 0, 0, 0))
    full = lambda a: pl.BlockSpec(a.shape, lambda b: (0,) * a.ndim)
    return pl.pallas_call(
        _attn_s_body,
        grid=(bsz,),
        in_specs=[per_b, per_b, per_b, cache_spec, cache_spec, full(lw), full(lwn), full(on_d)],
        out_specs=per_b,
        out_shape=jax.ShapeDtypeStruct((bsz, dec, W_GRP), MXU_DT),
        compiler_params=_cparams(("arbitrary",)),
        name="attn_sample",
    )(q, kn, vn, pos_minor(cache_k), pos_minor(cache_v), lw, lwn, on_d)


ROUTER_LANES = 128
HALF_D = D_MODEL // 2
ROW_TILE = HALF_D // LANES
ROW_DT = jnp.uint32


def _store_row_tiles(ref, val):
    n = val.shape[0]
    bits = lambda t: lax.bitcast_convert_type(t.astype(jnp.bfloat16).astype(F32), ROW_DT)
    packed = (bits(val[:, HALF_D:]) & jnp.uint32(0xFFFF0000)) | (bits(val[:, :HALF_D]) >> 16)
    for j in range(ROW_TILE):
        ref[pl.ds(j, n, stride=ROW_TILE), :] = packed[:, j * LANES:(j + 1) * LANES]


def _load_row_tiles(ref):
    n = ref.shape[0] // ROW_TILE
    packed = jnp.concatenate([ref[pl.ds(j, n, stride=ROW_TILE), :] for j in range(ROW_TILE)], axis=1)
    return (lax.bitcast_convert_type(packed << 16, F32),
            lax.bitcast_convert_type(packed & jnp.uint32(0xFFFF0000), F32))


def _post_mix_body(x_ref, a_ref, b_ref, c_ref, d_ref, w_ref, g1_ref, sc_ref, sh_ref, g_ref,
                   wr_ref, br_ref, tri_ref, cnt_in_ref,
                   x1_ref, h2_ref, eid_ref, gate_ref, rank_ref, cnt_ref,
                   run_ref):
    i = pl.program_id(0)

    @pl.when(i == 0)
    def _():
        run_ref[...] = cnt_in_ref[...]

    mix = jnp.concatenate([a_ref[...], b_ref[...], c_ref[...], d_ref[...]], axis=1)
    y = jnp.dot(mix, w_ref[...], preferred_element_type=F32)
    x1 = x_ref[...] + g1_ref[...] * y
    x1_ref[...] = x1
    h2 = _rms_rows(x1) * g_ref[...]
    h2 = h2 * (1.0 + sc_ref[...]) + sh_ref[...]
    _store_row_tiles(h2_ref, h2)
    logits = jnp.dot(h2.astype(MXU_DT), wr_ref[...], preferred_element_type=F32) + br_ref[...]
    tm = logits.shape[0]
    lane = lax.broadcasted_iota(jnp.int32, (tm, ROUTER_LANES), 1)
    big = jnp.int32(10 ** 6)
    is_g = lane < N_GROUPS
    gl = jnp.where(is_g, logits, -jnp.inf)
    gmax = jnp.max(gl, axis=-1, keepdims=True)
    gsel = jnp.min(jnp.where(gl == gmax, lane, big), axis=-1, keepdims=True)
    gprob = 1.0 / jnp.sum(jnp.where(is_g, jnp.exp(logits - gmax), 0.0), axis=-1, keepdims=True)
    lo_lane = N_GROUPS + gsel * EXP_PER_GROUP
    in_grp = (lane >= lo_lane) & (lane < lo_lane + EXP_PER_GROUP)
    el = jnp.where(in_grp, logits, -jnp.inf)
    v1 = jnp.max(el, axis=-1, keepdims=True)
    i1 = jnp.min(jnp.where(el == v1, lane, big), axis=-1, keepdims=True)
    el2 = jnp.where(lane == i1, -jnp.inf, el)
    v2 = jnp.max(el2, axis=-1, keepdims=True)
    i2 = jnp.min(jnp.where(el2 == v2, lane, big), axis=-1, keepdims=True)
    e2w = jnp.exp(v2 - v1)
    gate1 = (1.0 / (1.0 + e2w)) * gprob
    gate2 = (e2w / (1.0 + e2w)) * gprob
    e1 = i1 - N_GROUPS
    e2 = i2 - N_GROUPS
    oh1 = lane == e1
    oh2 = lane == e2
    both = jnp.where(oh1 | oh2, 1.0, 0.0)
    before = jnp.dot(tri_ref[...], both.astype(jnp.bfloat16), preferred_element_type=F32) + run_ref[...]
    r1 = jnp.sum(jnp.where(oh1, before, 0.0), axis=-1, keepdims=True).astype(jnp.int32)
    r2 = jnp.sum(jnp.where(oh2, before, 0.0), axis=-1, keepdims=True).astype(jnp.int32)
    run = run_ref[...] + jnp.sum(both, axis=0, keepdims=True)
    run_ref[...] = run
    cnt_ref[...] = run
    eid_ref[...] = jnp.where(lane == 0, e1, jnp.where(lane == 1, e2, 0))
    gate_ref[...] = jnp.where(lane == 0, gate1, jnp.where(lane == 1, gate2, 0.0))
    rank_ref[...] = jnp.where(lane == 0, r1, jnp.where(lane == 1, r2, 0))


def _post_mix(x, pieces, mod, lp, cnt_in, *, tm, rows_per_mod):
    t = x.shape[0]
    row = lambda i: (i, 0)
    const = lambda i: (0, 0)
    piece = pl.BlockSpec((tm, W_GRP), row)
    wide = pl.BlockSpec((tm, D_MODEL), row)
    lanes = pl.BlockSpec((tm, ROUTER_LANES), row)
    outs = [jax.ShapeDtypeStruct((t, D_MODEL), F32), jax.ShapeDtypeStruct((t * ROW_TILE, LANES), ROW_DT),
            jax.ShapeDtypeStruct((t, ROUTER_LANES), jnp.int32), jax.ShapeDtypeStruct((t, ROUTER_LANES), F32),
            jax.ShapeDtypeStruct((t, ROUTER_LANES), jnp.int32), jax.ShapeDtypeStruct((1, ROUTER_LANES), F32)]
    return pl.pallas_call(
        _post_mix_body,
        grid=(t // tm,),
        in_specs=[wide, piece, piece, piece, piece,
                  pl.BlockSpec((D_MODEL, D_MODEL), const),
                  _mod_spec(mod, 2, tm, rows_per_mod), _mod_spec(mod, 4, tm, rows_per_mod),
                  _mod_spec(mod, 3, tm, rows_per_mod),
                  pl.BlockSpec((1, D_MODEL), const),
                  pl.BlockSpec((D_MODEL, ROUTER_LANES), const), pl.BlockSpec((1, ROUTER_LANES), const),
                  pl.BlockSpec((tm, tm), const), pl.BlockSpec((1, ROUTER_LANES), const)],
        out_specs=[wide, pl.BlockSpec((tm * ROW_TILE, LANES), row), lanes, lanes, lanes,
                   pl.BlockSpec((1, ROUTER_LANES), const)],
        out_shape=outs,
        scratch_shapes=[pltpu.VMEM((1, ROUTER_LANES), F32)],
        compiler_params=_cparams(("arbitrary",)),
        name="post_mix",
    )(x, *pieces, lp['w_out'], mod, mod, mod, lp['norm_ffn'], lp['router_w'], lp['router_b'], lp['tri'], cnt_in)


def _row_copy(src_ref, s, dst_ref, d, sem):
    return pltpu.make_async_copy(src_ref.at[pl.ds(pl.multiple_of(s * ROW_TILE, ROW_TILE), ROW_TILE)],
                                 dst_ref.at[pl.ds(pl.multiple_of(d * ROW_TILE, ROW_TILE), ROW_TILE)], sem)


def _dispatch_body(dest_ref, h_ref, xs_in_ref, xs_ref, sem):
    del xs_in_ref
    tm = h_ref.shape[0] // ROW_TILE
    base = pl.program_id(0) * (2 * tm)

    def issue(t, c):
        _row_copy(h_ref, t, xs_ref, dest_ref[base + 2 * t], sem).start()
        _row_copy(h_ref, t, xs_ref, dest_ref[base + 2 * t + 1], sem).start()
        return c

    lax.fori_loop(0, tm, issue, 0)
    for _ in range(2):
        pltpu.make_async_copy(h_ref, xs_ref.at[pl.ds(0, tm * ROW_TILE)], sem).wait()


def _dispatch(h2, dest_flat, xs, *, tm):
    t = h2.shape[0] // ROW_TILE
    return pl.pallas_call(
        _dispatch_body,
        grid_spec=pltpu.PrefetchScalarGridSpec(
            num_scalar_prefetch=1,
            grid=(t // tm,),
            in_specs=[pl.BlockSpec((tm * ROW_TILE, LANES), lambda i, d: (i, 0)),
                      pl.BlockSpec(memory_space=pl.ANY)],
            out_specs=pl.BlockSpec(memory_space=pl.ANY),
            scratch_shapes=[pltpu.SemaphoreType.DMA(())]),
        out_shape=jax.ShapeDtypeStruct(xs.shape, xs.dtype),
        input_output_aliases={2: 0},
        compiler_params=_cparams(("arbitrary",)),
        name="moe_dispatch",
    )(dest_flat, h2, xs)


def _experts_body(blk_e_ref, n_used_ref, xs_ref, wg_ref, wu_ref, wd_ref, o_ref, wgb_ref, wub_ref, wdb_ref):
    i = pl.program_id(0)
    e = blk_e_ref[i]
    e_prev = blk_e_ref[jnp.maximum(i - 1, 0)]

    @pl.when((i == 0) | (e != e_prev))
    def _():
        wgb_ref[...] = wg_ref[...].astype(MXU_DT)
        wub_ref[...] = wu_ref[...].astype(MXU_DT)
        wdb_ref[...] = wd_ref[...].astype(MXU_DT)

    @pl.when(i < n_used_ref[0])
    def _():
        x_lo, x_hi = [t.astype(MXU_DT) for t in _load_row_tiles(xs_ref)]
        g = (jnp.dot(x_lo, wgb_ref[:HALF_D, :], preferred_element_type=F32)
             + jnp.dot(x_hi, wgb_ref[HALF_D:, :], preferred_element_type=F32))
        u = (jnp.dot(x_lo, wub_ref[:HALF_D, :], preferred_element_type=F32)
             + jnp.dot(x_hi, wub_ref[HALF_D:, :], preferred_element_type=F32))
        hmid = (g * _sigmoid(g)) * u
        _store_row_tiles(o_ref, jnp.dot(hmid.astype(MXU_DT), wdb_ref[...], preferred_element_type=F32))

    @pl.when(i >= n_used_ref[0])
    def _():
        o_ref[...] = jnp.zeros_like(o_ref)


def _experts(xs, blk_e, n_used, wg, wu, wd, l):
    n_blk = xs.shape[0] // (MOE_ROWS * ROW_TILE)
    xmap = lambda i, be, nu: (jnp.minimum(i, nu[0] - 1), 0)
    return pl.pallas_call(
        _experts_body,
        grid_spec=pltpu.PrefetchScalarGridSpec(
            num_scalar_prefetch=2,
            grid=(n_blk,),
            in_specs=[pl.BlockSpec((MOE_ROWS * ROW_TILE, LANES), xmap),
                      pl.BlockSpec((None, None, D_MODEL, D_EXPERT), lambda i, be, nu: (l, be[i], 0, 0)),
                      pl.BlockSpec((None, None, D_MODEL, D_EXPERT), lambda i, be, nu: (l, be[i], 0, 0)),
                      pl.BlockSpec((None, None, D_EXPERT, D_MODEL), lambda i, be, nu: (l, be[i], 0, 0))],
            out_specs=pl.BlockSpec((MOE_ROWS * ROW_TILE, LANES), lambda i, be, nu: (i, 0)),
            scratch_shapes=[pltpu.VMEM((D_MODEL, D_EXPERT), MXU_DT), pltpu.VMEM((D_MODEL, D_EXPERT), MXU_DT),
                            pltpu.VMEM((D_EXPERT, D_MODEL), MXU_DT)]),
        out_shape=jax.ShapeDtypeStruct(xs.shape, ROW_DT),
        compiler_params=_cparams(("arbitrary",)),
        name="moe_experts",
    )(blk_e, n_used, xs, wg, wu, wd)


def _combine_body(dest_ref, ys_ref, x1_ref, gate_ref, g2_ref, x2_ref, buf0_ref, buf1_ref, sem):
    tm = x1_ref.shape[0]
    base = pl.program_id(0) * (2 * tm)

    def issue(t, c):
        _row_copy(ys_ref, dest_ref[base + 2 * t], buf0_ref, t, sem).start()
        _row_copy(ys_ref, dest_ref[base + 2 * t + 1], buf1_ref, t, sem).start()
        return c

    lax.fori_loop(0, tm, issue, 0)
    for buf_ref in (buf0_ref, buf1_ref):
        pltpu.make_async_copy(ys_ref.at[pl.ds(0, tm * ROW_TILE)], buf_ref, sem).wait()
    gate = gate_ref[...]
    lo0, hi0 = _load_row_tiles(buf0_ref)
    lo1, hi1 = _load_row_tiles(buf1_ref)
    g0, g1 = gate[:, 0:1], gate[:, 1:2]
    y = jnp.concatenate([lo0 * g0 + lo1 * g1, hi0 * g0 + hi1 * g1], axis=1)
    x2_ref[...] = x1_ref[...] + g2_ref[...] * y


def _combine(ys, dest_flat, x1, gate, mod, *, tm, rows_per_mod):
    t = x1.shape[0]
    if mod.ndim == 4:
        g2_spec = pl.BlockSpec((None, None, 1, D_MODEL), lambda i, d: (5, (i * tm) // rows_per_mod, 0, 0))
    else:
        g2_spec = pl.BlockSpec((None, tm, D_MODEL), lambda i, d: (5, i, 0))
    return pl.pallas_call(
        _combine_body,
        grid_spec=pltpu.PrefetchScalarGridSpec(
            num_scalar_prefetch=1,
            grid=(t // tm,),
            in_specs=[pl.BlockSpec(memory_space=pl.ANY),
                      pl.BlockSpec((tm, D_MODEL), lambda i, d: (i, 0)),
                      pl.BlockSpec((tm, ROUTER_LANES), lambda i, d: (i, 0)),
                      g2_spec],
            out_specs=pl.BlockSpec((tm, D_MODEL), lambda i, d: (i, 0)),
            scratch_shapes=[pltpu.VMEM((tm * ROW_TILE, LANES), ROW_DT), pltpu.VMEM((tm * ROW_TILE, LANES), ROW_DT),
                            pltpu.SemaphoreType.DMA(())]),
        out_shape=jax.ShapeDtypeStruct((t, D_MODEL), F32),
        compiler_params=_cparams(("arbitrary",)),
        name="moe_combine",
    )(dest_flat, ys, x1, gate, mod)


def _routing_tables(cnt, eids, ranks, n_blk):
    counts = cnt[0, :N_EXPERTS].astype(jnp.int32)
    padded = (counts + MOE_ROWS - 1) // MOE_ROWS * MOE_ROWS
    ends = jnp.cumsum(padded)
    starts = ends - padded
    experts = jnp.arange(N_EXPERTS, dtype=jnp.int32)
    start_of = lambda e: jnp.sum(jnp.where(e[..., None] == experts, starts, 0), axis=-1)
    dests = [(start_of(e[:, :2]) + r[:, :2]).reshape(-1) for e, r in zip(eids, ranks)]
    blk_start = jnp.arange(n_blk, dtype=jnp.int32) * MOE_ROWS
    blk_e = jnp.minimum(jnp.sum((ends[None, :] <= blk_start[:, None]).astype(jnp.int32), axis=1), N_EXPERTS - 1)
    n_used = (ends[-1] // MOE_ROWS).astype(jnp.int32).reshape(1)
    return dests, blk_e, n_used


def _block_diag(w):
    g, r, c = w.shape
    eye = jnp.eye(g, dtype=w.dtype)
    return (eye[:, None, :, None] * w[:, :, None, :]).reshape(g * r, g * c)


def _s5_discretise(log_dt, a_re, a_im, b_re, b_im):
    step = jnp.exp(log_dt)[:, None]
    mag = jnp.exp(a_re * step)
    ang = a_im * step
    abr = mag * jnp.cos(ang)
    abi = mag * jnp.sin(ang)
    den = a_re * a_re + a_im * a_im
    zr = ((abr - 1.0) * a_re + abi * a_im) / den
    zi = (abi * a_re - (abr - 1.0) * a_im) / den
    bbr = zr[..., None] * b_re - zi[..., None] * b_im
    bbi = zr[..., None] * b_im + zi[..., None] * b_re
    return abr, abi, bbr, bbi


def _prep_layer(P, l):
    row = lambda a: a.reshape(1, -1)
    abr, abi, bbr, bbi = _s5_discretise(P['s5_log_dt'][l], P['s5_a_re'][l], P['s5_a_im'][l],
                                        P['s5_b_re'][l], P['s5_b_im'][l])
    on = P['out_norm'][l]
    bones = _block_diag(jnp.full((ATT_HEADS, HEAD_DIM, HEAD_DIM), 1.0 / HEAD_DIM, F32)).astype(jnp.bfloat16)
    return {
        'norm_mix': row(P['norm_mix'][l]), 'norm_ffn': row(P['norm_ffn'][l]),
        'w_in': P['w_in'][l].astype(MXU_DT), 'w_out': P['w_out'][l].astype(MXU_DT),
        'bones': bones,
        'qg': row(jnp.tile(P['q_norm'][l], ATT_HEADS)), 'kg': row(jnp.tile(P['k_norm'][l], ATT_HEADS)),
        'lru_conv_w': P['lru_conv_w'][l], 'lru_conv_b': row(P['lru_conv_b'][l]),
        'lru_wa_blk': _block_diag(P['lru_wa'][l]).astype(MXU_DT), 'lru_ba': row(P['lru_ba'][l]),
        'lru_wx_blk': _block_diag(P['lru_wx'][l]).astype(MXU_DT), 'lru_bx': row(P['lru_bx'][l]),
        'lru_lambda': row(P['lru_lambda'][l]), 'sconv_w': P['sconv_w'][l],
        'on_a': row(on[0:W_GRP]), 'on_b': row(on[W_GRP:2 * W_GRP]),
        'on_c': row(on[2 * W_GRP:3 * W_GRP]), 'on_d': row(on[3 * W_GRP:]),
        's5_bb': jnp.concatenate([_block_diag(bbr.transpose(0, 2, 1)), _block_diag(bbi.transpose(0, 2, 1))],
                                 axis=1).astype(MXU_DT),
        's5_cre': _block_diag(P['s5_c_re'][l].transpose(0, 2, 1)).astype(MXU_DT),
        's5_cim': _block_diag(P['s5_c_im'][l].transpose(0, 2, 1)).astype(MXU_DT),
        's5_abr': row(abr), 's5_abi': row(abi), 's5_d': row(P['s5_d'][l]),
        's5_glu_w': P['s5_glu_w'][l].astype(MXU_DT), 's5_glu_b': row(P['s5_glu_b'][l]),
        'router_w': jnp.zeros((D_MODEL, ROUTER_LANES), F32)
                       .at[:, :N_GROUPS].set(P['router_g_w'][l])
                       .at[:, N_GROUPS:N_GROUPS + N_EXPERTS].set(P['router_e_w'][l]).astype(MXU_DT),
        'router_b': jnp.zeros((1, ROUTER_LANES), F32)
                       .at[0, :N_GROUPS].set(P['router_g_b'][l])
                       .at[0, N_GROUPS:N_GROUPS + N_EXPERTS].set(P['router_e_b'][l]),
    }


TOKEN_TILE = 512
SEQ_TILE = 256

_PARAM_NAMES = ('rel_bias', 'mod_w', 'mod_b', 'norm_mix', 'norm_ffn', 'w_in', 'lru_conv_w', 'lru_conv_b',
                'lru_wa', 'lru_ba', 'lru_wx', 'lru_bx', 'lru_lambda', 'sconv_w', 's5_log_dt', 's5_a_re',
                's5_a_im', 's5_b_re', 's5_b_im', 's5_c_re', 's5_c_im', 's5_d', 's5_glu_w', 's5_glu_b',
                'q_norm', 'k_norm', 'out_norm', 'w_out', 'router_g_w', 'router_g_b', 'router_e_w',
                'router_e_b', 'moe_w_gate', 'moe_w_up', 'moe_w_down')


def _mixers(x, mod, lp, st, attn_fn, *, batch, seq, tm, tl, rows_per_mod, dils=(), kv_window=None):
    res = _in_proj(x, mod, lp['norm_mix'], lp['w_in'], lp['bones'], lp['qg'], lp['kg'],
                   tm=tm, rows_per_mod=rows_per_mod, dils=dils, seq=seq, kv_window=kv_window)
    zm, q, kf, vf, kb, vb = res[:6]
    zm3 = zm.reshape(batch, seq, 6 * W_GRP)
    oa, ob, lru_h, lru_conv, sconv = _mixer_ab(zm3, lp, st['lru_h'], st['lru_conv'], st['sconv'], tl=tl)
    oc, s5_re, s5_im = _s5_mixer(zm3, lp, st['s5_re'], st['s5_im'], tl=tl)
    r3 = lambda t: t.reshape(batch, seq, W_GRP)
    if kv_window is None:
        kf, vf = r3(kf), r3(vf)
        win = lambda t: t.reshape(batch, seq, ATT_HEADS, HEAD_DIM)
    else:
        win = lambda t: jnp.transpose(t.reshape(batch, ATT_HEADS, HEAD_DIM, kv_window), (0, 3, 1, 2))
    od = attn_fn(r3(q), kf, vf, r3(kb), r3(vb), res[6:])
    flat = lambda t: t.reshape(batch * seq, W_GRP)
    new_st = {'lru_h': lru_h[:, 0], 'lru_conv': lru_conv, 'sconv': sconv,
              's5_re': s5_re.reshape(batch, S5_GROUPS, S5_STATE), 's5_im': s5_im.reshape(batch, S5_GROUPS, S5_STATE),
              'win_k': win(kf), 'win_v': win(vf)}
    return [flat(oa), flat(ob), flat(oc), flat(od)], new_st


def kernel(x_prompt, x_sample, c_prompt, c_sample, state_lru_h, state_lru_conv, state_sconv, state_s5_re, state_s5_im, cache_win_k, cache_win_v, rel_bias, mod_w, mod_b, norm_mix, norm_ffn, w_in, lru_conv_w, lru_conv_b, lru_wa, lru_ba, lru_wx, lru_bx, lru_lambda, sconv_w, s5_log_dt, s5_a_re, s5_a_im, s5_b_re, s5_b_im, s5_c_re, s5_c_im, s5_d, s5_glu_w, s5_glu_b, q_norm, k_norm, out_norm, w_out, router_g_w, router_g_b, router_e_w, router_e_b, moe_w_gate, moe_w_up, moe_w_down):
    P = dict(zip(_PARAM_NAMES, (rel_bias, mod_w, mod_b, norm_mix, norm_ffn, w_in, lru_conv_w, lru_conv_b,
                                lru_wa, lru_ba, lru_wx, lru_bx, lru_lambda, sconv_w, s5_log_dt, s5_a_re,
                                s5_a_im, s5_b_re, s5_b_im, s5_c_re, s5_c_im, s5_d, s5_glu_w, s5_glu_b,
                                q_norm, k_norm, out_norm, w_out, router_g_w, router_g_b, router_e_w,
                                router_e_b, moe_w_gate, moe_w_up, moe_w_down)))
    bp, seq, d = x_prompt.shape
    bs, dec, _ = x_sample.shape
    depth = mod_w.shape[0]
    tp, ts = bp * seq, bs * dec
    wb = cache_win_k.shape[2]
    wp = min(PATTERNS[-1][0], seq)
    tm_p = min(TOKEN_TILE, tp)
    tm_s = min(TOKEN_TILE, ts)
    tl_p = min(SEQ_TILE, seq)

    nc = -(-(bp + bs) // 8) * 8
    c_all = jnp.zeros((nc, d), F32).at[:bp].set(c_prompt).at[bp:bp + bs].set(c_sample)
    mod_all = _modulation(c_all, mod_w, mod_b)
    bias_p = _bias_table(rel_bias, _prompt_bias_codes()).reshape(len(PATTERNS), 2, ATT_HEADS * Q_BLK, 2 * Q_BLK)
    dils_p = tuple(dil for _, dil in PATTERNS if dil > 1)
    (lw_codes, lw_logm), (lwn_codes, lwn_logm) = _sample_bias_codes(dec, wb)
    lw_s = (_bias_table(rel_bias, lw_codes) + lw_logm).reshape(ATT_HEADS, dec, wb)
    lwn_s = (_bias_table(rel_bias, lwn_codes) + lwn_logm).reshape(ATT_HEADS, dec, dec)
    tri = jnp.asarray(np.tril(np.ones((TOKEN_TILE, TOKEN_TILE), np.float32), -1), jnp.bfloat16)
    n_blk = (2 * (tp + ts)) // MOE_ROWS + N_EXPERTS

    zero_st = {'lru_h': jnp.zeros((bp, 1, W_GRP), F32), 'lru_conv': jnp.zeros((bp, LRU_CONV - 1, W_GRP), F32),
               'sconv': jnp.zeros((bp, SCONV_W - 1, W_GRP), F32),
               's5_re': jnp.zeros((bp, 1, S5_N), F32), 's5_im': jnp.zeros((bp, 1, S5_N), F32)}
    names = ('lru_h', 'lru_conv', 'sconv', 's5_re', 's5_im', 'win_k', 'win_v')
    acc_p = {n: [] for n in names}
    acc_s = {n: [] for n in names}
    xp = x_prompt.reshape(tp, d)
    xs = x_sample.reshape(ts, d)
    for l in range(depth):
        lp = _prep_layer(P, l)
        lp['tri'] = tri
        m6 = mod_all[l].reshape(nc, 6, d).transpose(1, 0, 2)
        mod_p = m6[:, :bp].reshape(6, bp, 1, d)
        mod_s = jnp.repeat(m6[:, bp:bp + bs], dec, axis=1)
        attn_p = lambda q, kf, vf, kb, vb, ex: _attn_prompt(
            [(q[:, None], kb[:, None], vb[:, None])] + [tuple(ex[3 * t:3 * t + 3]) for t in range(len(dils_p))],
            bias_p, lp['on_d'], tm=tm_p)
        pieces_p, st_p = _mixers(xp, mod_p, lp, zero_st, attn_p, batch=bp, seq=seq, tm=tm_p, tl=tl_p,
                                 rows_per_mod=seq, dils=dils_p, kv_window=wp)
        cnt0 = jnp.zeros((1, ROUTER_LANES), F32)
        x1p, h2p, eid_p, gate_p, rank_p, cnt = _post_mix(xp, pieces_p, mod_p, lp, cnt0, tm=tm_p, rows_per_mod=seq)
        samp_st = {'lru_h': state_lru_h[l][:, None], 'lru_conv': state_lru_conv[l], 'sconv': state_sconv[l],
                   's5_re': state_s5_re[l].reshape(bs, 1, S5_N), 's5_im': state_s5_im[l].reshape(bs, 1, S5_N)}
        attn_s = lambda q, kf, vf, kb, vb, ex: _attn_sample(
            q, kf, vf, cache_win_k, cache_win_v, l, lw_s, lwn_s, lp['on_d'])
        pieces_s, st_s = _mixers(xs, mod_s, lp, samp_st, attn_s, batch=bs, seq=dec, tm=tm_s, tl=dec,
                                 rows_per_mod=dec)
        x1s, h2s, eid_s, gate_s, rank_s, cnt = _post_mix(xs, pieces_s, mod_s, lp, cnt, tm=tm_s, rows_per_mod=dec)
        dests, blk_e, n_used = _routing_tables(cnt, [eid_p, eid_s], [rank_p, rank_s], n_blk)
        slots = jnp.zeros((n_blk * MOE_ROWS * ROW_TILE, LANES), ROW_DT)
        slots = _dispatch(h2p, dests[0], slots, tm=tm_p)
        slots = _dispatch(h2s, dests[1], slots, tm=tm_s)
        ys = _experts(slots, blk_e, n_used, moe_w_gate, moe_w_up, moe_w_down, l)
        xp = _combine(ys, dests[0], x1p, gate_p, mod_p, tm=tm_p, rows_per_mod=seq)
        xs = _combine(ys, dests[1], x1s, gate_s, mod_s, tm=tm_s, rows_per_mod=dec)
        for n in names:
            acc_p[n].append(st_p[n])
            acc_s[n].append(st_s[n])
    new_p = {n: jnp.stack(acc_p[n], axis=0) for n in names}
    new_s = {n: jnp.stack(acc_s[n], axis=0) for n in names}
    return (xp.reshape(bp, seq, d), xs.reshape(bs, dec, d),
            new_p['lru_h'], new_p['lru_conv'], new_p['sconv'], new_p['s5_re'], new_p['s5_im'],
            new_p['win_k'], new_p['win_v'],
            new_s['lru_h'], new_s['lru_conv'], new_s['sconv'], new_s['s5_re'], new_s['s5_im'],
            new_s['win_k'], new_s['win_v'])
```

```python
import functools
import math

import numpy as np
import jax
import jax.numpy as jnp
from jax import lax
from jax.experimental import pallas as pl
from jax.experimental.pallas import tpu as pltpu

F32 = jnp.float32
MXU_DT = jnp.bfloat16
HIGHEST = lax.Precision.HIGHEST

D_MODEL = 1024
DEPTH = 4
W_GRP = 256
N_Z = 9
LRU_HEADS = 4
LRU_CONV = 4
LRU_C = 8.0
SCONV_W = 3
S5_CH = 16
S5_GROUPS = 16
S5_STATE = 64
S5_N = S5_GROUPS * S5_STATE
ATT_HEADS = 4
HEAD_DIM = 64
PATTERNS = ((128, 1), (512, 4), (2048, 16))
Q_BLK = 128
REL_BUCKETS = 32
REL_MAX_DIST = 2048
N_GROUPS = 4
EXP_PER_GROUP = 8
N_EXPERTS = 32
D_EXPERT = 512
EPS = 1e-6
NEG = -1e30

VMEM_LIMIT = 56 * 1024 * 1024
LANES = 128
MOE_ROWS = 512
SEGS = 8
ATTN_SUB_BLOCKS = 4


def _cparams(sem):
    return pltpu.CompilerParams(dimension_semantics=sem, vmem_limit_bytes=VMEM_LIMIT)


def _gelu(x):
    return 0.5 * x * (1.0 + jnp.tanh(math.sqrt(2.0 / math.pi) * (x + 0.044715 * (x * x * x))))


def _sigmoid(x):
    return 1.0 / (1.0 + jnp.exp(-x))


def _rms_rows(x):
    return x * lax.rsqrt(jnp.mean(x * x, axis=-1, keepdims=True) + EPS)


def _shift_rows(x, s, fill, row):
    return jnp.where(row >= s, pltpu.roll(x, s, 0), fill)


def _mod_body(c_ref, w_ref, b_ref, o_ref):
    c = c_ref[...]
    s = c * _sigmoid(c)
    o_ref[...] = jnp.dot(s.astype(MXU_DT), w_ref[...].astype(MXU_DT), preferred_element_type=F32) + b_ref[...]


def _modulation(c_all, mod_w, mod_b):
    nb = c_all.shape[0]
    depth = mod_w.shape[0]
    n_out = mod_w.shape[2]
    tn = D_MODEL
    return pl.pallas_call(
        _mod_body,
        grid=(depth, n_out // tn),
        in_specs=[pl.BlockSpec((nb, D_MODEL), lambda l, j: (0, 0)),
                  pl.BlockSpec((None, D_MODEL, tn), lambda l, j: (l, 0, j)),
                  pl.BlockSpec((None, 1, tn), lambda l, j: (l, 0, j))],
        out_specs=pl.BlockSpec((None, nb, tn), lambda l, j: (l, 0, j)),
        out_shape=jax.ShapeDtypeStruct((depth, nb, n_out), F32),
        compiler_params=_cparams(("arbitrary", "arbitrary")),
        name="modulation",
    )(c_all, mod_w, mod_b.reshape(depth, 1, n_out))


def _head_mean_sq(t, bones):
    sq = t * t
    hi = sq.astype(jnp.bfloat16)
    lo = (sq - hi.astype(F32)).astype(jnp.bfloat16)
    return (jnp.dot(hi, bones, preferred_element_type=F32)
            + jnp.dot(lo, bones, preferred_element_type=F32))


def _inproj_body(*refs, dils, kv_pos_minor):
    (x_ref, sc_ref, sh_ref, g_ref, w_ref, bones_ref, qg_ref, kg_ref,
     zm_ref, q_ref, kf_ref, vf_ref, kb_ref, vb_ref) = refs[:14]
    x = x_ref[...]
    h = _rms_rows(x) * g_ref[...]
    h = h * (1.0 + sc_ref[...]) + sh_ref[...]
    z = jnp.dot(h.astype(MXU_DT), w_ref[...], preferred_element_type=F32)
    nm = 6 * W_GRP
    zm_ref[...] = z[:, :nm]
    q = z[:, nm:nm + W_GRP]
    k = z[:, nm + W_GRP:nm + 2 * W_GRP]
    v = z[:, nm + 2 * W_GRP:]
    bones = bones_ref[...]
    qn = (q * lax.rsqrt(_head_mean_sq(q, bones) + EPS) * qg_ref[...]) * (HEAD_DIM ** -0.5)
    kn = k * lax.rsqrt(_head_mean_sq(k, bones) + EPS) * kg_ref[...]
    q_ref[...] = qn.astype(q_ref.dtype)
    if kv_pos_minor:
        kf_ref[...] = kn.T
        vf_ref[...] = v.T
    else:
        kf_ref[...] = kn
        vf_ref[...] = v
    kb_ref[...] = kn.astype(kb_ref.dtype)
    vb_ref[...] = v.astype(vb_ref.dtype)
    if dils:
        stage_ref = refs[-1]
        tm = x_ref.shape[0]
        for a, val in enumerate((qn, kn, v)):
            for half in range(W_GRP // LANES):
                stage_ref[a, half] = val[:, half * LANES:(half + 1) * LANES]
        for di, dil in enumerate(dils):
            for a in range(3):
                out_ref = refs[14 + 3 * di + a]
                for r in range(dil):
                    for half in range(W_GRP // LANES):
                        out_ref[r, :, half * LANES:(half + 1) * LANES] = (
                            stage_ref.at[a, half][pl.ds(r, tm // dil, stride=dil), :].astype(out_ref.dtype))


def _mod_spec(mod, k, tm, rows_per_mod):
    if mod.ndim == 4:
        return pl.BlockSpec((None, None, 1, D_MODEL), lambda i: (k, (i * tm) // rows_per_mod, 0, 0))
    return pl.BlockSpec((None, tm, D_MODEL), lambda i: (k, i, 0))


def _in_proj(x, mod, norm_g, w_in, bones, qg, kg, *, tm, rows_per_mod, dils=(), seq=None, kv_window=None):
    t = x.shape[0]
    n_in = w_in.shape[1]
    nm = 6 * W_GRP
    row = lambda i: (i, 0)
    const = lambda i: (0, 0)
    outs = [jax.ShapeDtypeStruct((t, nm), F32),
            jax.ShapeDtypeStruct((t, W_GRP), MXU_DT),
            jax.ShapeDtypeStruct((t, W_GRP), F32),
            jax.ShapeDtypeStruct((t, W_GRP), F32),
            jax.ShapeDtypeStruct((t, W_GRP), MXU_DT),
            jax.ShapeDtypeStruct((t, W_GRP), MXU_DT)]
    out_specs = [pl.BlockSpec((tm, nm), row)] + [pl.BlockSpec((tm, W_GRP), row)] * 5
    scratch = []
    if kv_window is not None:
        tiles_per_seq = seq // tm
        skip = (seq - kv_window) // tm
        assert (seq - kv_window) % tm == 0 and kv_window % tm == 0
        win = pl.BlockSpec((None, W_GRP, tm),
                           lambda i: (i // tiles_per_seq, 0, jnp.maximum(i % tiles_per_seq - skip, 0)))
        outs[2] = outs[3] = jax.ShapeDtypeStruct((t // seq, W_GRP, kv_window), F32)
        out_specs[2] = out_specs[3] = win
    if dils:
        tiles_per_seq = seq // tm
        for dil in dils:
            outs += [jax.ShapeDtypeStruct((t // seq, dil, seq // dil, W_GRP), MXU_DT)] * 3
            out_specs += [pl.BlockSpec((None, dil, tm // dil, W_GRP),
                                       lambda i: (i // tiles_per_seq, 0, i % tiles_per_seq, 0))] * 3
        scratch = [pltpu.VMEM((3, W_GRP // LANES, tm, LANES), F32)]
    return pl.pallas_call(
        functools.partial(_inproj_body, dils=tuple(dils), kv_pos_minor=kv_window is not None),
        grid=(t // tm,),
        in_specs=[pl.BlockSpec((tm, D_MODEL), row),
                  _mod_spec(mod, 1, tm, rows_per_mod),
                  _mod_spec(mod, 0, tm, rows_per_mod),
                  pl.BlockSpec((1, D_MODEL), const),
                  pl.BlockSpec((D_MODEL, n_in), const),
                  pl.BlockSpec((W_GRP, W_GRP), const),
                  pl.BlockSpec((1, W_GRP), const),
                  pl.BlockSpec((1, W_GRP), const)],
        out_specs=out_specs,
        out_shape=outs,
        scratch_shapes=scratch,
        compiler_params=_cparams(("arbitrary",)),
        name="in_proj",
    )(x, mod, mod, norm_g, w_in, bones, qg, kg)


def _softplus(x):
    return jnp.maximum(x, 0.0) + jnp.log(1.0 + jnp.exp(-jnp.abs(x)))


def _mixer_ab_body(xa_ref, ga_ref, gb_ref, gc_ref, xb_ref,
                   cw_ref, cb_ref, wa_ref, ba_ref, wx_ref, bx_ref, lam_ref, sw_ref, on_a_ref, on_b_ref,
                   h0_ref, conv0_ref, sconv0_ref,
                   oa_ref, ob_ref, hn_ref, convn_ref, sconvn_ref,
                   xe_ref, pe_ref, hc_ref):
    i = pl.program_id(1)
    tl = xa_ref.shape[0]

    @pl.when(i == 0)
    def _():
        xe_ref[8 - (LRU_CONV - 1):8, :] = conv0_ref[...]
        pe_ref[8 - (SCONV_W - 1):8, :] = sconv0_ref[...]
        hc_ref[...] = h0_ref[...]

    row = lax.broadcasted_iota(jnp.int32, (tl, 1), 0)
    xa = xa_ref[...]
    xe_ref[8:, :] = xa
    cw = cw_ref[...]
    xc = cw[LRU_CONV - 1:LRU_CONV, :] * xa
    for s in range(1, LRU_CONV):
        xc = xc + cw[LRU_CONV - 1 - s:LRU_CONV - s, :] * xe_ref[8 - s:8 - s + tl, :]
    xc = xc + cb_ref[...]
    convn_ref[...] = xa[tl - (LRU_CONV - 1):, :]
    xe_ref[0:8, :] = xa[tl - 8:, :]
    xcb = xc.astype(MXU_DT)
    r = _sigmoid(jnp.dot(xcb, wa_ref[...], preferred_element_type=F32) + ba_ref[...])
    ig = _sigmoid(jnp.dot(xcb, wx_ref[...], preferred_element_type=F32) + bx_ref[...])
    log_a = (-LRU_C * r) * _softplus(-lam_ref[...])
    a = jnp.exp(log_a)
    b = jnp.sqrt(-jnp.tanh(log_a) * (a * a + 1.0)) * (ig * xc)
    s = 1
    while s < tl:
        b = a * _shift_rows(b, s, 0.0, row) + b
        a = a * _shift_rows(a, s, 1.0, row)
        s *= 2
    h = b + a * hc_ref[...]
    hc_ref[...] = h[tl - 1:, :]
    hn_ref[...] = h[tl - 1:, :]
    out_a = h * _gelu(ga_ref[...])
    oa_ref[...] = (_rms_rows(out_a) * on_a_ref[...]).astype(oa_ref.dtype)
    p = gc_ref[...] * xb_ref[...]
    pe_ref[8:, :] = p
    sw = sw_ref[...]
    yb = sw[SCONV_W - 1:SCONV_W, :] * p
    for s in range(1, SCONV_W):
        yb = yb + sw[SCONV_W - 1 - s:SCONV_W - s, :] * pe_ref[8 - s:8 - s + tl, :]
    sconvn_ref[...] = p[tl - (SCONV_W - 1):, :]
    pe_ref[0:8, :] = p[tl - 8:, :]
    out_b = gb_ref[...] * yb
    ob_ref[...] = (_rms_rows(out_b) * on_b_ref[...]).astype(ob_ref.dtype)


def _mixer_ab(zm, lp, h0, conv0, sconv0, *, tl):
    bsz, seq, _ = zm.shape
    col = lambda c: pl.BlockSpec((None, tl, W_GRP), lambda b, i, c=c: (b, i, c))
    const = lambda shp: pl.BlockSpec(shp, lambda b, i: (0,) * len(shp))
    per_b = lambda n: pl.BlockSpec((None, n, W_GRP), lambda b, i: (b, 0, 0))
    outs = [jax.ShapeDtypeStruct((bsz, seq, W_GRP), MXU_DT),
            jax.ShapeDtypeStruct((bsz, seq, W_GRP), MXU_DT),
            jax.ShapeDtypeStruct((bsz, 1, W_GRP), F32),
            jax.ShapeDtypeStruct((bsz, LRU_CONV - 1, W_GRP), F32),
            jax.ShapeDtypeStruct((bsz, SCONV_W - 1, W_GRP), F32)]
    return pl.pallas_call(
        _mixer_ab_body,
        grid=(bsz, seq // tl),
        in_specs=[col(0), col(1), col(2), col(3), col(4),
                  const((LRU_CONV, W_GRP)), const((1, W_GRP)),
                  const((W_GRP, W_GRP)), const((1, W_GRP)),
                  const((W_GRP, W_GRP)), const((1, W_GRP)),
                  const((1, W_GRP)), const((SCONV_W, W_GRP)),
                  const((1, W_GRP)), const((1, W_GRP)),
                  per_b(1), per_b(LRU_CONV - 1), per_b(SCONV_W - 1)],
        out_specs=[pl.BlockSpec((None, tl, W_GRP), lambda b, i: (b, i, 0)),
                   pl.BlockSpec((None, tl, W_GRP), lambda b, i: (b, i, 0)),
                   per_b(1), per_b(LRU_CONV - 1), per_b(SCONV_W - 1)],
        out_shape=outs,
        scratch_shapes=[pltpu.VMEM((tl + 8, W_GRP), F32),
                        pltpu.VMEM((tl + 8, W_GRP), F32),
                        pltpu.VMEM((1, W_GRP), F32)],
        compiler_params=_cparams(("arbitrary", "arbitrary")),
        name="mixer_ab",
    )(zm, zm, zm, zm, zm,
      lp['lru_conv_w'], lp['lru_conv_b'], lp['lru_wa_blk'], lp['lru_ba'], lp['lru_wx_blk'], lp['lru_bx'],
      lp['lru_lambda'], lp['sconv_w'], lp['on_a'], lp['on_b'],
      h0, conv0, sconv0)


def _s5_body(u_ref, bb_ref, cre_ref, cim_ref, ar_ref, ai_ref, d_ref, gw_ref, gb_ref, on_ref,
             h0r_ref, h0i_ref,
             o_ref, hnr_ref, hni_ref,
             tr_ref, ti_ref, pr_ref, pi_ref, hr_ref, hi_ref, lr_ref, li_ref, stage_ref):
    b = pl.program_id(0)
    i = pl.program_id(1)
    tl = u_ref.shape[0]
    steps = tl // SEGS
    groups = S5_N // LANES
    halves = W_GRP // LANES
    ar = ar_ref[...]
    ai = ai_ref[...]

    def powers(base_r, base_i, n, first):
        row = lax.broadcasted_iota(jnp.int32, (n, 1), 0)
        tr = jnp.broadcast_to(base_r, (n, S5_N))
        ti = jnp.broadcast_to(base_i, (n, S5_N))
        s = first
        while s < n:
            sr = _shift_rows(tr, s, 1.0, row)
            si = _shift_rows(ti, s, 0.0, row)
            tr, ti = tr * sr - ti * si, tr * si + ti * sr
            s *= 2
        return tr, ti

    @pl.when((b == 0) & (i == 0))
    def _():
        tr, ti = powers(ar, ai, tl, SEGS)
        tr_ref[...] = tr
        ti_ref[...] = ti
        pr, pi = powers(tr[tl - 1:, :], ti[tl - 1:, :], SEGS, 1)
        pr_ref[...] = pr
        pi_ref[...] = pi

    @pl.when(i == 0)
    def _():
        hr_ref[...] = h0r_ref[...]
        hi_ref[...] = h0i_ref[...]

    u = u_ref[...]
    if steps > 1:
        for half in range(halves):
            stage_ref[half] = u[:, half * LANES:(half + 1) * LANES]
        u = jnp.concatenate(
            [jnp.concatenate([stage_ref.at[half][pl.ds(j, SEGS, stride=steps), :] for half in range(halves)], axis=1)
             for j in range(steps)], axis=0)
    bu = jnp.dot(u.astype(MXU_DT), bb_ref[...], preferred_element_type=F32)
    a_r = [jnp.broadcast_to(ar[:, c * LANES:(c + 1) * LANES], (SEGS, LANES)) for c in range(groups)]
    a_i = [jnp.broadcast_to(ai[:, c * LANES:(c + 1) * LANES], (SEGS, LANES)) for c in range(groups)]
    loc_r = [jnp.zeros((SEGS, LANES), F32) for _ in range(groups)]
    loc_i = [jnp.zeros((SEGS, LANES), F32) for _ in range(groups)]
    for j in range(steps):
        rows = slice(j * SEGS, (j + 1) * SEGS)
        for c in range(groups):
            cols = slice(c * LANES, (c + 1) * LANES)
            nr = (a_r[c] * loc_r[c] - a_i[c] * loc_i[c]) + bu[rows, c * LANES:(c + 1) * LANES]
            ni = (a_r[c] * loc_i[c] + a_i[c] * loc_r[c]) + bu[rows, S5_N + c * LANES:S5_N + (c + 1) * LANES]
            loc_r[c], loc_i[c] = nr, ni
            lr_ref[rows, cols] = nr
            li_ref[rows, cols] = ni
    er = jnp.concatenate(loc_r, axis=1)
    ei = jnp.concatenate(loc_i, axis=1)
    seg = lax.broadcasted_iota(jnp.int32, (SEGS, 1), 0)
    pr = pr_ref[...]
    pi = pi_ref[...]
    mr, mi = pr[0:1, :], pi[0:1, :]
    s = 1
    while s < SEGS:
        sr = _shift_rows(er, s, 0.0, seg)
        si = _shift_rows(ei, s, 0.0, seg)
        er, ei = er + (mr * sr - mi * si), ei + (mr * si + mi * sr)
        mr, mi = mr * mr - mi * mi, 2.0 * (mr * mi)
        s *= 2
    cr = hr_ref[...]
    ci = hi_ref[...]
    er, ei = er + (pr * cr - pi * ci), ei + (pr * ci + pi * cr)
    in_r = jnp.where(seg >= 1, pltpu.roll(er, 1, 0), cr)
    in_i = jnp.where(seg >= 1, pltpu.roll(ei, 1, 0), ci)
    hr_ref[...] = er[SEGS - 1:, :]
    hi_ref[...] = ei[SEGS - 1:, :]
    hnr_ref[...] = er[SEGS - 1:, :]
    hni_ref[...] = ei[SEGS - 1:, :]
    tr = tr_ref[...]
    ti = ti_ref[...]
    sr = jnp.tile(in_r, (steps, 1))
    si = jnp.tile(in_i, (steps, 1))
    hr = lr_ref[...] + (tr * sr - ti * si)
    hi = li_ref[...] + (tr * si + ti * sr)
    y = (jnp.dot(hr.astype(MXU_DT), cre_ref[...], preferred_element_type=F32)
         - jnp.dot(hi.astype(MXU_DT), cim_ref[...], preferred_element_type=F32)) + d_ref[...] * u
    g = jnp.dot(_gelu(y).astype(MXU_DT), gw_ref[...], preferred_element_type=F32) + gb_ref[...]
    out = g[:, :W_GRP] * _sigmoid(g[:, W_GRP:])
    out = _rms_rows(out) * on_ref[...]
    if steps > 1:
        for half in range(halves):
            stage_ref[half] = out[:, half * LANES:(half + 1) * LANES]
        out = jnp.concatenate(
            [jnp.concatenate([stage_ref.at[half][pl.ds(sg, steps, stride=SEGS), :] for half in range(halves)], axis=1)
             for sg in range(SEGS)], axis=0)
    o_ref[...] = out.astype(o_ref.dtype)


def _s5_mixer(zm, lp, h0r, h0i, *, tl):
    bsz, seq, _ = zm.shape
    const = lambda shp: pl.BlockSpec(shp, lambda b, i: (0,) * len(shp))
    per_b = pl.BlockSpec((None, 1, S5_N), lambda b, i: (b, 0, 0))
    outs = [jax.ShapeDtypeStruct((bsz, seq, W_GRP), MXU_DT),
            jax.ShapeDtypeStruct((bsz, 1, S5_N), F32),
            jax.ShapeDtypeStruct((bsz, 1, S5_N), F32)]
    return pl.pallas_call(
        _s5_body,
        grid=(bsz, seq // tl),
        in_specs=[pl.BlockSpec((None, tl, W_GRP), lambda b, i: (b, i, 5)),
                  const((W_GRP, 2 * S5_N)), const((S5_N, W_GRP)), const((S5_N, W_GRP)),
                  const((1, S5_N)), const((1, S5_N)), const((1, W_GRP)),
                  const((W_GRP, 2 * W_GRP)), const((1, 2 * W_GRP)), const((1, W_GRP)),
                  per_b, per_b],
        out_specs=[pl.BlockSpec((None, tl, W_GRP), lambda b, i: (b, i, 0)), per_b, per_b],
        out_shape=outs,
        scratch_shapes=[pltpu.VMEM((tl, S5_N), F32), pltpu.VMEM((tl, S5_N), F32),
                        pltpu.VMEM((SEGS, S5_N), F32), pltpu.VMEM((SEGS, S5_N), F32),
                        pltpu.VMEM((1, S5_N), F32), pltpu.VMEM((1, S5_N), F32),
                        pltpu.VMEM((tl, S5_N), F32), pltpu.VMEM((tl, S5_N), F32),
                        pltpu.VMEM((W_GRP // LANES, tl, LANES), F32)],
        compiler_params=_cparams(("arbitrary", "arbitrary")),
        name="s5_mixer",
    )(zm, lp['s5_bb'], lp['s5_cre'], lp['s5_cim'], lp['s5_abr'], lp['s5_abi'], lp['s5_d'],
      lp['s5_glu_w'], lp['s5_glu_b'], lp['on_c'], h0r, h0i)


CODE_MASKED = -1
CODE_ZERO = -2


def _bias_body(rb_ref, code_ref, o_ref):
    code = code_ref[...]
    acc = jnp.where(code == CODE_MASKED, NEG, 0.0).astype(F32)
    for c in range(REL_BUCKETS * ATT_HEADS):
        acc = jnp.where(code == c, rb_ref[c], acc)
    o_ref[...] = acc


def _bias_table(rel_bias, codes):
    rows, cols = codes.shape
    tr = max(t for t in range(8, 513, 8) if rows % t == 0)
    return pl.pallas_call(
        _bias_body,
        grid_spec=pltpu.PrefetchScalarGridSpec(
            num_scalar_prefetch=1,
            grid=(rows // tr,),
            in_specs=[pl.BlockSpec((tr, cols), lambda i, rb: (i, 0))],
            out_specs=pl.BlockSpec((tr, cols), lambda i, rb: (i, 0))),
        out_shape=jax.ShapeDtypeStruct((rows, cols), F32),
        compiler_params=_cparams(("arbitrary",)),
        name="bias_table",
    )(rel_bias.reshape(-1), jnp.asarray(codes))


def _t5_bucket(n):
    n = np.asarray(n).astype(np.int32)
    max_exact = REL_BUCKETS // 2
    nf = np.maximum(n, 1).astype(np.float32)
    large = max_exact + (np.log(nf / max_exact) / np.log(REL_MAX_DIST / max_exact)
                         * (REL_BUCKETS - max_exact)).astype(np.int32)
    large = np.minimum(large, REL_BUCKETS - 1)
    return np.where(n < max_exact, n, large).astype(np.int32)


def _prompt_bias_codes():
    i = np.arange(Q_BLK)[:, None]
    j = np.arange(2 * Q_BLK)[None, :]
    rel = Q_BLK + i - j
    out = np.zeros((len(PATTERNS), 2, ATT_HEADS, Q_BLK, 2 * Q_BLK), np.int32)
    for p, (win, dil) in enumerate(PATTERNS):
        span = win // dil
        valid = (rel >= 0) & (rel <= span)
        bucket = _t5_bucket(np.clip(rel, 0, None) * dil)
        for var in range(2):
            v = valid & ((j >= Q_BLK) | (var == 1))
            for h in range(ATT_HEADS):
                out[p, var, h] = np.where(v, bucket * ATT_HEADS + h, CODE_MASKED)
    return out.reshape(-1, 2 * Q_BLK)


def _sample_bias_codes(dec_seq, wb):
    def tables(n_keys, first_pos):
        codes = np.full((ATT_HEADS, dec_seq, n_keys), CODE_MASKED, np.int32)
        logm = np.zeros((ATT_HEADS, dec_seq, n_keys), np.float32)
        for s in range(dec_seq):
            dist = (wb + s) - (first_pos + np.arange(n_keys))
            mult = np.zeros(n_keys, np.int32)
            for win, dil in PATTERNS:
                mult += ((dist >= 0) & (dist % dil == 0) & (dist // dil <= win // dil)).astype(np.int32)
            bucket = _t5_bucket(np.clip(dist, 0, None))
            for h in range(ATT_HEADS):
                codes[h, s] = np.where(mult > 0, bucket * ATT_HEADS + h, CODE_MASKED)
                logm[h, s] = np.log(np.maximum(mult, 1))
        return codes.reshape(ATT_HEADS * dec_seq, n_keys), logm.reshape(ATT_HEADS * dec_seq, n_keys)
    return tables(wb, 0), tables(dec_seq, wb)


def _attn_p_body(q_ref, kp_ref, kc_ref, vp_ref, vc_ref, bias_ref, o_ref, lse_ref):
    lane = lax.broadcasted_iota(jnp.int32, (1, W_GRP), 1)
    head_of_lane = [(lane >= h * HEAD_DIM) & (lane < (h + 1) * HEAD_DIM) for h in range(ATT_HEADS)]
    n_sub = q_ref.shape[0] // Q_BLK
    first_variant = jnp.minimum(pl.program_id(2), 1)
    for sub in range(n_sub):
        rows = slice(sub * Q_BLK, (sub + 1) * Q_BLK)
        q = q_ref[rows, :]
        if sub == 0:
            k_prev, v_prev, bias = kp_ref[...], vp_ref[...], bias_ref[first_variant]
        else:
            prev_rows = slice((sub - 1) * Q_BLK, sub * Q_BLK)
            k_prev, v_prev, bias = kc_ref[prev_rows, :], vc_ref[prev_rows, :], bias_ref[1]
        k2 = jnp.concatenate([k_prev, kc_ref[rows, :]], axis=0)
        v2 = jnp.concatenate([v_prev, vc_ref[rows, :]], axis=0)
        qs = jnp.concatenate([jnp.where(hm, q, jnp.zeros_like(q)) for hm in head_of_lane], axis=0)
        s = lax.dot_general(qs, k2, (((1,), (1,)), ((), ())), preferred_element_type=F32) + bias
        m = jnp.max(s, axis=-1, keepdims=True)
        pr = jnp.exp(s - m)
        den = jnp.sum(pr, axis=-1, keepdims=True)
        pv = jnp.dot(pr.astype(MXU_DT), v2, preferred_element_type=F32) / den
        lse_rows = m + jnp.log(den)
        o = jnp.zeros((Q_BLK, W_GRP), F32)
        lse = jnp.zeros((Q_BLK, W_GRP), F32)
        for h, hm in enumerate(head_of_lane):
            o = jnp.where(hm, pv[h * Q_BLK:(h + 1) * Q_BLK], o)
            lse = jnp.where(hm, lse_rows[h * Q_BLK:(h + 1) * Q_BLK], lse)
        o_ref[rows, :] = o
        lse_ref[rows, :] = lse


def _attn_prompt_pattern(q, k, v, bias):
    bsz, dil, md, _ = q.shape
    n_sub = math.gcd(ATTN_SUB_BLOCKS, md // Q_BLK)
    cur = pl.BlockSpec((None, None, n_sub * Q_BLK, W_GRP), lambda b, r, n: (b, r, n, 0))
    prv = pl.BlockSpec((None, None, Q_BLK, W_GRP), lambda b, r, n: (b, r, jnp.maximum(n * n_sub - 1, 0), 0))
    bsp = pl.BlockSpec((2, ATT_HEADS * Q_BLK, 2 * Q_BLK), lambda b, r, n: (0, 0, 0))
    return pl.pallas_call(
        _attn_p_body,
        grid=(bsz, dil, md // (n_sub * Q_BLK)),
        in_specs=[cur, prv, cur, prv, cur, bsp],
        out_specs=[cur, cur],
        out_shape=[jax.ShapeDtypeStruct((bsz, dil, md, W_GRP), F32)] * 2,
        compiler_params=_cparams(("arbitrary", "arbitrary", "arbitrary")),
        name=f"attn_prompt_d{dil}",
    )(q, k, k, v, v, bias)


def _attn_merge_body(*refs, dils):
    n_pat = len(dils)
    on_ref, out_ref, stage_ref = refs[2 * n_pat:]
    tm = out_ref.shape[0]
    halves = W_GRP // LANES
    vals = []
    for t in range(2 * n_pat):
        dil = dils[t // 2]
        if dil == 1:
            vals.append(refs[t][...])
            continue
        for r in range(dil):
            blk = refs[t][r]
            for half in range(halves):
                stage_ref.at[t, half][pl.ds(r, tm // dil, stride=dil), :] = blk[:, half * LANES:(half + 1) * LANES]
        vals.append(jnp.concatenate([stage_ref[t, half] for half in range(halves)], axis=1))
    os_, ls_ = vals[0::2], vals[1::2]
    mx = functools.reduce(jnp.maximum, ls_)
    ws = [jnp.exp(l_ - mx) for l_ in ls_]
    num = functools.reduce(lambda a_, b_: a_ + b_, [w_ * o_ for w_, o_ in zip(ws, os_)])
    merged = num / functools.reduce(lambda a_, b_: a_ + b_, ws)
    out_ref[...] = (_rms_rows(merged) * on_ref[...]).astype(out_ref.dtype)


def _attn_merge(pattern_outs, on_d, *, tm):
    dils = tuple(o.shape[1] for o, _ in pattern_outs)
    bsz, _, seq, _ = pattern_outs[0][0].shape
    seq = seq * dils[0]
    specs, args = [], []
    for (o, lse), dil in zip(pattern_outs, dils):
        if dil == 1:
            sp = pl.BlockSpec((None, None, tm, W_GRP), lambda b, j: (b, 0, j, 0))
        else:
            sp = pl.BlockSpec((None, dil, tm // dil, W_GRP), lambda b, j: (b, 0, j, 0))
        specs += [sp, sp]
        args += [o, lse]
    return pl.pallas_call(
        functools.partial(_attn_merge_body, dils=dils),
        grid=(bsz, seq // tm),
        in_specs=specs + [pl.BlockSpec((1, W_GRP), lambda b, j: (0, 0))],
        out_specs=pl.BlockSpec((None, tm, W_GRP), lambda b, j: (b, j, 0)),
        out_shape=jax.ShapeDtypeStruct((bsz, seq, W_GRP), MXU_DT),
        scratch_shapes=[pltpu.VMEM((2 * len(dils), W_GRP // LANES, tm, LANES), F32)],
        compiler_params=_cparams(("arbitrary", "arbitrary")),
        name="attn_merge",
    )(*args, on_d)


def _attn_prompt(qkv_by_dil, bias_all, on_d, *, tm):
    outs = [_attn_prompt_pattern(q, k, v, bias_all[p]) for p, (q, k, v) in enumerate(qkv_by_dil)]
    return _attn_merge(outs, on_d, tm=tm)


def _attn_s_body(q_ref, kn_ref, vn_ref, kt_ref, vt_ref, lw_ref, lwn_ref, on_ref, o_ref):
    q = q_ref[...]
    kn = kn_ref[...].astype(MXU_DT)
    vn = vn_ref[...].astype(MXU_DT)
    nt = (((1,), (1,)), ((), ()))
    outs = []
    for h in range(ATT_HEADS):
        cols = slice(h * HEAD_DIM, (h + 1) * HEAD_DIM)
        qh = q[:, cols]
        sc = jnp.dot(qh, kt_ref[h].astype(MXU_DT), preferred_element_type=F32) + lw_ref[h]
        scn = lax.dot_general(qh, kn[:, cols], nt, preferred_element_type=F32) + lwn_ref[h]
        m = jnp.maximum(jnp.max(sc, axis=-1, keepdims=True), jnp.max(scn, axis=-1, keepdims=True))
        p = jnp.exp(sc - m)
        pn = jnp.exp(scn - m)
        den = jnp.sum(p, axis=-1, keepdims=True) + jnp.sum(pn, axis=-1, keepdims=True)
        num = (lax.dot_general(p.astype(MXU_DT), vt_ref[h].astype(MXU_DT), nt, preferred_element_type=F32)
               + jnp.dot(pn.astype(MXU_DT), vn[:, cols], preferred_element_type=F32))
        outs.append(num / den)
    merged = jnp.concatenate(outs, axis=1)
    o_ref[...] = (_rms_rows(merged) * on_ref[...]).astype(o_ref.dtype)


def _attn_sample(q, kn, vn, cache_k, cache_v, l, lw, lwn, on_d):
    bsz, dec, _ = q.shape
    wb = cache_k.shape[2]
    per_b = pl.BlockSpec((None, dec, W_GRP), lambda b: (b, 0, 0))
    pos_minor = lambda c: jnp.transpose(c, (0, 1, 3, 4, 2))
    cache_spec = pl.BlockSpec((None, None, ATT_HEADS, HEAD_DIM, wb), lambda b: (l, b, 0, 0, 0))
    full = lambda a: pl.BlockSpec(a.shape, lambda b: (0,) * a.ndim)
    return pl.pallas_call(
        _attn_s_body,
        grid=(bsz,),
        in_specs=[per_b, per_b, per_b, cache_spec, cache_spec, full(lw), full(lwn), full(on_d)],
        out_specs=per_b,
        out_shape=jax.ShapeDtypeStruct((bsz, dec, W_GRP), MXU_DT),
        compiler_params=_cparams(("arbitrary",)),
        name="attn_sample",
    )(q, kn, vn, pos_minor(cache_k), pos_minor(cache_v), lw, lwn, on_d)


ROUTER_LANES = 128
HALF_D = D_MODEL // 2
ROW_TILE = HALF_D // LANES
ROW_DT = jnp.uint32


def _store_row_tiles(ref, val):
    n = val.shape[0]
    bits = lambda t: lax.bitcast_convert_type(t.astype(jnp.bfloat16).astype(F32), ROW_DT)
    packed = (bits(val[:, HALF_D:]) & jnp.uint32(0xFFFF0000)) | (bits(val[:, :HALF_D]) >> 16)
    for j in range(ROW_TILE):
        ref[pl.ds(j, n, stride=ROW_TILE), :] = packed[:, j * LANES:(j + 1) * LANES]


def _load_row_tiles(ref):
    n = ref.shape[0] // ROW_TILE
    packed = jnp.concatenate([ref[pl.ds(j, n, stride=ROW_TILE), :] for j in range(ROW_TILE)], axis=1)
    return (lax.bitcast_convert_type(packed << 16, F32),
            lax.bitcast_convert_type(packed & jnp.uint32(0xFFFF0000), F32))


OUT_TILE = D_MODEL // LANES


def _store_out_tiles(ref, val):
    n = val.shape[0]
    for j in range(OUT_TILE):
        ref[pl.ds(j, n, stride=OUT_TILE), :] = val[:, j * LANES:(j + 1) * LANES]


def _load_out_tiles(ref):
    n = ref.shape[0] // OUT_TILE
    return jnp.concatenate([ref[pl.ds(j, n, stride=OUT_TILE), :] for j in range(OUT_TILE)], axis=1)


def _post_mix_body(x_ref, a_ref, b_ref, c_ref, d_ref, w_ref, g1_ref, sc_ref, sh_ref, g_ref,
                   wr_ref, br_ref, tri_ref, cnt_in_ref,
                   x1_ref, h2_ref, eid_ref, gate_ref, rank_ref, cnt_ref,
                   run_ref):
    i = pl.program_id(0)

    @pl.when(i == 0)
    def _():
        run_ref[...] = cnt_in_ref[...]

    mix = jnp.concatenate([a_ref[...], b_ref[...], c_ref[...], d_ref[...]], axis=1)
    y = jnp.dot(mix, w_ref[...], preferred_element_type=F32)
    x1 = x_ref[...] + g1_ref[...] * y
    x1_ref[...] = x1
    h2 = _rms_rows(x1) * g_ref[...]
    h2 = h2 * (1.0 + sc_ref[...]) + sh_ref[...]
    _store_row_tiles(h2_ref, h2)
    logits = jnp.dot(h2.astype(MXU_DT), wr_ref[...], preferred_element_type=F32) + br_ref[...]
    tm = logits.shape[0]
    lane = lax.broadcasted_iota(jnp.int32, (tm, ROUTER_LANES), 1)
    big = jnp.int32(10 ** 6)
    is_g = lane < N_GROUPS
    gl = jnp.where(is_g, logits, -jnp.inf)
    gmax = jnp.max(gl, axis=-1, keepdims=True)
    gsel = jnp.min(jnp.where(gl == gmax, lane, big), axis=-1, keepdims=True)
    gprob = 1.0 / jnp.sum(jnp.where(is_g, jnp.exp(logits - gmax), 0.0), axis=-1, keepdims=True)
    lo_lane = N_GROUPS + gsel * EXP_PER_GROUP
    in_grp = (lane >= lo_lane) & (lane < lo_lane + EXP_PER_GROUP)
    el = jnp.where(in_grp, logits, -jnp.inf)
    v1 = jnp.max(el, axis=-1, keepdims=True)
    i1 = jnp.min(jnp.where(el == v1, lane, big), axis=-1, keepdims=True)
    el2 = jnp.where(lane == i1, -jnp.inf, el)
    v2 = jnp.max(el2, axis=-1, keepdims=True)
    i2 = jnp.min(jnp.where(el2 == v2, lane, big), axis=-1, keepdims=True)
    e2w = jnp.exp(v2 - v1)
    gate1 = (1.0 / (1.0 + e2w)) * gprob
    gate2 = (e2w / (1.0 + e2w)) * gprob
    e1 = i1 - N_GROUPS
    e2 = i2 - N_GROUPS
    oh1 = lane == e1
    oh2 = lane == e2
    both = jnp.where(oh1 | oh2, 1.0, 0.0)
    before = jnp.dot(tri_ref[...], both.astype(jnp.bfloat16), preferred_element_type=F32) + run_ref[...]
    r1 = jnp.sum(jnp.where(oh1, before, 0.0), axis=-1, keepdims=True).astype(jnp.int32)
    r2 = jnp.sum(jnp.where(oh2, before, 0.0), axis=-1, keepdims=True).astype(jnp.int32)
    run = run_ref[...] + jnp.sum(both, axis=0, keepdims=True)
    run_ref[...] = run
    cnt_ref[...] = run
    eid_ref[...] = jnp.where(lane == 0, e1, jnp.where(lane == 1, e2, 0))
    gate_ref[...] = jnp.where(lane == 0, gate1, jnp.where(lane == 1, gate2, 0.0))
    rank_ref[...] = jnp.where(lane == 0, r1, jnp.where(lane == 1, r2, 0))


def _post_mix(x, pieces, mod, lp, cnt_in, *, tm, rows_per_mod):
    t = x.shape[0]
    row = lambda i: (i, 0)
    const = lambda i: (0, 0)
    piece = pl.BlockSpec((tm, W_GRP), row)
    wide = pl.BlockSpec((tm, D_MODEL), row)
    lanes = pl.BlockSpec((tm, ROUTER_LANES), row)
    outs = [jax.ShapeDtypeStruct((t, D_MODEL), F32), jax.ShapeDtypeStruct((t * ROW_TILE, LANES), ROW_DT),
            jax.ShapeDtypeStruct((t, ROUTER_LANES), jnp.int32), jax.ShapeDtypeStruct((t, ROUTER_LANES), F32),
            jax.ShapeDtypeStruct((t, ROUTER_LANES), jnp.int32), jax.ShapeDtypeStruct((1, ROUTER_LANES), F32)]
    return pl.pallas_call(
        _post_mix_body,
        grid=(t // tm,),
        in_specs=[wide, piece, piece, piece, piece,
                  pl.BlockSpec((D_MODEL, D_MODEL), const),
                  _mod_spec(mod, 2, tm, rows_per_mod), _mod_spec(mod, 4, tm, rows_per_mod),
                  _mod_spec(mod, 3, tm, rows_per_mod),
                  pl.BlockSpec((1, D_MODEL), const),
                  pl.BlockSpec((D_MODEL, ROUTER_LANES), const), pl.BlockSpec((1, ROUTER_LANES), const),
                  pl.BlockSpec((tm, tm), const), pl.BlockSpec((1, ROUTER_LANES), const)],
        out_specs=[wide, pl.BlockSpec((tm * ROW_TILE, LANES), row), lanes, lanes, lanes,
                   pl.BlockSpec((1, ROUTER_LANES), const)],
        out_shape=outs,
        scratch_shapes=[pltpu.VMEM((1, ROUTER_LANES), F32)],
        compiler_params=_cparams(("arbitrary",)),
        name="post_mix",
    )(x, *pieces, lp['w_out'], mod, mod, mod, lp['norm_ffn'], lp['router_w'], lp['router_b'], lp['tri'], cnt_in)


def _row_copy(src_ref, s, dst_ref, d, sem, tile=ROW_TILE):
    return pltpu.make_async_copy(src_ref.at[pl.ds(pl.multiple_of(s * tile, tile), tile)],
                                 dst_ref.at[pl.ds(pl.multiple_of(d * tile, tile), tile)], sem)


def _dispatch_body(dest_ref, h_ref, xs_in_ref, xs_ref, sem):
    del xs_in_ref
    tm = h_ref.shape[0] // ROW_TILE
    base = pl.program_id(0) * (2 * tm)

    def issue(t, c):
        _row_copy(h_ref, t, xs_ref, dest_ref[base + 2 * t], sem).start()
        _row_copy(h_ref, t, xs_ref, dest_ref[base + 2 * t + 1], sem).start()
        return c

    lax.fori_loop(0, tm, issue, 0)
    for _ in range(2):
        pltpu.make_async_copy(h_ref, xs_ref.at[pl.ds(0, tm * ROW_TILE)], sem).wait()


def _dispatch(h2, dest_flat, xs, *, tm):
    t = h2.shape[0] // ROW_TILE
    return pl.pallas_call(
        _dispatch_body,
        grid_spec=pltpu.PrefetchScalarGridSpec(
            num_scalar_prefetch=1,
            grid=(t // tm,),
            in_specs=[pl.BlockSpec((tm * ROW_TILE, LANES), lambda i, d: (i, 0)),
                      pl.BlockSpec(memory_space=pl.ANY)],
            out_specs=pl.BlockSpec(memory_space=pl.ANY),
            scratch_shapes=[pltpu.SemaphoreType.DMA(())]),
        out_shape=jax.ShapeDtypeStruct(xs.shape, xs.dtype),
        input_output_aliases={2: 0},
        compiler_params=_cparams(("arbitrary",)),
        name="moe_dispatch",
    )(dest_flat, h2, xs)


def _experts_body(blk_e_ref, n_used_ref, xs_ref, wg_ref, wu_ref, wd_ref, o_ref, wgb_ref, wub_ref, wdb_ref):
    i = pl.program_id(0)
    e = blk_e_ref[i]
    e_prev = blk_e_ref[jnp.maximum(i - 1, 0)]

    @pl.when((i == 0) | (e != e_prev))
    def _():
        wgb_ref[...] = wg_ref[...].astype(MXU_DT)
        wub_ref[...] = wu_ref[...].astype(MXU_DT)
        wdb_ref[...] = wd_ref[...].astype(MXU_DT)

    @pl.when(i < n_used_ref[0])
    def _():
        x_lo, x_hi = [t.astype(MXU_DT) for t in _load_row_tiles(xs_ref)]
        g = (jnp.dot(x_lo, wgb_ref[:HALF_D, :], preferred_element_type=F32)
             + jnp.dot(x_hi, wgb_ref[HALF_D:, :], preferred_element_type=F32))
        u = (jnp.dot(x_lo, wub_ref[:HALF_D, :], preferred_element_type=F32)
             + jnp.dot(x_hi, wub_ref[HALF_D:, :], preferred_element_type=F32))
        hmid = (g * _sigmoid(g)) * u
        _store_out_tiles(o_ref, jnp.dot(hmid.astype(MXU_DT), wdb_ref[...], preferred_element_type=F32))

    @pl.when(i >= n_used_ref[0])
    def _():
        o_ref[...] = jnp.zeros_like(o_ref)


def _experts(xs, blk_e, n_used, wg, wu, wd, l):
    n_blk = xs.shape[0] // (MOE_ROWS * ROW_TILE)
    xmap = lambda i, be, nu: (jnp.minimum(i, nu[0] - 1), 0)
    return pl.pallas_call(
        _experts_body,
        grid_spec=pltpu.PrefetchScalarGridSpec(
            num_scalar_prefetch=2,
            grid=(n_blk,),
            in_specs=[pl.BlockSpec((MOE_ROWS * ROW_TILE, LANES), xmap),
                      pl.BlockSpec((None, None, D_MODEL, D_EXPERT), lambda i, be, nu: (l, be[i], 0, 0)),
                      pl.BlockSpec((None, None, D_MODEL, D_EXPERT), lambda i, be, nu: (l, be[i], 0, 0)),
                      pl.BlockSpec((None, None, D_EXPERT, D_MODEL), lambda i, be, nu: (l, be[i], 0, 0))],
            out_specs=pl.BlockSpec((MOE_ROWS * OUT_TILE, LANES), lambda i, be, nu: (i, 0)),
            scratch_shapes=[pltpu.VMEM((D_MODEL, D_EXPERT), MXU_DT), pltpu.VMEM((D_MODEL, D_EXPERT), MXU_DT),
                            pltpu.VMEM((D_EXPERT, D_MODEL), MXU_DT)]),
        out_shape=jax.ShapeDtypeStruct((n_blk * MOE_ROWS * OUT_TILE, LANES), F32),
        compiler_params=_cparams(("arbitrary",)),
        name="moe_experts",
    )(blk_e, n_used, xs, wg, wu, wd)


def _combine_body(dest_ref, ys_ref, x1_ref, gate_ref, g2_ref, x2_ref, buf0_ref, buf1_ref, sem):
    tm = x1_ref.shape[0]
    base = pl.program_id(0) * (2 * tm)

    def issue(t, c):
        _row_copy(ys_ref, dest_ref[base + 2 * t], buf0_ref, t, sem, OUT_TILE).start()
        _row_copy(ys_ref, dest_ref[base + 2 * t + 1], buf1_ref, t, sem, OUT_TILE).start()
        return c

    lax.fori_loop(0, tm, issue, 0)
    for buf_ref in (buf0_ref, buf1_ref):
        pltpu.make_async_copy(ys_ref.at[pl.ds(0, tm * OUT_TILE)], buf_ref, sem).wait()
    gate = gate_ref[...]
    y = _load_out_tiles(buf0_ref) * gate[:, 0:1] + _load_out_tiles(buf1_ref) * gate[:, 1:2]
    x2_ref[...] = x1_ref[...] + g2_ref[...] * y


def _combine(ys, dest_flat, x1, gate, mod, *, tm, rows_per_mod):
    t = x1.shape[0]
    if mod.ndim == 4:
        g2_spec = pl.BlockSpec((None, None, 1, D_MODEL), lambda i, d: (5, (i * tm) // rows_per_mod, 0, 0))
    else:
        g2_spec = pl.BlockSpec((None, tm, D_MODEL), lambda i, d: (5, i, 0))
    return pl.pallas_call(
        _combine_body,
        grid_spec=pltpu.PrefetchScalarGridSpec(
            num_scalar_prefetch=1,
            grid=(t // tm,),
            in_specs=[pl.BlockSpec(memory_space=pl.ANY),
                      pl.BlockSpec((tm, D_MODEL), lambda i, d: (i, 0)),
                      pl.BlockSpec((tm, ROUTER_LANES), lambda i, d: (i, 0)),
                      g2_spec],
            out_specs=pl.BlockSpec((tm, D_MODEL), lambda i, d: (i, 0)),
            scratch_shapes=[pltpu.VMEM((tm * OUT_TILE, LANES), F32), pltpu.VMEM((tm * OUT_TILE, LANES), F32),
                            pltpu.SemaphoreType.DMA(())]),
        out_shape=jax.ShapeDtypeStruct((t, D_MODEL), F32),
        compiler_params=_cparams(("arbitrary",)),
        name="moe_combine",
    )(dest_flat, ys, x1, gate, mod)


def _routing_tables(cnt, eids, ranks, n_blk):
    counts = cnt[0, :N_EXPERTS].astype(jnp.int32)
    padded = (counts + MOE_ROWS - 1) // MOE_ROWS * MOE_ROWS
    ends = jnp.cumsum(padded)
    starts = ends - padded
    experts = jnp.arange(N_EXPERTS, dtype=jnp.int32)
    start_of = lambda e: jnp.sum(jnp.where(e[..., None] == experts, starts, 0), axis=-1)
    dests = [(start_of(e[:, :2]) + r[:, :2]).reshape(-1) for e, r in zip(eids, ranks)]
    blk_start = jnp.arange(n_blk, dtype=jnp.int32) * MOE_ROWS
    blk_e = jnp.minimum(jnp.sum((ends[None, :] <= blk_start[:, None]).astype(jnp.int32), axis=1), N_EXPERTS - 1)
    n_used = (ends[-1] // MOE_ROWS).astype(jnp.int32).reshape(1)
    return dests, blk_e, n_used


def _block_diag(w):
    g, r, c = w.shape
    eye = jnp.eye(g, dtype=w.dtype)
    return (eye[:, None, :, None] * w[:, :, None, :]).reshape(g * r, g * c)


def _s5_discretise(log_dt, a_re, a_im, b_re, b_im):
    step = jnp.exp(log_dt)[:, None]
    mag = jnp.exp(a_re * step)
    ang = a_im * step
    abr = mag * jnp.cos(ang)
    abi = mag * jnp.sin(ang)
    den = a_re * a_re + a_im * a_im
    zr = ((abr - 1.0) * a_re + abi * a_im) / den
    zi = (abi * a_re - (abr - 1.0) * a_im) / den
    bbr = zr[..., None] * b_re - zi[..., None] * b_im
    bbi = zr[..., None] * b_im + zi[..., None] * b_re
    return abr, abi, bbr, bbi


def _prep_layer(P, l):
    row = lambda a: a.reshape(1, -1)
    abr, abi, bbr, bbi = _s5_discretise(P['s5_log_dt'][l], P['s5_a_re'][l], P['s5_a_im'][l],
                                        P['s5_b_re'][l], P['s5_b_im'][l])
    on = P['out_norm'][l]
    bones = _block_diag(jnp.full((ATT_HEADS, HEAD_DIM, HEAD_DIM), 1.0 / HEAD_DIM, F32)).astype(jnp.bfloat16)
    return {
        'norm_mix': row(P['norm_mix'][l]), 'norm_ffn': row(P['norm_ffn'][l]),
        'w_in': P['w_in'][l].astype(MXU_DT), 'w_out': P['w_out'][l].astype(MXU_DT),
        'bones': bones,
        'qg': row(jnp.tile(P['q_norm'][l], ATT_HEADS)), 'kg': row(jnp.tile(P['k_norm'][l], ATT_HEADS)),
        'lru_conv_w': P['lru_conv_w'][l], 'lru_conv_b': row(P['lru_conv_b'][l]),
        'lru_wa_blk': _block_diag(P['lru_wa'][l]).astype(MXU_DT), 'lru_ba': row(P['lru_ba'][l]),
        'lru_wx_blk': _block_diag(P['lru_wx'][l]).astype(MXU_DT), 'lru_bx': row(P['lru_bx'][l]),
        'lru_lambda': row(P['lru_lambda'][l]), 'sconv_w': P['sconv_w'][l],
        'on_a': row(on[0:W_GRP]), 'on_b': row(on[W_GRP:2 * W_GRP]),
        'on_c': row(on[2 * W_GRP:3 * W_GRP]), 'on_d': row(on[3 * W_GRP:]),
        's5_bb': jnp.concatenate([_block_diag(bbr.transpose(0, 2, 1)), _block_diag(bbi.transpose(0, 2, 1))],
                                 axis=1).astype(MXU_DT),
        's5_cre': _block_diag(P['s5_c_re'][l].transpose(0, 2, 1)).astype(MXU_DT),
        's5_cim': _block_diag(P['s5_c_im'][l].transpose(0, 2, 1)).astype(MXU_DT),
        's5_abr': row(abr), 's5_abi': row(abi), 's5_d': row(P['s5_d'][l]),
        's5_glu_w': P['s5_glu_w'][l].astype(MXU_DT), 's5_glu_b': row(P['s5_glu_b'][l]),
        'router_w': jnp.zeros((D_MODEL, ROUTER_LANES), F32)
                       .at[:, :N_GROUPS].set(P['router_g_w'][l])
                       .at[:, N_GROUPS:N_GROUPS + N_EXPERTS].set(P['router_e_w'][l]).astype(MXU_DT),
        'router_b': jnp.zeros((1, ROUTER_LANES), F32)
                       .at[0, :N_GROUPS].set(P['router_g_b'][l])
                       .at[0, N_GROUPS:N_GROUPS + N_EXPERTS].set(P['router_e_b'][l]),
    }


TOKEN_TILE = 512
SEQ_TILE = 256

_PARAM_NAMES = ('rel_bias', 'mod_w', 'mod_b', 'norm_mix', 'norm_ffn', 'w_in', 'lru_conv_w', 'lru_conv_b',
                'lru_wa', 'lru_ba', 'lru_wx', 'lru_bx', 'lru_lambda', 'sconv_w', 's5_log_dt', 's5_a_re',
                's5_a_im', 's5_b_re', 's5_b_im', 's5_c_re', 's5_c_im', 's5_d', 's5_glu_w', 's5_glu_b',
                'q_norm', 'k_norm', 'out_norm', 'w_out', 'router_g_w', 'router_g_b', 'router_e_w',
                'router_e_b', 'moe_w_gate', 'moe_w_up', 'moe_w_down')


def _mixers(x, mod, lp, st, attn_fn, *, batch, seq, tm, tl, rows_per_mod, dils=(), kv_window=None):
    res = _in_proj(x, mod, lp['norm_mix'], lp['w_in'], lp['bones'], lp['qg'], lp['kg'],
                   tm=tm, rows_per_mod=rows_per_mod, dils=dils, seq=seq, kv_window=kv_window)
    zm, q, kf, vf, kb, vb = res[:6]
    zm3 = zm.reshape(batch, seq, 6 * W_GRP)
    oa, ob, lru_h, lru_conv, sconv = _mixer_ab(zm3, lp, st['lru_h'], st['lru_conv'], st['sconv'], tl=tl)
    oc, s5_re, s5_im = _s5_mixer(zm3, lp, st['s5_re'], st['s5_im'], tl=tl)
    r3 = lambda t: t.reshape(batch, seq, W_GRP)
    if kv_window is None:
        kf, vf = r3(kf), r3(vf)
        win = lambda t: t.reshape(batch, seq, ATT_HEADS, HEAD_DIM)
    else:
        win = lambda t: jnp.transpose(t.reshape(batch, ATT_HEADS, HEAD_DIM, kv_window), (0, 3, 1, 2))
    od = attn_fn(r3(q), kf, vf, r3(kb), r3(vb), res[6:])
    flat = lambda t: t.reshape(batch * seq, W_GRP)
    new_st = {'lru_h': lru_h[:, 0], 'lru_conv': lru_conv, 'sconv': sconv,
              's5_re': s5_re.reshape(batch, S5_GROUPS, S5_STATE), 's5_im': s5_im.reshape(batch, S5_GROUPS, S5_STATE),
              'win_k': win(kf), 'win_v': win(vf)}
    return [flat(oa), flat(ob), flat(oc), flat(od)], new_st


def kernel(x_prompt, x_sample, c_prompt, c_sample, state_lru_h, state_lru_conv, state_sconv, state_s5_re, state_s5_im, cache_win_k, cache_win_v, rel_bias, mod_w, mod_b, norm_mix, norm_ffn, w_in, lru_conv_w, lru_conv_b, lru_wa, lru_ba, lru_wx, lru_bx, lru_lambda, sconv_w, s5_log_dt, s5_a_re, s5_a_im, s5_b_re, s5_b_im, s5_c_re, s5_c_im, s5_d, s5_glu_w, s5_glu_b, q_norm, k_norm, out_norm, w_out, router_g_w, router_g_b, router_e_w, router_e_b, moe_w_gate, moe_w_up, moe_w_down):
    P = dict(zip(_PARAM_NAMES, (rel_bias, mod_w, mod_b, norm_mix, norm_ffn, w_in, lru_conv_w, lru_conv_b,
                                lru_wa, lru_ba, lru_wx, lru_bx, lru_lambda, sconv_w, s5_log_dt, s5_a_re,
                                s5_a_im, s5_b_re, s5_b_im, s5_c_re, s5_c_im, s5_d, s5_glu_w, s5_glu_b,
                                q_norm, k_norm, out_norm, w_out, router_g_w, router_g_b, router_e_w,
                                router_e_b, moe_w_gate, moe_w_up, moe_w_down)))
    bp, seq, d = x_prompt.shape
    bs, dec, _ = x_sample.shape
    depth = mod_w.shape[0]
    tp, ts = bp * seq, bs * dec
    wb = cache_win_k.shape[2]
    wp = min(PATTERNS[-1][0], seq)
    tm_p = min(TOKEN_TILE, tp)
    tm_s = min(TOKEN_TILE, ts)
    tl_p = min(SEQ_TILE, seq)

    nc = -(-(bp + bs) // 8) * 8
    c_all = jnp.zeros((nc, d), F32).at[:bp].set(c_prompt).at[bp:bp + bs].set(c_sample)
    mod_all = _modulation(c_all, mod_w, mod_b)
    bias_p = _bias_table(rel_bias, _prompt_bias_codes()).reshape(len(PATTERNS), 2, ATT_HEADS * Q_BLK, 2 * Q_BLK)
    dils_p = tuple(dil for _, dil in PATTERNS if dil > 1)
    (lw_codes, lw_logm), (lwn_codes, lwn_logm) = _sample_bias_codes(dec, wb)
    lw_s = (_bias_table(rel_bias, lw_codes) + lw_logm).reshape(ATT_HEADS, dec, wb)
    lwn_s = (_bias_table(rel_bias, lwn_codes) + lwn_logm).reshape(ATT_HEADS, dec, dec)
    tri = jnp.asarray(np.tril(np.ones((TOKEN_TILE, TOKEN_TILE), np.float32), -1), jnp.bfloat16)
    n_blk = (2 * (tp + ts)) // MOE_ROWS + N_EXPERTS

    zero_st = {'lru_h': jnp.zeros((bp, 1, W_GRP), F32), 'lru_conv': jnp.zeros((bp, LRU_CONV - 1, W_GRP), F32),
               'sconv': jnp.zeros((bp, SCONV_W - 1, W_GRP), F32),
               's5_re': jnp.zeros((bp, 1, S5_N), F32), 's5_im': jnp.zeros((bp, 1, S5_N), F32)}
    names = ('lru_h', 'lru_conv', 'sconv', 's5_re', 's5_im', 'win_k', 'win_v')
    acc_p = {n: [] for n in names}
    acc_s = {n: [] for n in names}
    xp = x_prompt.reshape(tp, d)
    xs = x_sample.reshape(ts, d)
    for l in range(depth):
        lp = _prep_layer(P, l)
        lp['tri'] = tri
        m6 = mod_all[l].reshape(nc, 6, d).transpose(1, 0, 2)
        mod_p = m6[:, :bp].reshape(6, bp, 1, d)
        mod_s = jnp.repeat(m6[:, bp:bp + bs], dec, axis=1)
        attn_p = lambda q, kf, vf, kb, vb, ex: _attn_prompt(
            [(q[:, None], kb[:, None], vb[:, None])] + [tuple(ex[3 * t:3 * t + 3]) for t in range(len(dils_p))],
            bias_p, lp['on_d'], tm=tm_p)
        pieces_p, st_p = _mixers(xp, mod_p, lp, zero_st, attn_p, batch=bp, seq=seq, tm=tm_p, tl=tl_p,
                                 rows_per_mod=seq, dils=dils_p, kv_window=wp)
        cnt0 = jnp.zeros((1, ROUTER_LANES), F32)
        x1p, h2p, eid_p, gate_p, rank_p, cnt = _post_mix(xp, pieces_p, mod_p, lp, cnt0, tm=tm_p, rows_per_mod=seq)
        samp_st = {'lru_h': state_lru_h[l][:, None], 'lru_conv': state_lru_conv[l], 'sconv': state_sconv[l],
                   's5_re': state_s5_re[l].reshape(bs, 1, S5_N), 's5_im': state_s5_im[l].reshape(bs, 1, S5_N)}
        attn_s = lambda q, kf, vf, kb, vb, ex: _attn_sample(
            q, kf, vf, cache_win_k, cache_win_v, l, lw_s, lwn_s, lp['on_d'])
        pieces_s, st_s = _mixers(xs, mod_s, lp, samp_st, attn_s, batch=bs, seq=dec, tm=tm_s, tl=dec,
                                 rows_per_mod=dec)
        x1s, h2s, eid_s, gate_s, rank_s, cnt = _post_mix(xs, pieces_s, mod_s, lp, cnt, tm=tm_s, rows_per_mod=dec)
        dests, blk_e, n_used = _routing_tables(cnt, [eid_p, eid_s], [rank_p, rank_s], n_blk)
        slots = jnp.zeros((n_blk * MOE_ROWS * ROW_TILE, LANES), ROW_DT)
        slots = _dispatch(h2p, dests[0], slots, tm=tm_p)
        slots = _dispatch(h2s, dests[1], slots, tm=tm_s)
        ys = _experts(slots, blk_e, n_used, moe_w_gate, moe_w_up, moe_w_down, l)
        xp = _combine(ys, dests[0], x1p, gate_p, mod_p, tm=tm_p, rows_per_mod=seq)
        xs = _combine(ys, dests[1], x1s, gate_s, mod_s, tm=tm_s, rows_per_mod=dec)
        for n in names:
            acc_p[n].append(st_p[n])
            acc_s[n].append(st_s[n])
    new_p = {n: jnp.stack(acc_p[n], axis=0) for n in names}
    new_s = {n: jnp.stack(acc_s[n], axis=0) for n in names}
    return (xp.reshape(bp, seq, d), xs.reshape(bs, dec, d),
            new_p['lru_h'], new_p['lru_conv'], new_p['sconv'], new_p['s5_re'], new_p['s5_im'],
            new_p['win_k'], new_p['win_v'],
            new_s['lru_h'], new_s['lru_conv'], new_s['sconv'], new_s['s5_re'], new_s['s5_im'],
            new_s['win_k'], new_s['win_v'])
```

```python
import functools
import math

import numpy as np
import jax
import jax.numpy as jnp
from jax import lax
from jax.experimental import pallas as pl
from jax.experimental.pallas import tpu as pltpu

F32 = jnp.float32
MXU_DT = jnp.bfloat16
HIGHEST = lax.Precision.HIGHEST

D_MODEL = 1024
DEPTH = 4
W_GRP = 256
N_Z = 9
LRU_HEADS = 4
LRU_CONV = 4
LRU_C = 8.0
SCONV_W = 3
S5_CH = 16
S5_GROUPS = 16
S5_STATE = 64
S5_N = S5_GROUPS * S5_STATE
ATT_HEADS = 4
HEAD_DIM = 64
PATTERNS = ((128, 1), (512, 4), (2048, 16))
Q_BLK = 128
REL_BUCKETS = 32
REL_MAX_DIST = 2048
N_GROUPS = 4
EXP_PER_GROUP = 8
N_EXPERTS = 32
D_EXPERT = 512
EPS = 1e-6
NEG = -1e30

VMEM_LIMIT = 56 * 1024 * 1024
LANES = 128
MOE_ROWS = 512
SEGS = 8
ATTN_SUB_BLOCKS = 4


def _cparams(sem):
    return pltpu.CompilerParams(dimension_semantics=sem, vmem_limit_bytes=VMEM_LIMIT)


def _gelu(x):
    return 0.5 * x * (1.0 + jnp.tanh(math.sqrt(2.0 / math.pi) * (x + 0.044715 * (x * x * x))))


def _sigmoid(x):
    return 1.0 / (1.0 + jnp.exp(-x))


def _rms_rows(x):
    return x * lax.rsqrt(jnp.mean(x * x, axis=-1, keepdims=True) + EPS)


def _shift_rows(x, s, fill, row):
    return jnp.where(row >= s, pltpu.roll(x, s, 0), fill)


def _mod_body(c_ref, w_ref, b_ref, o_ref):
    c = c_ref[...]
    s = c * _sigmoid(c)
    o_ref[...] = jnp.dot(s.astype(MXU_DT), w_ref[...].astype(MXU_DT), preferred_element_type=F32) + b_ref[...]


def _modulation(c_all, mod_w, mod_b):
    nb = c_all.shape[0]
    depth = mod_w.shape[0]
    n_out = mod_w.shape[2]
    tn = D_MODEL
    return pl.pallas_call(
        _mod_body,
        grid=(depth, n_out // tn),
        in_specs=[pl.BlockSpec((nb, D_MODEL), lambda l, j: (0, 0)),
                  pl.BlockSpec((None, D_MODEL, tn), lambda l, j: (l, 0, j)),
                  pl.BlockSpec((None, 1, tn), lambda l, j: (l, 0, j))],
        out_specs=pl.BlockSpec((None, nb, tn), lambda l, j: (l, 0, j)),
        out_shape=jax.ShapeDtypeStruct((depth, nb, n_out), F32),
        compiler_params=_cparams(("arbitrary", "arbitrary")),
        name="modulation",
    )(c_all, mod_w, mod_b.reshape(depth, 1, n_out))


def _head_mean_sq(t, bones):
    sq = t * t
    hi = sq.astype(jnp.bfloat16)
    lo = (sq - hi.astype(F32)).astype(jnp.bfloat16)
    return (jnp.dot(hi, bones, preferred_element_type=F32)
            + jnp.dot(lo, bones, preferred_element_type=F32))


def _inproj_body(*refs, dils, kv_pos_minor):
    (x_ref, sc_ref, sh_ref, g_ref, w_ref, bones_ref, qg_ref, kg_ref,
     zm_ref, q_ref, kf_ref, vf_ref, kb_ref, vb_ref) = refs[:14]
    x = x_ref[...]
    h = _rms_rows(x) * g_ref[...]
    h = h * (1.0 + sc_ref[...]) + sh_ref[...]
    z = jnp.dot(h.astype(MXU_DT), w_ref[...], preferred_element_type=F32)
    nm = 6 * W_GRP
    zm_ref[...] = z[:, :nm]
    q = z[:, nm:nm + W_GRP]
    k = z[:, nm + W_GRP:nm + 2 * W_GRP]
    v = z[:, nm + 2 * W_GRP:]
    bones = bones_ref[...]
    qn = (q * lax.rsqrt(_head_mean_sq(q, bones) + EPS) * qg_ref[...]) * (HEAD_DIM ** -0.5)
    kn = k * lax.rsqrt(_head_mean_sq(k, bones) + EPS) * kg_ref[...]
    q_ref[...] = qn.astype(q_ref.dtype)
    if kv_pos_minor:
        kf_ref[...] = kn.T
        vf_ref[...] = v.T
    else:
        kf_ref[...] = kn
        vf_ref[...] = v
    kb_ref[...] = kn.astype(kb_ref.dtype)
    vb_ref[...] = v.astype(vb_ref.dtype)
    if dils:
        stage_ref = refs[-1]
        tm = x_ref.shape[0]
        for a, val in enumerate((qn, kn, v)):
            for half in range(W_GRP // LANES):
                stage_ref[a, half] = val[:, half * LANES:(half + 1) * LANES]
        for di, dil in enumerate(dils):
            for a in range(3):
                out_ref = refs[14 + 3 * di + a]
                for r in range(dil):
                    for half in range(W_GRP // LANES):
                        out_ref[r, :, half * LANES:(half + 1) * LANES] = (
                            stage_ref.at[a, half][pl.ds(r, tm // dil, stride=dil), :].astype(out_ref.dtype))


def _mod_spec(mod, k, tm, rows_per_mod):
    if mod.ndim == 4:
        return pl.BlockSpec((None, None, 1, D_MODEL), lambda i: (k, (i * tm) // rows_per_mod, 0, 0))
    return pl.BlockSpec((None, tm, D_MODEL), lambda i: (k, i, 0))


def _in_proj(x, mod, norm_g, w_in, bones, qg, kg, *, tm, rows_per_mod, dils=(), seq=None, kv_window=None):
    t = x.shape[0]
    n_in = w_in.shape[1]
    nm = 6 * W_GRP
    row = lambda i: (i, 0)
    const = lambda i: (0, 0)
    outs = [jax.ShapeDtypeStruct((t, nm), F32),
            jax.ShapeDtypeStruct((t, W_GRP), MXU_DT),
            jax.ShapeDtypeStruct((t, W_GRP), F32),
            jax.ShapeDtypeStruct((t, W_GRP), F32),
            jax.ShapeDtypeStruct((t, W_GRP), MXU_DT),
            jax.ShapeDtypeStruct((t, W_GRP), MXU_DT)]
    out_specs = [pl.BlockSpec((tm, nm), row)] + [pl.BlockSpec((tm, W_GRP), row)] * 5
    scratch = []
    if kv_window is not None:
        tiles_per_seq = seq // tm
        skip = (seq - kv_window) // tm
        assert (seq - kv_window) % tm == 0 and kv_window % tm == 0
        win = pl.BlockSpec((None, W_GRP, tm),
                           lambda i: (i // tiles_per_seq, 0, jnp.maximum(i % tiles_per_seq - skip, 0)))
        outs[2] = outs[3] = jax.ShapeDtypeStruct((t // seq, W_GRP, kv_window), F32)
        out_specs[2] = out_specs[3] = win
    if dils:
        tiles_per_seq = seq // tm
        for dil in dils:
            outs += [jax.ShapeDtypeStruct((t // seq, dil, seq // dil, W_GRP), MXU_DT)] * 3
            out_specs += [pl.BlockSpec((None, dil, tm // dil, W_GRP),
                                       lambda i: (i // tiles_per_seq, 0, i % tiles_per_seq, 0))] * 3
        scratch = [pltpu.VMEM((3, W_GRP // LANES, tm, LANES), F32)]
    return pl.pallas_call(
        functools.partial(_inproj_body, dils=tuple(dils), kv_pos_minor=kv_window is not None),
        grid=(t // tm,),
        in_specs=[pl.BlockSpec((tm, D_MODEL), row),
                  _mod_spec(mod, 1, tm, rows_per_mod),
                  _mod_spec(mod, 0, tm, rows_per_mod),
                  pl.BlockSpec((1, D_MODEL), const),
                  pl.BlockSpec((D_MODEL, n_in), const),
                  pl.BlockSpec((W_GRP, W_GRP), const),
                  pl.BlockSpec((1, W_GRP), const),
                  pl.BlockSpec((1, W_GRP), const)],
        out_specs=out_specs,
        out_shape=outs,
        scratch_shapes=scratch,
        compiler_params=_cparams(("arbitrary",)),
        name="in_proj",
    )(x, mod, mod, norm_g, w_in, bones, qg, kg)


def _softplus(x):
    return jnp.maximum(x, 0.0) + jnp.log(1.0 + jnp.exp(-jnp.abs(x)))


def _mixer_ab_body(xa_ref, ga_ref, gb_ref, gc_ref, xb_ref,
                   cw_ref, cb_ref, wa_ref, ba_ref, wx_ref, bx_ref, lam_ref, sw_ref, on_a_ref, on_b_ref,
                   h0_ref, conv0_ref, sconv0_ref,
                   oa_ref, ob_ref, hn_ref, convn_ref, sconvn_ref,
                   xe_ref, pe_ref, hc_ref):
    i = pl.program_id(1)
    tl = xa_ref.shape[0]

    @pl.when(i == 0)
    def _():
        xe_ref[8 - (LRU_CONV - 1):8, :] = conv0_ref[...]
        pe_ref[8 - (SCONV_W - 1):8, :] = sconv0_ref[...]
        hc_ref[...] = h0_ref[...]

    row = lax.broadcasted_iota(jnp.int32, (tl, 1), 0)
    xa = xa_ref[...]
    xe_ref[8:, :] = xa
    cw = cw_ref[...]
    xc = cw[LRU_CONV - 1:LRU_CONV, :] * xa
    for s in range(1, LRU_CONV):
        xc = xc + cw[LRU_CONV - 1 - s:LRU_CONV - s, :] * xe_ref[8 - s:8 - s + tl, :]
    xc = xc + cb_ref[...]
    convn_ref[...] = xa[tl - (LRU_CONV - 1):, :]
    xe_ref[0:8, :] = xa[tl - 8:, :]
    xcb = xc.astype(MXU_DT)
    r = _sigmoid(jnp.dot(xcb, wa_ref[...], preferred_element_type=F32) + ba_ref[...])
    ig = _sigmoid(jnp.dot(xcb, wx_ref[...], preferred_element_type=F32) + bx_ref[...])
    log_a = (-LRU_C * r) * _softplus(-lam_ref[...])
    a = jnp.exp(log_a)
    b = jnp.sqrt(-jnp.tanh(log_a) * (a * a + 1.0)) * (ig * xc)
    s = 1
    while s < tl:
        b = a * _shift_rows(b, s, 0.0, row) + b
        a = a * _shift_rows(a, s, 1.0, row)
        s *= 2
    h = b + a * hc_ref[...]
    hc_ref[...] = h[tl - 1:, :]
    hn_ref[...] = h[tl - 1:, :]
    out_a = h * _gelu(ga_ref[...])
    oa_ref[...] = (_rms_rows(out_a) * on_a_ref[...]).astype(oa_ref.dtype)
    p = gc_ref[...] * xb_ref[...]
    pe_ref[8:, :] = p
    sw = sw_ref[...]
    yb = sw[SCONV_W - 1:SCONV_W, :] * p
    for s in range(1, SCONV_W):
        yb = yb + sw[SCONV_W - 1 - s:SCONV_W - s, :] * pe_ref[8 - s:8 - s + tl, :]
    sconvn_ref[...] = p[tl - (SCONV_W - 1):, :]
    pe_ref[0:8, :] = p[tl - 8:, :]
    out_b = gb_ref[...] * yb
    ob_ref[...] = (_rms_rows(out_b) * on_b_ref[...]).astype(ob_ref.dtype)


def _mixer_ab(zm, lp, h0, conv0, sconv0, *, tl):
    bsz, seq, _ = zm.shape
    col = lambda c: pl.BlockSpec((None, tl, W_GRP), lambda b, i, c=c: (b, i, c))
    const = lambda shp: pl.BlockSpec(shp, lambda b, i: (0,) * len(shp))
    per_b = lambda n: pl.BlockSpec((None, n, W_GRP), lambda b, i: (b, 0, 0))
    outs = [jax.ShapeDtypeStruct((bsz, seq, W_GRP), MXU_DT),
            jax.ShapeDtypeStruct((bsz, seq, W_GRP), MXU_DT),
            jax.ShapeDtypeStruct((bsz, 1, W_GRP), F32),
            jax.ShapeDtypeStruct((bsz, LRU_CONV - 1, W_GRP), F32),
            jax.ShapeDtypeStruct((bsz, SCONV_W - 1, W_GRP), F32)]
    return pl.pallas_call(
        _mixer_ab_body,
        grid=(bsz, seq // tl),
        in_specs=[col(0), col(1), col(2), col(3), col(4),
                  const((LRU_CONV, W_GRP)), const((1, W_GRP)),
                  const((W_GRP, W_GRP)), const((1, W_GRP)),
                  const((W_GRP, W_GRP)), const((1, W_GRP)),
                  const((1, W_GRP)), const((SCONV_W, W_GRP)),
                  const((1, W_GRP)), const((1, W_GRP)),
                  per_b(1), per_b(LRU_CONV - 1), per_b(SCONV_W - 1)],
        out_specs=[pl.BlockSpec((None, tl, W_GRP), lambda b, i: (b, i, 0)),
                   pl.BlockSpec((None, tl, W_GRP), lambda b, i: (b, i, 0)),
                   per_b(1), per_b(LRU_CONV - 1), per_b(SCONV_W - 1)],
        out_shape=outs,
        scratch_shapes=[pltpu.VMEM((tl + 8, W_GRP), F32),
                        pltpu.VMEM((tl + 8, W_GRP), F32),
                        pltpu.VMEM((1, W_GRP), F32)],
        compiler_params=_cparams(("arbitrary", "arbitrary")),
        name="mixer_ab",
    )(zm, zm, zm, zm, zm,
      lp['lru_conv_w'], lp['lru_conv_b'], lp['lru_wa_blk'], lp['lru_ba'], lp['lru_wx_blk'], lp['lru_bx'],
      lp['lru_lambda'], lp['sconv_w'], lp['on_a'], lp['on_b'],
      h0, conv0, sconv0)


def _s5_body(u_ref, bb_ref, cre_ref, cim_ref, ar_ref, ai_ref, d_ref, gw_ref, gb_ref, on_ref,
             h0r_ref, h0i_ref,
             o_ref, hnr_ref, hni_ref,
             tr_ref, ti_ref, pr_ref, pi_ref, hr_ref, hi_ref, lr_ref, li_ref, stage_ref):
    b = pl.program_id(0)
    i = pl.program_id(1)
    tl = u_ref.shape[0]
    steps = tl // SEGS
    groups = S5_N // LANES
    halves = W_GRP // LANES
    ar = ar_ref[...]
    ai = ai_ref[...]

    def powers(base_r, base_i, n, first):
        row = lax.broadcasted_iota(jnp.int32, (n, 1), 0)
        tr = jnp.broadcast_to(base_r, (n, S5_N))
        ti = jnp.broadcast_to(base_i, (n, S5_N))
        s = first
        while s < n:
            sr = _shift_rows(tr, s, 1.0, row)
            si = _shift_rows(ti, s, 0.0, row)
            tr, ti = tr * sr - ti * si, tr * si + ti * sr
            s *= 2
        return tr, ti

    @pl.when((b == 0) & (i == 0))
    def _():
        tr, ti = powers(ar, ai, tl, SEGS)
        tr_ref[...] = tr
        ti_ref[...] = ti
        pr, pi = powers(tr[tl - 1:, :], ti[tl - 1:, :], SEGS, 1)
        pr_ref[...] = pr
        pi_ref[...] = pi

    @pl.when(i == 0)
    def _():
        hr_ref[...] = h0r_ref[...]
        hi_ref[...] = h0i_ref[...]

    u = u_ref[...]
    if steps > 1:
        for half in range(halves):
            stage_ref[half] = u[:, half * LANES:(half + 1) * LANES]
        u = jnp.concatenate(
            [jnp.concatenate([stage_ref.at[half][pl.ds(j, SEGS, stride=steps), :] for half in range(halves)], axis=1)
             for j in range(steps)], axis=0)
    bu = jnp.dot(u.astype(MXU_DT), bb_ref[...], preferred_element_type=F32)
    a_r = [jnp.broadcast_to(ar[:, c * LANES:(c + 1) * LANES], (SEGS, LANES)) for c in range(groups)]
    a_i = [jnp.broadcast_to(ai[:, c * LANES:(c + 1) * LANES], (SEGS, LANES)) for c in range(groups)]
    loc_r = [jnp.zeros((SEGS, LANES), F32) for _ in range(groups)]
    loc_i = [jnp.zeros((SEGS, LANES), F32) for _ in range(groups)]
    for j in range(steps):
        rows = slice(j * SEGS, (j + 1) * SEGS)
        for c in range(groups):
            cols = slice(c * LANES, (c + 1) * LANES)
            nr = (a_r[c] * loc_r[c] - a_i[c] * loc_i[c]) + bu[rows, c * LANES:(c + 1) * LANES]
            ni = (a_r[c] * loc_i[c] + a_i[c] * loc_r[c]) + bu[rows, S5_N + c * LANES:S5_N + (c + 1) * LANES]
            loc_r[c], loc_i[c] = nr, ni
            lr_ref[rows, cols] = nr
            li_ref[rows, cols] = ni
    er = jnp.concatenate(loc_r, axis=1)
    ei = jnp.concatenate(loc_i, axis=1)
    seg = lax.broadcasted_iota(jnp.int32, (SEGS, 1), 0)
    pr = pr_ref[...]
    pi = pi_ref[...]
    mr, mi = pr[0:1, :], pi[0:1, :]
    s = 1
    while s < SEGS:
        sr = _shift_rows(er, s, 0.0, seg)
        si = _shift_rows(ei, s, 0.0, seg)
        er, ei = er + (mr * sr - mi * si), ei + (mr * si + mi * sr)
        mr, mi = mr * mr - mi * mi, 2.0 * (mr * mi)
        s *= 2
    cr = hr_ref[...]
    ci = hi_ref[...]
    er, ei = er + (pr * cr - pi * ci), ei + (pr * ci + pi * cr)
    in_r = jnp.where(seg >= 1, pltpu.roll(er, 1, 0), cr)
    in_i = jnp.where(seg >= 1, pltpu.roll(ei, 1, 0), ci)
    hr_ref[...] = er[SEGS - 1:, :]
    hi_ref[...] = ei[SEGS - 1:, :]
    hnr_ref[...] = er[SEGS - 1:, :]
    hni_ref[...] = ei[SEGS - 1:, :]
    tr = tr_ref[...]
    ti = ti_ref[...]
    sr = jnp.tile(in_r, (steps, 1))
    si = jnp.tile(in_i, (steps, 1))
    hr = lr_ref[...] + (tr * sr - ti * si)
    hi = li_ref[...] + (tr * si + ti * sr)
    y = (jnp.dot(hr.astype(MXU_DT), cre_ref[...], preferred_element_type=F32)
         - jnp.dot(hi.astype(MXU_DT), cim_ref[...], preferred_element_type=F32)) + d_ref[...] * u
    g = jnp.dot(_gelu(y).astype(MXU_DT), gw_ref[...], preferred_element_type=F32) + gb_ref[...]
    out = g[:, :W_GRP] * _sigmoid(g[:, W_GRP:])
    out = _rms_rows(out) * on_ref[...]
    if steps > 1:
        for half in range(halves):
            stage_ref[half] = out[:, half * LANES:(half + 1) * LANES]
        out = jnp.concatenate(
            [jnp.concatenate([stage_ref.at[half][pl.ds(sg, steps, stride=SEGS), :] for half in range(halves)], axis=1)
             for sg in range(SEGS)], axis=0)
    o_ref[...] = out.astype(o_ref.dtype)


def _s5_mixer(zm, lp, h0r, h0i, *, tl):
    bsz, seq, _ = zm.shape
    const = lambda shp: pl.BlockSpec(shp, lambda b, i: (0,) * len(shp))
    per_b = pl.BlockSpec((None, 1, S5_N), lambda b, i: (b, 0, 0))
    outs = [jax.ShapeDtypeStruct((bsz, seq, W_GRP), MXU_DT),
            jax.ShapeDtypeStruct((bsz, 1, S5_N), F32),
            jax.ShapeDtypeStruct((bsz, 1, S5_N), F32)]
    return pl.pallas_call(
        _s5_body,
        grid=(bsz, seq // tl),
        in_specs=[pl.BlockSpec((None, tl, W_GRP), lambda b, i: (b, i, 5)),
                  const((W_GRP, 2 * S5_N)), const((S5_N, W_GRP)), const((S5_N, W_GRP)),
                  const((1, S5_N)), const((1, S5_N)), const((1, W_GRP)),
                  const((W_GRP, 2 * W_GRP)), const((1, 2 * W_GRP)), const((1, W_GRP)),
                  per_b, per_b],
        out_specs=[pl.BlockSpec((None, tl, W_GRP), lambda b, i: (b, i, 0)), per_b, per_b],
        out_shape=outs,
        scratch_shapes=[pltpu.VMEM((tl, S5_N), F32), pltpu.VMEM((tl, S5_N), F32),
                        pltpu.VMEM((SEGS, S5_N), F32), pltpu.VMEM((SEGS, S5_N), F32),
                        pltpu.VMEM((1, S5_N), F32), pltpu.VMEM((1, S5_N), F32),
                        pltpu.VMEM((tl, S5_N), F32), pltpu.VMEM((tl, S5_N), F32),
                        pltpu.VMEM((W_GRP // LANES, tl, LANES), F32)],
        compiler_params=_cparams(("arbitrary", "arbitrary")),
        name="s5_mixer",
    )(zm, lp['s5_bb'], lp['s5_cre'], lp['s5_cim'], lp['s5_abr'], lp['s5_abi'], lp['s5_d'],
      lp['s5_glu_w'], lp['s5_glu_b'], lp['on_c'], h0r, h0i)


CODE_MASKED = -1
CODE_ZERO = -2


def _bias_body(rb_ref, code_ref, o_ref):
    code = code_ref[...]
    acc = jnp.where(code == CODE_MASKED, NEG, 0.0).astype(F32)
    for c in range(REL_BUCKETS * ATT_HEADS):
        acc = jnp.where(code == c, rb_ref[c], acc)
    o_ref[...] = acc


def _bias_table(rel_bias, codes):
    rows, cols = codes.shape
    tr = max(t for t in range(8, 513, 8) if rows % t == 0)
    return pl.pallas_call(
        _bias_body,
        grid_spec=pltpu.PrefetchScalarGridSpec(
            num_scalar_prefetch=1,
            grid=(rows // tr,),
            in_specs=[pl.BlockSpec((tr, cols), lambda i, rb: (i, 0))],
            out_specs=pl.BlockSpec((tr, cols), lambda i, rb: (i, 0))),
        out_shape=jax.ShapeDtypeStruct((rows, cols), F32),
        compiler_params=_cparams(("arbitrary",)),
        name="bias_table",
    )(rel_bias.reshape(-1), jnp.asarray(codes))


def _t5_bucket(n):
    n = np.asarray(n).astype(np.int32)
    max_exact = REL_BUCKETS // 2
    nf = np.maximum(n, 1).astype(np.float32)
    large = max_exact + (np.log(nf / max_exact) / np.log(REL_MAX_DIST / max_exact)
                         * (REL_BUCKETS - max_exact)).astype(np.int32)
    large = np.minimum(large, REL_BUCKETS - 1)
    return np.where(n < max_exact, n, large).astype(np.int32)


def _prompt_bias_codes():
    i = np.arange(Q_BLK)[:, None]
    j = np.arange(2 * Q_BLK)[None, :]
    rel = Q_BLK + i - j
    out = np.zeros((len(PATTERNS), 2, ATT_HEADS, Q_BLK, 2 * Q_BLK), np.int32)
    for p, (win, dil) in enumerate(PATTERNS):
        span = win // dil
        valid = (rel >= 0) & (rel <= span)
        bucket = _t5_bucket(np.clip(rel, 0, None) * dil)
        for var in range(2):
            v = valid & ((j >= Q_BLK) | (var == 1))
            for h in range(ATT_HEADS):
                out[p, var, h] = np.where(v, bucket * ATT_HEADS + h, CODE_MASKED)
    return out.reshape(-1, 2 * Q_BLK)


def _sample_bias_codes(dec_seq, wb):
    def tables(n_keys, first_pos):
        codes = np.full((ATT_HEADS, dec_seq, n_keys), CODE_MASKED, np.int32)
        logm = np.zeros((ATT_HEADS, dec_seq, n_keys), np.float32)
        for s in range(dec_seq):
            dist = (wb + s) - (first_pos + np.arange(n_keys))
            mult = np.zeros(n_keys, np.int32)
            for win, dil in PATTERNS:
                mult += ((dist >= 0) & (dist % dil == 0) & (dist // dil <= win // dil)).astype(np.int32)
            bucket = _t5_bucket(np.clip(dist, 0, None))
            for h in range(ATT_HEADS):
                codes[h, s] = np.where(mult > 0, bucket * ATT_HEADS + h, CODE_MASKED)
                logm[h, s] = np.log(np.maximum(mult, 1))
        return codes.reshape(ATT_HEADS * dec_seq, n_keys), logm.reshape(ATT_HEADS * dec_seq, n_keys)
    return tables(wb, 0), tables(dec_seq, wb)


def _attn_p_body(q_ref, kp_ref, kc_ref, vp_ref, vc_ref, bias_ref, o_ref, lse_ref):
    lane = lax.broadcasted_iota(jnp.int32, (1, W_GRP), 1)
    head_of_lane = [(lane >= h * HEAD_DIM) & (lane < (h + 1) * HEAD_DIM) for h in range(ATT_HEADS)]
    n_sub = q_ref.shape[0] // Q_BLK
    first_variant = jnp.minimum(pl.program_id(2), 1)
    for sub in range(n_sub):
        rows = slice(sub * Q_BLK, (sub + 1) * Q_BLK)
        q = q_ref[rows, :]
        if sub == 0:
            k_prev, v_prev, bias = kp_ref[...], vp_ref[...], bias_ref[first_variant]
        else:
            prev_rows = slice((sub - 1) * Q_BLK, sub * Q_BLK)
            k_prev, v_prev, bias = kc_ref[prev_rows, :], vc_ref[prev_rows, :], bias_ref[1]
        k2 = jnp.concatenate([k_prev, kc_ref[rows, :]], axis=0)
        v2 = jnp.concatenate([v_prev, vc_ref[rows, :]], axis=0)
        qs = jnp.concatenate([jnp.where(hm, q, jnp.zeros_like(q)) for hm in head_of_lane], axis=0)
        s = lax.dot_general(qs, k2, (((1,), (1,)), ((), ())), preferred_element_type=F32) + bias
        m = jnp.max(s, axis=-1, keepdims=True)
        pr = jnp.exp(s - m)
        den = jnp.sum(pr, axis=-1, keepdims=True)
        pv = jnp.dot(pr.astype(MXU_DT), v2, preferred_element_type=F32) / den
        lse_rows = m + jnp.log(den)
        o = jnp.zeros((Q_BLK, W_GRP), F32)
        lse = jnp.zeros((Q_BLK, W_GRP), F32)
        for h, hm in enumerate(head_of_lane):
            o = jnp.where(hm, pv[h * Q_BLK:(h + 1) * Q_BLK], o)
            lse = jnp.where(hm, lse_rows[h * Q_BLK:(h + 1) * Q_BLK], lse)
        o_ref[rows, :] = o
        lse_ref[rows, :] = lse


def _attn_prompt_pattern(q, k, v, bias):
    bsz, dil, md, _ = q.shape
    n_sub = math.gcd(ATTN_SUB_BLOCKS, md // Q_BLK)
    cur = pl.BlockSpec((None, None, n_sub * Q_BLK, W_GRP), lambda b, r, n: (b, r, n, 0))
    prv = pl.BlockSpec((None, None, Q_BLK, W_GRP), lambda b, r, n: (b, r, jnp.maximum(n * n_sub - 1, 0), 0))
    bsp = pl.BlockSpec((2, ATT_HEADS * Q_BLK, 2 * Q_BLK), lambda b, r, n: (0, 0, 0))
    return pl.pallas_call(
        _attn_p_body,
        grid=(bsz, dil, md // (n_sub * Q_BLK)),
        in_specs=[cur, prv, cur, prv, cur, bsp],
        out_specs=[cur, cur],
        out_shape=[jax.ShapeDtypeStruct((bsz, dil, md, W_GRP), F32)] * 2,
        compiler_params=_cparams(("arbitrary", "arbitrary", "arbitrary")),
        name=f"attn_prompt_d{dil}",
    )(q, k, k, v, v, bias)


def _attn_merge_body(*refs, dils):
    n_pat = len(dils)
    on_ref, out_ref, stage_ref = refs[2 * n_pat:]
    tm = out_ref.shape[0]
    halves = W_GRP // LANES
    vals = []
    for t in range(2 * n_pat):
        dil = dils[t // 2]
        if dil == 1:
            vals.append(refs[t][...])
            continue
        for r in range(dil):
            blk = refs[t][r]
            for half in range(halves):
                stage_ref.at[t, half][pl.ds(r, tm // dil, stride=dil), :] = blk[:, half * LANES:(half + 1) * LANES]
        vals.append(jnp.concatenate([stage_ref[t, half] for half in range(halves)], axis=1))
    os_, ls_ = vals[0::2], vals[1::2]
    mx = functools.reduce(jnp.maximum, ls_)
    ws = [jnp.exp(l_ - mx) for l_ in ls_]
    num = functools.reduce(lambda a_, b_: a_ + b_, [w_ * o_ for w_, o_ in zip(ws, os_)])
    merged = num / functools.reduce(lambda a_, b_: a_ + b_, ws)
    out_ref[...] = (_rms_rows(merged) * on_ref[...]).astype(out_ref.dtype)


def _attn_merge(pattern_outs, on_d, *, tm):
    dils = tuple(o.shape[1] for o, _ in pattern_outs)
    bsz, _, seq, _ = pattern_outs[0][0].shape
    seq = seq * dils[0]
    specs, args = [], []
    for (o, lse), dil in zip(pattern_outs, dils):
        if dil == 1:
            sp = pl.BlockSpec((None, None, tm, W_GRP), lambda b, j: (b, 0, j, 0))
        else:
            sp = pl.BlockSpec((None, dil, tm // dil, W_GRP), lambda b, j: (b, 0, j, 0))
        specs += [sp, sp]
        args += [o, lse]
    return pl.pallas_call(
        functools.partial(_attn_merge_body, dils=dils),
        grid=(bsz, seq // tm),
        in_specs=specs + [pl.BlockSpec((1, W_GRP), lambda b, j: (0, 0))],
        out_specs=pl.BlockSpec((None, tm, W_GRP), lambda b, j: (b, j, 0)),
        out_shape=jax.ShapeDtypeStruct((bsz, seq, W_GRP), MXU_DT),
        scratch_shapes=[pltpu.VMEM((2 * len(dils), W_GRP // LANES, tm, LANES), F32)],
        compiler_params=_cparams(("arbitrary", "arbitrary")),
        name="attn_merge",
    )(*args, on_d)


def _attn_prompt(qkv_by_dil, bias_all, on_d, *, tm):
    outs = [_attn_prompt_pattern(q, k, v, bias_all[p]) for p, (q, k, v) in enumerate(qkv_by_dil)]
    return _attn_merge(outs, on_d, tm=tm)


def _attn_s_body(q_ref, kn_ref, vn_ref, kt_ref, vt_ref, lw_ref, lwn_ref, on_ref, o_ref):
    q = q_ref[...]
    kn = kn_ref[...].astype(MXU_DT)
    vn = vn_ref[...].astype(MXU_DT)
    nt = (((1,), (1,)), ((), ()))
    outs = []
    for h in range(ATT_HEADS):
        cols = slice(h * HEAD_DIM, (h + 1) * HEAD_DIM)
        qh = q[:, cols]
        sc = jnp.dot(qh, kt_ref[h].astype(MXU_DT), preferred_element_type=F32) + lw_ref[h]
        scn = lax.dot_general(qh, kn[:, cols], nt, preferred_element_type=F32) + lwn_ref[h]
        m = jnp.maximum(jnp.max(sc, axis=-1, keepdims=True), jnp.max(scn, axis=-1, keepdims=True))
        p = jnp.exp(sc - m)
        pn = jnp.exp(scn - m)
        den = jnp.sum(p, axis=-1, keepdims=True) + jnp.sum(pn, axis=-1, keepdims=True)
        num = (lax.dot_general(p.astype(MXU_DT), vt_ref[h].astype(MXU_DT), nt, preferred_element_type=F32)
               + jnp.dot(pn.astype(MXU_DT), vn[:, cols], preferred_element_type=F32))
        outs.append(num / den)
    merged = jnp.concatenate(outs, axis=1)
    o_ref[...] = (_rms_rows(merged) * on_ref[...]).astype(o_ref.dtype)


def _attn_sample(q, kn, vn, cache_k, cache_v, l, lw, lwn, on_d):
    bsz, dec, _ = q.shape
    wb = cache_k.shape[2]
    per_b = pl.BlockSpec((None, dec, W_GRP), lambda b: (b, 0, 0))
    pos_minor = lambda c: jnp.transpose(c, (0, 1, 3, 4, 2))
    cache_spec = pl.BlockSpec((None, None, ATT_HEADS, HEAD_DIM, wb), lambda b: (l, b, 0, 0, 0))
    full = lambda a: pl.BlockSpec(a.shape, lambda b: (0,) * a.ndim)
    return pl.pallas_call(
        _attn_s_body,
        grid=(bsz,),
        in_specs=[per_b, per_b, per_b, cache_spec, cache_spec, full(lw), full(lwn), full(on_d)],
        out_specs=per_b,
        out_shape=jax.ShapeDtypeStruct((bsz, dec, W_GRP), MXU_DT),
        compiler_params=_cparams(("arbitrary",)),
        name="attn_sample",
    )(q, kn, vn, pos_minor(cache_k), pos_minor(cache_v), lw, lwn, on_d)


ROUTER_LANES = 128
HALF_D = D_MODEL // 2
ROW_TILE = HALF_D // LANES
ROW_DT = jnp.uint32


def _store_row_tiles(ref, val):
    n = val.shape[0]
    bits = lambda t: lax.bitcast_convert_type(t.astype(jnp.bfloat16).astype(F32), ROW_DT)
    packed = (bits(val[:, HALF_D:]) & jnp.uint32(0xFFFF0000)) | (bits(val[:, :HALF_D]) >> 16)
    for j in range(ROW_TILE):
        ref[pl.ds(j, n, stride=ROW_TILE), :] = packed[:, j * LANES:(j + 1) * LANES]


def _load_row_tiles(ref):
    n = ref.shape[0] // ROW_TILE
    packed = jnp.concatenate([ref[pl.ds(j, n, stride=ROW_TILE), :] for j in range(ROW_TILE)], axis=1)
    return (lax.bitcast_convert_type(packed << 16, F32),
            lax.bitcast_convert_type(packed & jnp.uint32(0xFFFF0000), F32))


OUT_TILE = D_MODEL // LANES


def _store_out_tiles(ref, val):
    n = val.shape[0]
    for j in range(OUT_TILE):
        ref[pl.ds(j, n, stride=OUT_TILE), :] = val[:, j * LANES:(j + 1) * LANES]


def _load_out_tiles(ref):
    n = ref.shape[0] // OUT_TILE
    return jnp.concatenate([ref[pl.ds(j, n, stride=OUT_TILE), :] for j in range(OUT_TILE)], axis=1)


def _post_mix_body(x_ref, a_ref, b_ref, c_ref, d_ref, w_ref, g1_ref, sc_ref, sh_ref, g_ref,
                   wr_ref, br_ref, tri_ref, cnt_in_ref,
                   x1_ref, h2_ref, eid_ref, gate_ref, rank_ref, cnt_ref,
                   run_ref):
    i = pl.program_id(0)

    @pl.when(i == 0)
    def _():
        run_ref[...] = cnt_in_ref[...]

    mix = jnp.concatenate([a_ref[...], b_ref[...], c_ref[...], d_ref[...]], axis=1)
    y = jnp.dot(mix, w_ref[...], preferred_element_type=F32)
    x1 = x_ref[...] + g1_ref[...] * y
    x1_ref[...] = x1
    h2 = _rms_rows(x1) * g_ref[...]
    h2 = h2 * (1.0 + sc_ref[...]) + sh_ref[...]
    _store_row_tiles(h2_ref, h2)
    logits = jnp.dot(h2.astype(MXU_DT), wr_ref[...], preferred_element_type=F32) + br_ref[...]
    tm = logits.shape[0]
    lane = lax.broadcasted_iota(jnp.int32, (tm, ROUTER_LANES), 1)
    big = jnp.int32(10 ** 6)
    is_g = lane < N_GROUPS
    gl = jnp.where(is_g, logits, -jnp.inf)
    gmax = jnp.max(gl, axis=-1, keepdims=True)
    gsel = jnp.min(jnp.where(gl == gmax, lane, big), axis=-1, keepdims=True)
    gprob = 1.0 / jnp.sum(jnp.where(is_g, jnp.exp(logits - gmax), 0.0), axis=-1, keepdims=True)
    lo_lane = N_GROUPS + gsel * EXP_PER_GROUP
    in_grp = (lane >= lo_lane) & (lane < lo_lane + EXP_PER_GROUP)
    el = jnp.where(in_grp, logits, -jnp.inf)
    v1 = jnp.max(el, axis=-1, keepdims=True)
    i1 = jnp.min(jnp.where(el == v1, lane, big), axis=-1, keepdims=True)
    el2 = jnp.where(lane == i1, -jnp.inf, el)
    v2 = jnp.max(el2, axis=-1, keepdims=True)
    i2 = jnp.min(jnp.where(el2 == v2, lane, big), axis=-1, keepdims=True)
    e2w = jnp.exp(v2 - v1)
    gate1 = (1.0 / (1.0 + e2w)) * gprob
    gate2 = (e2w / (1.0 + e2w)) * gprob
    e1 = i1 - N_GROUPS
    e2 = i2 - N_GROUPS
    oh1 = lane == e1
    oh2 = lane == e2
    both = jnp.where(oh1 | oh2, 1.0, 0.0)
    before = jnp.dot(tri_ref[...], both.astype(jnp.bfloat16), preferred_element_type=F32) + run_ref[...]
    r1 = jnp.sum(jnp.where(oh1, before, 0.0), axis=-1, keepdims=True).astype(jnp.int32)
    r2 = jnp.sum(jnp.where(oh2, before, 0.0), axis=-1, keepdims=True).astype(jnp.int32)
    run = run_ref[...] + jnp.sum(both, axis=0, keepdims=True)
    run_ref[...] = run
    cnt_ref[...] = run
    eid_ref[...] = jnp.where(lane == 0, e1, jnp.where(lane == 1, e2, 0))
    gate_ref[...] = jnp.where(lane == 0, gate1, jnp.where(lane == 1, gate2, 0.0))
    rank_ref[...] = jnp.where(lane == 0, r1, jnp.where(lane == 1, r2, 0))


def _post_mix(x, pieces, mod, lp, cnt_in, *, tm, rows_per_mod):
    t = x.shape[0]
    row = lambda i: (i, 0)
    const = lambda i: (0, 0)
    piece = pl.BlockSpec((tm, W_GRP), row)
    wide = pl.BlockSpec((tm, D_MODEL), row)
    lanes = pl.BlockSpec((tm, ROUTER_LANES), row)
    outs = [jax.ShapeDtypeStruct((t, D_MODEL), F32), jax.ShapeDtypeStruct((t * ROW_TILE, LANES), ROW_DT),
            jax.ShapeDtypeStruct((t, ROUTER_LANES), jnp.int32), jax.ShapeDtypeStruct((t, ROUTER_LANES), F32),
            jax.ShapeDtypeStruct((t, ROUTER_LANES), jnp.int32), jax.ShapeDtypeStruct((1, ROUTER_LANES), F32)]
    return pl.pallas_call(
        _post_mix_body,
        grid=(t // tm,),
        in_specs=[wide, piece, piece, piece, piece,
                  pl.BlockSpec((D_MODEL, D_MODEL), const),
                  _mod_spec(mod, 2, tm, rows_per_mod), _mod_spec(mod, 4, tm, rows_per_mod),
                  _mod_spec(mod, 3, tm, rows_per_mod),
                  pl.BlockSpec((1, D_MODEL), const),
                  pl.BlockSpec((D_MODEL, ROUTER_LANES), const), pl.BlockSpec((1, ROUTER_LANES), const),
                  pl.BlockSpec((tm, tm), const), pl.BlockSpec((1, ROUTER_LANES), const)],
        out_specs=[wide, pl.BlockSpec((tm * ROW_TILE, LANES), row), lanes, lanes, lanes,
                   pl.BlockSpec((1, ROUTER_LANES), const)],
        out_shape=outs,
        scratch_shapes=[pltpu.VMEM((1, ROUTER_LANES), F32)],
        compiler_params=_cparams(("arbitrary",)),
        name="post_mix",
    )(x, *pieces, lp['w_out'], mod, mod, mod, lp['norm_ffn'], lp['router_w'], lp['router_b'], lp['tri'], cnt_in)


def _row_copy(src_ref, s, dst_ref, d, sem, tile=ROW_TILE):
    return pltpu.make_async_copy(src_ref.at[pl.ds(pl.multiple_of(s * tile, tile), tile)],
                                 dst_ref.at[pl.ds(pl.multiple_of(d * tile, tile), tile)], sem)


def _dispatch_body(dest_ref, h_ref, xs_in_ref, xs_ref, sem):
    del xs_in_ref
    tm = h_ref.shape[0] // ROW_TILE
    base = pl.program_id(0) * (2 * tm)

    def issue(t, c):
        _row_copy(h_ref, t, xs_ref, dest_ref[base + 2 * t], sem).start()
        _row_copy(h_ref, t, xs_ref, dest_ref[base + 2 * t + 1], sem).start()
        return c

    lax.fori_loop(0, tm, issue, 0)
    for _ in range(2):
        pltpu.make_async_copy(h_ref, xs_ref.at[pl.ds(0, tm * ROW_TILE)], sem).wait()


def _dispatch(h2, dest_flat, xs, *, tm):
    t = h2.shape[0] // ROW_TILE
    return pl.pallas_call(
        _dispatch_body,
        grid_spec=pltpu.PrefetchScalarGridSpec(
            num_scalar_prefetch=1,
            grid=(t // tm,),
            in_specs=[pl.BlockSpec((tm * ROW_TILE, LANES), lambda i, d: (i, 0)),
                      pl.BlockSpec(memory_space=pl.ANY)],
            out_specs=pl.BlockSpec(memory_space=pl.ANY),
            scratch_shapes=[pltpu.SemaphoreType.DMA(())]),
        out_shape=jax.ShapeDtypeStruct(xs.shape, xs.dtype),
        input_output_aliases={2: 0},
        compiler_params=_cparams(("arbitrary",)),
        name="moe_dispatch",
    )(dest_flat, h2, xs)


def _experts_body(blk_e_ref, n_used_ref, xs_ref, wg_ref, wu_ref, wd_ref, o_ref, wgb_ref, wub_ref, wdb_ref):
    i = pl.program_id(0)
    e = blk_e_ref[i]
    e_prev = blk_e_ref[jnp.maximum(i - 1, 0)]

    @pl.when((i == 0) | (e != e_prev))
    def _():
        wgb_ref[...] = wg_ref[...].astype(MXU_DT)
        wub_ref[...] = wu_ref[...].astype(MXU_DT)
        wdb_ref[...] = wd_ref[...].astype(MXU_DT)

    @pl.when(i < n_used_ref[0])
    def _():
        x_lo, x_hi = [t.astype(MXU_DT) for t in _load_row_tiles(xs_ref)]
        g = (jnp.dot(x_lo, wgb_ref[:HALF_D, :], preferred_element_type=F32)
             + jnp.dot(x_hi, wgb_ref[HALF_D:, :], preferred_element_type=F32))
        u = (jnp.dot(x_lo, wub_ref[:HALF_D, :], preferred_element_type=F32)
             + jnp.dot(x_hi, wub_ref[HALF_D:, :], preferred_element_type=F32))
        hmid = (g * _sigmoid(g)) * u
        _store_out_tiles(o_ref, jnp.dot(hmid.astype(MXU_DT), wdb_ref[...], preferred_element_type=F32))

    @pl.when(i >= n_used_ref[0])
    def _():
        o_ref[...] = jnp.zeros_like(o_ref)


def _experts(xs, blk_e, n_used, wg, wu, wd, l):
    n_blk = xs.shape[0] // (MOE_ROWS * ROW_TILE)
    xmap = lambda i, be, nu: (jnp.minimum(i, nu[0] - 1), 0)
    return pl.pallas_call(
        _experts_body,
        grid_spec=pltpu.PrefetchScalarGridSpec(
            num_scalar_prefetch=2,
            grid=(n_blk,),
            in_specs=[pl.BlockSpec((MOE_ROWS * ROW_TILE, LANES), xmap),
                      pl.BlockSpec((None, None, D_MODEL, D_EXPERT), lambda i, be, nu: (l, be[i], 0, 0)),
                      pl.BlockSpec((None, None, D_MODEL, D_EXPERT), lambda i, be, nu: (l, be[i], 0, 0)),
                      pl.BlockSpec((None, None, D_EXPERT, D_MODEL), lambda i, be, nu: (l, be[i], 0, 0))],
            out_specs=pl.BlockSpec((MOE_ROWS * OUT_TILE, LANES), lambda i, be, nu: (i, 0)),
            scratch_shapes=[pltpu.VMEM((D_MODEL, D_EXPERT), MXU_DT), pltpu.VMEM((D_MODEL, D_EXPERT), MXU_DT),
                            pltpu.VMEM((D_EXPERT, D_MODEL), MXU_DT)]),
        out_shape=jax.ShapeDtypeStruct((n_blk * MOE_ROWS * OUT_TILE, LANES), F32),
        compiler_params=_cparams(("arbitrary",)),
        name="moe_experts",
    )(blk_e, n_used, xs, wg, wu, wd)


def _combine_body(dest_ref, ys_ref, x1_ref, gate_ref, g2_ref, x2_ref, a0_ref, a1_ref, b0_ref, b1_ref, sem):
    tm = x1_ref.shape[0]
    i = pl.program_id(0)
    n = pl.num_programs(0)
    sets = ((a0_ref, a1_ref, sem.at[0]), (b0_ref, b1_ref, sem.at[1]))

    def gather(tile, bufs):
        k0_ref, k1_ref, s = bufs
        base = tile * (2 * tm)

        def issue(t, c):
            _row_copy(ys_ref, dest_ref[base + 2 * t], k0_ref, t, s, OUT_TILE).start()
            _row_copy(ys_ref, dest_ref[base + 2 * t + 1], k1_ref, t, s, OUT_TILE).start()
            return c

        lax.fori_loop(0, tm, issue, 0)

    def step(cur, nxt):
        @pl.when(i + 1 < n)
        def _():
            gather(i + 1, nxt)

        k0_ref, k1_ref, s = cur
        for buf_ref in (k0_ref, k1_ref):
            pltpu.make_async_copy(ys_ref.at[pl.ds(0, tm * OUT_TILE)], buf_ref, s).wait()
        gate = gate_ref[...]
        y = _load_out_tiles(k0_ref) * gate[:, 0:1] + _load_out_tiles(k1_ref) * gate[:, 1:2]
        x2_ref[...] = x1_ref[...] + g2_ref[...] * y

    @pl.when(i == 0)
    def _():
        gather(0, sets[0])

    @pl.when(i % 2 == 0)
    def _():
        step(sets[0], sets[1])

    @pl.when(i % 2 == 1)
    def _():
        step(sets[1], sets[0])


def _combine(ys, dest_flat, x1, gate, mod, *, tm, rows_per_mod):
    t = x1.shape[0]
    if mod.ndim == 4:
        g2_spec = pl.BlockSpec((None, None, 1, D_MODEL), lambda i, d: (5, (i * tm) // rows_per_mod, 0, 0))
    else:
        g2_spec = pl.BlockSpec((None, tm, D_MODEL), lambda i, d: (5, i, 0))
    return pl.pallas_call(
        _combine_body,
        grid_spec=pltpu.PrefetchScalarGridSpec(
            num_scalar_prefetch=1,
            grid=(t // tm,),
            in_specs=[pl.BlockSpec(memory_space=pl.ANY),
                      pl.BlockSpec((tm, D_MODEL), lambda i, d: (i, 0)),
                      pl.BlockSpec((tm, ROUTER_LANES), lambda i, d: (i, 0)),
                      g2_spec],
            out_specs=pl.BlockSpec((tm, D_MODEL), lambda i, d: (i, 0)),
            scratch_shapes=[pltpu.VMEM((tm * OUT_TILE, LANES), F32)] * 4 + [pltpu.SemaphoreType.DMA((2,))]),
        out_shape=jax.ShapeDtypeStruct((t, D_MODEL), F32),
        compiler_params=_cparams(("arbitrary",)),
        name="moe_combine",
    )(dest_flat, ys, x1, gate, mod)


def _routing_tables(cnt, eids, ranks, n_blk):
    counts = cnt[0, :N_EXPERTS].astype(jnp.int32)
    padded = (counts + MOE_ROWS - 1) // MOE_ROWS * MOE_ROWS
    ends = jnp.cumsum(padded)
    starts = ends - padded
    experts = jnp.arange(N_EXPERTS, dtype=jnp.int32)
    start_of = lambda e: jnp.sum(jnp.where(e[..., None] == experts, starts, 0), axis=-1)
    dests = [(start_of(e[:, :2]) + r[:, :2]).reshape(-1) for e, r in zip(eids, ranks)]
    blk_start = jnp.arange(n_blk, dtype=jnp.int32) * MOE_ROWS
    blk_e = jnp.minimum(jnp.sum((ends[None, :] <= blk_start[:, None]).astype(jnp.int32), axis=1), N_EXPERTS - 1)
    n_used = (ends[-1] // MOE_ROWS).astype(jnp.int32).reshape(1)
    return dests, blk_e, n_used


def _block_diag(w):
    g, r, c = w.shape
    eye = jnp.eye(g, dtype=w.dtype)
    return (eye[:, None, :, None] * w[:, :, None, :]).reshape(g * r, g * c)


def _s5_discretise(log_dt, a_re, a_im, b_re, b_im):
    step = jnp.exp(log_dt)[:, None]
    mag = jnp.exp(a_re * step)
    ang = a_im * step
    abr = mag * jnp.cos(ang)
    abi = mag * jnp.sin(ang)
    den = a_re * a_re + a_im * a_im
    zr = ((abr - 1.0) * a_re + abi * a_im) / den
    zi = (abi * a_re - (abr - 1.0) * a_im) / den
    bbr = zr[..., None] * b_re - zi[..., None] * b_im
    bbi = zr[..., None] * b_im + zi[..., None] * b_re
    return abr, abi, bbr, bbi


def _prep_layer(P, l):
    row = lambda a: a.reshape(1, -1)
    abr, abi, bbr, bbi = _s5_discretise(P['s5_log_dt'][l], P['s5_a_re'][l], P['s5_a_im'][l],
                                        P['s5_b_re'][l], P['s5_b_im'][l])
    on = P['out_norm'][l]
    bones = _block_diag(jnp.full((ATT_HEADS, HEAD_DIM, HEAD_DIM), 1.0 / HEAD_DIM, F32)).astype(jnp.bfloat16)
    return {
        'norm_mix': row(P['norm_mix'][l]), 'norm_ffn': row(P['norm_ffn'][l]),
        'w_in': P['w_in'][l].astype(MXU_DT), 'w_out': P['w_out'][l].astype(MXU_DT),
        'bones': bones,
        'qg': row(jnp.tile(P['q_norm'][l], ATT_HEADS)), 'kg': row(jnp.tile(P['k_norm'][l], ATT_HEADS)),
        'lru_conv_w': P['lru_conv_w'][l], 'lru_conv_b': row(P['lru_conv_b'][l]),
        'lru_wa_blk': _block_diag(P['lru_wa'][l]).astype(MXU_DT), 'lru_ba': row(P['lru_ba'][l]),
        'lru_wx_blk': _block_diag(P['lru_wx'][l]).astype(MXU_DT), 'lru_bx': row(P['lru_bx'][l]),
        'lru_lambda': row(P['lru_lambda'][l]), 'sconv_w': P['sconv_w'][l],
        'on_a': row(on[0:W_GRP]), 'on_b': row(on[W_GRP:2 * W_GRP]),
        'on_c': row(on[2 * W_GRP:3 * W_GRP]), 'on_d': row(on[3 * W_GRP:]),
        's5_bb': jnp.concatenate([_block_diag(bbr.transpose(0, 2, 1)), _block_diag(bbi.transpose(0, 2, 1))],
                                 axis=1).astype(MXU_DT),
        's5_cre': _block_diag(P['s5_c_re'][l].transpose(0, 2, 1)).astype(MXU_DT),
        's5_cim': _block_diag(P['s5_c_im'][l].transpose(0, 2, 1)).astype(MXU_DT),
        's5_abr': row(abr), 's5_abi': row(abi), 's5_d': row(P['s5_d'][l]),
        's5_glu_w': P['s5_glu_w'][l].astype(MXU_DT), 's5_glu_b': row(P['s5_glu_b'][l]),
        'router_w': jnp.zeros((D_MODEL, ROUTER_LANES), F32)
                       .at[:, :N_GROUPS].set(P['router_g_w'][l])
                       .at[:, N_GROUPS:N_GROUPS + N_EXPERTS].set(P['router_e_w'][l]).astype(MXU_DT),
        'router_b': jnp.zeros((1, ROUTER_LANES), F32)
                       .at[0, :N_GROUPS].set(P['router_g_b'][l])
                       .at[0, N_GROUPS:N_GROUPS + N_EXPERTS].set(P['router_e_b'][l]),
    }


TOKEN_TILE = 512
SEQ_TILE = 256

_PARAM_NAMES = ('rel_bias', 'mod_w', 'mod_b', 'norm_mix', 'norm_ffn', 'w_in', 'lru_conv_w', 'lru_conv_b',
                'lru_wa', 'lru_ba', 'lru_wx', 'lru_bx', 'lru_lambda', 'sconv_w', 's5_log_dt', 's5_a_re',
                's5_a_im', 's5_b_re', 's5_b_im', 's5_c_re', 's5_c_im', 's5_d', 's5_glu_w', 's5_glu_b',
                'q_norm', 'k_norm', 'out_norm', 'w_out', 'router_g_w', 'router_g_b', 'router_e_w',
                'router_e_b', 'moe_w_gate', 'moe_w_up', 'moe_w_down')


def _mixers(x, mod, lp, st, attn_fn, *, batch, seq, tm, tl, rows_per_mod, dils=(), kv_window=None):
    res = _in_proj(x, mod, lp['norm_mix'], lp['w_in'], lp['bones'], lp['qg'], lp['kg'],
                   tm=tm, rows_per_mod=rows_per_mod, dils=dils, seq=seq, kv_window=kv_window)
    zm, q, kf, vf, kb, vb = res[:6]
    zm3 = zm.reshape(batch, seq, 6 * W_GRP)
    oa, ob, lru_h, lru_conv, sconv = _mixer_ab(zm3, lp, st['lru_h'], st['lru_conv'], st['sconv'], tl=tl)
    oc, s5_re, s5_im = _s5_mixer(zm3, lp, st['s5_re'], st['s5_im'], tl=tl)
    r3 = lambda t: t.reshape(batch, seq, W_GRP)
    if kv_window is None:
        kf, vf = r3(kf), r3(vf)
        win = lambda t: t.reshape(batch, seq, ATT_HEADS, HEAD_DIM)
    else:
        win = lambda t: jnp.transpose(t.reshape(batch, ATT_HEADS, HEAD_DIM, kv_window), (0, 3, 1, 2))
    od = attn_fn(r3(q), kf, vf, r3(kb), r3(vb), res[6:])
    flat = lambda t: t.reshape(batch * seq, W_GRP)
    new_st = {'lru_h': lru_h[:, 0], 'lru_conv': lru_conv, 'sconv': sconv,
              's5_re': s5_re.reshape(batch, S5_GROUPS, S5_STATE), 's5_im': s5_im.reshape(batch, S5_GROUPS, S5_STATE),
              'win_k': win(kf), 'win_v': win(vf)}
    return [flat(oa), flat(ob), flat(oc), flat(od)], new_st


def kernel(x_prompt, x_sample, c_prompt, c_sample, state_lru_h, state_lru_conv, state_sconv, state_s5_re, state_s5_im, cache_win_k, cache_win_v, rel_bias, mod_w, mod_b, norm_mix, norm_ffn, w_in, lru_conv_w, lru_conv_b, lru_wa, lru_ba, lru_wx, lru_bx, lru_lambda, sconv_w, s5_log_dt, s5_a_re, s5_a_im, s5_b_re, s5_b_im, s5_c_re, s5_c_im, s5_d, s5_glu_w, s5_glu_b, q_norm, k_norm, out_norm, w_out, router_g_w, router_g_b, router_e_w, router_e_b, moe_w_gate, moe_w_up, moe_w_down):
    P = dict(zip(_PARAM_NAMES, (rel_bias, mod_w, mod_b, norm_mix, norm_ffn, w_in, lru_conv_w, lru_conv_b,
                                lru_wa, lru_ba, lru_wx, lru_bx, lru_lambda, sconv_w, s5_log_dt, s5_a_re,
                                s5_a_im, s5_b_re, s5_b_im, s5_c_re, s5_c_im, s5_d, s5_glu_w, s5_glu_b,
                                q_norm, k_norm, out_norm, w_out, router_g_w, router_g_b, router_e_w,
                                router_e_b, moe_w_gate, moe_w_up, moe_w_down)))
    bp, seq, d = x_prompt.shape
    bs, dec, _ = x_sample.shape
    depth = mod_w.shape[0]
    tp, ts = bp * seq, bs * dec
    wb = cache_win_k.shape[2]
    wp = min(PATTERNS[-1][0], seq)
    tm_p = min(TOKEN_TILE, tp)
    tm_s = min(TOKEN_TILE, ts)
    tl_p = min(SEQ_TILE, seq)

    nc = -(-(bp + bs) // 8) * 8
    c_all = jnp.zeros((nc, d), F32).at[:bp].set(c_prompt).at[bp:bp + bs].set(c_sample)
    mod_all = _modulation(c_all, mod_w, mod_b)
    bias_p = _bias_table(rel_bias, _prompt_bias_codes()).reshape(len(PATTERNS), 2, ATT_HEADS * Q_BLK, 2 * Q_BLK)
    dils_p = tuple(dil for _, dil in PATTERNS if dil > 1)
    (lw_codes, lw_logm), (lwn_codes, lwn_logm) = _sample_bias_codes(dec, wb)
    lw_s = (_bias_table(rel_bias, lw_codes) + lw_logm).reshape(ATT_HEADS, dec, wb)
    lwn_s = (_bias_table(rel_bias, lwn_codes) + lwn_logm).reshape(ATT_HEADS, dec, dec)
    tri = jnp.asarray(np.tril(np.ones((TOKEN_TILE, TOKEN_TILE), np.float32), -1), jnp.bfloat16)
    n_blk = (2 * (tp + ts)) // MOE_ROWS + N_EXPERTS

    zero_st = {'lru_h': jnp.zeros((bp, 1, W_GRP), F32), 'lru_conv': jnp.zeros((bp, LRU_CONV - 1, W_GRP), F32),
               'sconv': jnp.zeros((bp, SCONV_W - 1, W_GRP), F32),
               's5_re': jnp.zeros((bp, 1, S5_N), F32), 's5_im': jnp.zeros((bp, 1, S5_N), F32)}
    names = ('lru_h', 'lru_conv', 'sconv', 's5_re', 's5_im', 'win_k', 'win_v')
    acc_p = {n: [] for n in names}
    acc_s = {n: [] for n in names}
    xp = x_prompt.reshape(tp, d)
    xs = x_sample.reshape(ts, d)
    for l in range(depth):
        lp = _prep_layer(P, l)
        lp['tri'] = tri
        m6 = mod_all[l].reshape(nc, 6, d).transpose(1, 0, 2)
        mod_p = m6[:, :bp].reshape(6, bp, 1, d)
        mod_s = jnp.repeat(m6[:, bp:bp + bs], dec, axis=1)
        attn_p = lambda q, kf, vf, kb, vb, ex: _attn_prompt(
            [(q[:, None], kb[:, None], vb[:, None])] + [tuple(ex[3 * t:3 * t + 3]) for t in range(len(dils_p))],
            bias_p, lp['on_d'], tm=tm_p)
        pieces_p, st_p = _mixers(xp, mod_p, lp, zero_st, attn_p, batch=bp, seq=seq, tm=tm_p, tl=tl_p,
                                 rows_per_mod=seq, dils=dils_p, kv_window=wp)
        cnt0 = jnp.zeros((1, ROUTER_LANES), F32)
        x1p, h2p, eid_p, gate_p, rank_p, cnt = _post_mix(xp, pieces_p, mod_p, lp, cnt0, tm=tm_p, rows_per_mod=seq)
        samp_st = {'lru_h': state_lru_h[l][:, None], 'lru_conv': state_lru_conv[l], 'sconv': state_sconv[l],
                   's5_re': state_s5_re[l].reshape(bs, 1, S5_N), 's5_im': state_s5_im[l].reshape(bs, 1, S5_N)}
        attn_s = lambda q, kf, vf, kb, vb, ex: _attn_sample(
            q, kf, vf, cache_win_k, cache_win_v, l, lw_s, lwn_s, lp['on_d'])
        pieces_s, st_s = _mixers(xs, mod_s, lp, samp_st, attn_s, batch=bs, seq=dec, tm=tm_s, tl=dec,
                                 rows_per_mod=dec)
        x1s, h2s, eid_s, gate_s, rank_s, cnt = _post_mix(xs, pieces_s, mod_s, lp, cnt, tm=tm_s, rows_per_mod=dec)
        dests, blk_e, n_used = _routing_tables(cnt, [eid_p, eid_s], [rank_p, rank_s], n_blk)
        slots = jnp.zeros((n_blk * MOE_ROWS * ROW_TILE, LANES), ROW_DT)
        slots = _dispatch(h2p, dests[0], slots, tm=tm_p)
        slots = _dispatch(h2s, dests[1], slots, tm=tm_s)
        ys = _experts(slots, blk_e, n_used, moe_w_gate, moe_w_up, moe_w_down, l)
        xp = _combine(ys, dests[0], x1p, gate_p, mod_p, tm=tm_p, rows_per_mod=seq)
        xs = _combine(ys, dests[1], x1s, gate_s, mod_s, tm=tm_s, rows_per_mod=dec)
        for n in names:
            acc_p[n].append(st_p[n])
            acc_s[n].append(st_s[n])
    new_p = {n: jnp.stack(acc_p[n], axis=0) for n in names}
    new_s = {n: jnp.stack(acc_s[n], axis=0) for n in names}
    return (xp.reshape(bp, seq, d), xs.reshape(bs, dec, d),
            new_p['lru_h'], new_p['lru_conv'], new_p['sconv'], new_p['s5_re'], new_p['s5_im'],
            new_p['win_k'], new_p['win_v'],
            new_s['lru_h'], new_s['lru_conv'], new_s['sconv'], new_s['s5_re'], new_s['s5_im'],
            new_s['win_k'], new_s['win_v'])
```

```python
import functools
import math

import numpy as np
import jax
import jax.numpy as jnp
from jax import lax
from jax.experimental import pallas as pl
from jax.experimental.pallas import tpu as pltpu

F32 = jnp.float32
MXU_DT = jnp.bfloat16
HIGHEST = lax.Precision.HIGHEST

D_MODEL = 1024
DEPTH = 4
W_GRP = 256
N_Z = 9
LRU_HEADS = 4
LRU_CONV = 4
LRU_C = 8.0
SCONV_W = 3
S5_CH = 16
S5_GROUPS = 16
S5_STATE = 64
S5_N = S5_GROUPS * S5_STATE
ATT_HEADS = 4
HEAD_DIM = 64
PATTERNS = ((128, 1), (512, 4), (2048, 16))
Q_BLK = 128
REL_BUCKETS = 32
REL_MAX_DIST = 2048
N_GROUPS = 4
EXP_PER_GROUP = 8
N_EXPERTS = 32
D_EXPERT = 512
EPS = 1e-6
NEG = -1e30

VMEM_LIMIT = 56 * 1024 * 1024
LANES = 128
MOE_ROWS = 512
SEGS = 8
ATTN_SUB_BLOCKS = 4


def _cparams(sem):
    return pltpu.CompilerParams(dimension_semantics=sem, vmem_limit_bytes=VMEM_LIMIT)


def _gelu(x):
    return 0.5 * x * (1.0 + jnp.tanh(math.sqrt(2.0 / math.pi) * (x + 0.044715 * (x * x * x))))


def _sigmoid(x):
    return 1.0 / (1.0 + jnp.exp(-x))


def _rms_rows(x):
    return x * lax.rsqrt(jnp.mean(x * x, axis=-1, keepdims=True) + EPS)


def _shift_rows(x, s, fill, row):
    return jnp.where(row >= s, pltpu.roll(x, s, 0), fill)


def _mod_body(c_ref, w_ref, b_ref, o_ref):
    c = c_ref[...]
    s = c * _sigmoid(c)
    o_ref[...] = jnp.dot(s.astype(MXU_DT), w_ref[...].astype(MXU_DT), preferred_element_type=F32) + b_ref[...]


def _modulation(c_all, mod_w, mod_b):
    nb = c_all.shape[0]
    depth = mod_w.shape[0]
    n_out = mod_w.shape[2]
    tn = D_MODEL
    return pl.pallas_call(
        _mod_body,
        grid=(depth, n_out // tn),
        in_specs=[pl.BlockSpec((nb, D_MODEL), lambda l, j: (0, 0)),
                  pl.BlockSpec((None, D_MODEL, tn), lambda l, j: (l, 0, j)),
                  pl.BlockSpec((None, 1, tn), lambda l, j: (l, 0, j))],
        out_specs=pl.BlockSpec((None, nb, tn), lambda l, j: (l, 0, j)),
        out_shape=jax.ShapeDtypeStruct((depth, nb, n_out), F32),
        compiler_params=_cparams(("arbitrary", "arbitrary")),
        name="modulation",
    )(c_all, mod_w, mod_b.reshape(depth, 1, n_out))


def _head_mean_sq(t, bones):
    sq = t * t
    hi = sq.astype(jnp.bfloat16)
    lo = (sq - hi.astype(F32)).astype(jnp.bfloat16)
    return (jnp.dot(hi, bones, preferred_element_type=F32)
            + jnp.dot(lo, bones, preferred_element_type=F32))


def _inproj_body(*refs, dils, kv_pos_minor):
    (x_ref, sc_ref, sh_ref, g_ref, w_ref, bones_ref, qg_ref, kg_ref,
     zm_ref, q_ref, kf_ref, vf_ref, kb_ref, vb_ref) = refs[:14]
    x = x_ref[...]
    h = _rms_rows(x) * g_ref[...]
    h = h * (1.0 + sc_ref[...]) + sh_ref[...]
    z = jnp.dot(h.astype(MXU_DT), w_ref[...], preferred_element_type=F32)
    nm = 6 * W_GRP
    zm_ref[...] = z[:, :nm]
    q = z[:, nm:nm + W_GRP]
    k = z[:, nm + W_GRP:nm + 2 * W_GRP]
    v = z[:, nm + 2 * W_GRP:]
    bones = bones_ref[...]
    qn = (q * lax.rsqrt(_head_mean_sq(q, bones) + EPS) * qg_ref[...]) * (HEAD_DIM ** -0.5)
    kn = k * lax.rsqrt(_head_mean_sq(k, bones) + EPS) * kg_ref[...]
    q_ref[...] = qn.astype(q_ref.dtype)
    if kv_pos_minor:
        kf_ref[...] = kn.T
        vf_ref[...] = v.T
    else:
        kf_ref[...] = kn
        vf_ref[...] = v
    kb_ref[...] = kn.astype(kb_ref.dtype)
    vb_ref[...] = v.astype(vb_ref.dtype)
    if dils:
        stage_ref = refs[-1]
        tm = x_ref.shape[0]
        for a, val in enumerate((qn, kn, v)):
            for half in range(W_GRP // LANES):
                stage_ref[a, half] = val[:, half * LANES:(half + 1) * LANES]
        for di, dil in enumerate(dils):
            for a in range(3):
                out_ref = refs[14 + 3 * di + a]
                for r in range(dil):
                    for half in range(W_GRP // LANES):
                        out_ref[r, :, half * LANES:(half + 1) * LANES] = (
                            stage_ref.at[a, half][pl.ds(r, tm // dil, stride=dil), :].astype(out_ref.dtype))


def _mod_spec(mod, k, tm, rows_per_mod):
    if mod.ndim == 4:
        return pl.BlockSpec((None, None, 1, D_MODEL), lambda i: (k, (i * tm) // rows_per_mod, 0, 0))
    return pl.BlockSpec((None, tm, D_MODEL), lambda i: (k, i, 0))


def _in_proj(x, mod, norm_g, w_in, bones, qg, kg, *, tm, rows_per_mod, dils=(), seq=None, kv_window=None):
    t = x.shape[0]
    n_in = w_in.shape[1]
    nm = 6 * W_GRP
    row = lambda i: (i, 0)
    const = lambda i: (0, 0)
    outs = [jax.ShapeDtypeStruct((t, nm), F32),
            jax.ShapeDtypeStruct((t, W_GRP), MXU_DT),
            jax.ShapeDtypeStruct((t, W_GRP), F32),
            jax.ShapeDtypeStruct((t, W_GRP), F32),
            jax.ShapeDtypeStruct((t, W_GRP), MXU_DT),
            jax.ShapeDtypeStruct((t, W_GRP), MXU_DT)]
    out_specs = [pl.BlockSpec((tm, nm), row)] + [pl.BlockSpec((tm, W_GRP), row)] * 5
    scratch = []
    if kv_window is not None:
        tiles_per_seq = seq // tm
        skip = (seq - kv_window) // tm
        assert (seq - kv_window) % tm == 0 and kv_window % tm == 0
        win = pl.BlockSpec((None, W_GRP, tm),
                           lambda i: (i // tiles_per_seq, 0, jnp.maximum(i % tiles_per_seq - skip, 0)))
        outs[2] = outs[3] = jax.ShapeDtypeStruct((t // seq, W_GRP, kv_window), F32)
        out_specs[2] = out_specs[3] = win
    if dils:
        tiles_per_seq = seq // tm
        for dil in dils:
            outs += [jax.ShapeDtypeStruct((t // seq, dil, seq // dil, W_GRP), MXU_DT)] * 3
            out_specs += [pl.BlockSpec((None, dil, tm // dil, W_GRP),
                                       lambda i: (i // tiles_per_seq, 0, i % tiles_per_seq, 0))] * 3
        scratch = [pltpu.VMEM((3, W_GRP // LANES, tm, LANES), F32)]
    return pl.pallas_call(
        functools.partial(_inproj_body, dils=tuple(dils), kv_pos_minor=kv_window is not None),
        grid=(t // tm,),
        in_specs=[pl.BlockSpec((tm, D_MODEL), row),
                  _mod_spec(mod, 1, tm, rows_per_mod),
                  _mod_spec(mod, 0, tm, rows_per_mod),
                  pl.BlockSpec((1, D_MODEL), const),
                  pl.BlockSpec((D_MODEL, n_in), const),
                  pl.BlockSpec((W_GRP, W_GRP), const),
                  pl.BlockSpec((1, W_GRP), const),
                  pl.BlockSpec((1, W_GRP), const)],
        out_specs=out_specs,
        out_shape=outs,
        scratch_shapes=scratch,
        compiler_params=_cparams(("arbitrary",)),
        name="in_proj",
    )(x, mod, mod, norm_g, w_in, bones, qg, kg)


def _softplus(x):
    return jnp.maximum(x, 0.0) + jnp.log(1.0 + jnp.exp(-jnp.abs(x)))


def _mixer_ab_body(xa_ref, ga_ref, gb_ref, gc_ref, xb_ref,
                   cw_ref, cb_ref, wa_ref, ba_ref, wx_ref, bx_ref, lam_ref, sw_ref, on_a_ref, on_b_ref,
                   h0_ref, conv0_ref, sconv0_ref,
                   oa_ref, ob_ref, hn_ref, convn_ref, sconvn_ref,
                   xe_ref, pe_ref, hc_ref):
    i = pl.program_id(1)
    tl = xa_ref.shape[0]

    @pl.when(i == 0)
    def _():
        xe_ref[8 - (LRU_CONV - 1):8, :] = conv0_ref[...]
        pe_ref[8 - (SCONV_W - 1):8, :] = sconv0_ref[...]
        hc_ref[...] = h0_ref[...]

    row = lax.broadcasted_iota(jnp.int32, (tl, 1), 0)
    xa = xa_ref[...]
    xe_ref[8:, :] = xa
    cw = cw_ref[...]
    xc = cw[LRU_CONV - 1:LRU_CONV, :] * xa
    for s in range(1, LRU_CONV):
        xc = xc + cw[LRU_CONV - 1 - s:LRU_CONV - s, :] * xe_ref[8 - s:8 - s + tl, :]
    xc = xc + cb_ref[...]
    convn_ref[...] = xa[tl - (LRU_CONV - 1):, :]
    xe_ref[0:8, :] = xa[tl - 8:, :]
    xcb = xc.astype(MXU_DT)
    r = _sigmoid(jnp.dot(xcb, wa_ref[...], preferred_element_type=F32) + ba_ref[...])
    ig = _sigmoid(jnp.dot(xcb, wx_ref[...], preferred_element_type=F32) + bx_ref[...])
    log_a = (-LRU_C * r) * _softplus(-lam_ref[...])
    a = jnp.exp(log_a)
    b = jnp.sqrt(-jnp.tanh(log_a) * (a * a + 1.0)) * (ig * xc)
    s = 1
    while s < tl:
        b = a * _shift_rows(b, s, 0.0, row) + b
        a = a * _shift_rows(a, s, 1.0, row)
        s *= 2
    h = b + a * hc_ref[...]
    hc_ref[...] = h[tl - 1:, :]
    hn_ref[...] = h[tl - 1:, :]
    out_a = h * _gelu(ga_ref[...])
    oa_ref[...] = (_rms_rows(out_a) * on_a_ref[...]).astype(oa_ref.dtype)
    p = gc_ref[...] * xb_ref[...]
    pe_ref[8:, :] = p
    sw = sw_ref[...]
    yb = sw[SCONV_W - 1:SCONV_W, :] * p
    for s in range(1, SCONV_W):
        yb = yb + sw[SCONV_W - 1 - s:SCONV_W - s, :] * pe_ref[8 - s:8 - s + tl, :]
    sconvn_ref[...] = p[tl - (SCONV_W - 1):, :]
    pe_ref[0:8, :] = p[tl - 8:, :]
    out_b = gb_ref[...] * yb
    ob_ref[...] = (_rms_rows(out_b) * on_b_ref[...]).astype(ob_ref.dtype)


def _mixer_ab(zm, lp, h0, conv0, sconv0, *, tl):
    bsz, seq, _ = zm.shape
    col = lambda c: pl.BlockSpec((None, tl, W_GRP), lambda b, i, c=c: (b, i, c))
    const = lambda shp: pl.BlockSpec(shp, lambda b, i: (0,) * len(shp))
    per_b = lambda n: pl.BlockSpec((None, n, W_GRP), lambda b, i: (b, 0, 0))
    outs = [jax.ShapeDtypeStruct((bsz, seq, W_GRP), MXU_DT),
            jax.ShapeDtypeStruct((bsz, seq, W_GRP), MXU_DT),
            jax.ShapeDtypeStruct((bsz, 1, W_GRP), F32),
            jax.ShapeDtypeStruct((bsz, LRU_CONV - 1, W_GRP), F32),
            jax.ShapeDtypeStruct((bsz, SCONV_W - 1, W_GRP), F32)]
    return pl.pallas_call(
        _mixer_ab_body,
        grid=(bsz, seq // tl),
        in_specs=[col(0), col(1), col(2), col(3), col(4),
                  const((LRU_CONV, W_GRP)), const((1, W_GRP)),
                  const((W_GRP, W_GRP)), const((1, W_GRP)),
                  const((W_GRP, W_GRP)), const((1, W_GRP)),
                  const((1, W_GRP)), const((SCONV_W, W_GRP)),
                  const((1, W_GRP)), const((1, W_GRP)),
                  per_b(1), per_b(LRU_CONV - 1), per_b(SCONV_W - 1)],
        out_specs=[pl.BlockSpec((None, tl, W_GRP), lambda b, i: (b, i, 0)),
                   pl.BlockSpec((None, tl, W_GRP), lambda b, i: (b, i, 0)),
                   per_b(1), per_b(LRU_CONV - 1), per_b(SCONV_W - 1)],
        out_shape=outs,
        scratch_shapes=[pltpu.VMEM((tl + 8, W_GRP), F32),
                        pltpu.VMEM((tl + 8, W_GRP), F32),
                        pltpu.VMEM((1, W_GRP), F32)],
        compiler_params=_cparams(("arbitrary", "arbitrary")),
        name="mixer_ab",
    )(zm, zm, zm, zm, zm,
      lp['lru_conv_w'], lp['lru_conv_b'], lp['lru_wa_blk'], lp['lru_ba'], lp['lru_wx_blk'], lp['lru_bx'],
      lp['lru_lambda'], lp['sconv_w'], lp['on_a'], lp['on_b'],
      h0, conv0, sconv0)


def _s5_body(u_ref, bb_ref, cre_ref, cim_ref, ar_ref, ai_ref, d_ref, gw_ref, gb_ref, on_ref,
             h0r_ref, h0i_ref,
             o_ref, hnr_ref, hni_ref,
             tr_ref, ti_ref, pr_ref, pi_ref, hr_ref, hi_ref, lr_ref, li_ref, stage_ref):
    b = pl.program_id(0)
    i = pl.program_id(1)
    tl = u_ref.shape[0]
    steps = tl // SEGS
    groups = S5_N // LANES
    halves = W_GRP // LANES
    ar = ar_ref[...]
    ai = ai_ref[...]

    def powers(base_r, base_i, n, first):
        row = lax.broadcasted_iota(jnp.int32, (n, 1), 0)
        tr = jnp.broadcast_to(base_r, (n, S5_N))
        ti = jnp.broadcast_to(base_i, (n, S5_N))
        s = first
        while s < n:
            sr = _shift_rows(tr, s, 1.0, row)
            si = _shift_rows(ti, s, 0.0, row)
            tr, ti = tr * sr - ti * si, tr * si + ti * sr
            s *= 2
        return tr, ti

    @pl.when((b == 0) & (i == 0))
    def _():
        tr, ti = powers(ar, ai, tl, SEGS)
        tr_ref[...] = tr
        ti_ref[...] = ti
        pr, pi = powers(tr[tl - 1:, :], ti[tl - 1:, :], SEGS, 1)
        pr_ref[...] = pr
        pi_ref[...] = pi

    @pl.when(i == 0)
    def _():
        hr_ref[...] = h0r_ref[...]
        hi_ref[...] = h0i_ref[...]

    u = u_ref[...]
    if steps > 1:
        for half in range(halves):
            stage_ref[half] = u[:, half * LANES:(half + 1) * LANES]
        u = jnp.concatenate(
            [jnp.concatenate([stage_ref.at[half][pl.ds(j, SEGS, stride=steps), :] for half in range(halves)], axis=1)
             for j in range(steps)], axis=0)
    bu = jnp.dot(u.astype(MXU_DT), bb_ref[...], preferred_element_type=F32)
    a_r = [jnp.broadcast_to(ar[:, c * LANES:(c + 1) * LANES], (SEGS, LANES)) for c in range(groups)]
    a_i = [jnp.broadcast_to(ai[:, c * LANES:(c + 1) * LANES], (SEGS, LANES)) for c in range(groups)]
    loc_r = [jnp.zeros((SEGS, LANES), F32) for _ in range(groups)]
    loc_i = [jnp.zeros((SEGS, LANES), F32) for _ in range(groups)]
    for j in range(steps):
        rows = slice(j * SEGS, (j + 1) * SEGS)
        for c in range(groups):
            cols = slice(c * LANES, (c + 1) * LANES)
            nr = (a_r[c] * loc_r[c] - a_i[c] * loc_i[c]) + bu[rows, c * LANES:(c + 1) * LANES]
            ni = (a_r[c] * loc_i[c] + a_i[c] * loc_r[c]) + bu[rows, S5_N + c * LANES:S5_N + (c + 1) * LANES]
            loc_r[c], loc_i[c] = nr, ni
            lr_ref[rows, cols] = nr
            li_ref[rows, cols] = ni
    er = jnp.concatenate(loc_r, axis=1)
    ei = jnp.concatenate(loc_i, axis=1)
    seg = lax.broadcasted_iota(jnp.int32, (SEGS, 1), 0)
    pr = pr_ref[...]
    pi = pi_ref[...]
    mr, mi = pr[0:1, :], pi[0:1, :]
    s = 1
    while s < SEGS:
        sr = _shift_rows(er, s, 0.0, seg)
        si = _shift_rows(ei, s, 0.0, seg)
        er, ei = er + (mr * sr - mi * si), ei + (mr * si + mi * sr)
        mr, mi = mr * mr - mi * mi, 2.0 * (mr * mi)
        s *= 2
    cr = hr_ref[...]
    ci = hi_ref[...]
    er, ei = er + (pr * cr - pi * ci), ei + (pr * ci + pi * cr)
    in_r = jnp.where(seg >= 1, pltpu.roll(er, 1, 0), cr)
    in_i = jnp.where(seg >= 1, pltpu.roll(ei, 1, 0), ci)
    hr_ref[...] = er[SEGS - 1:, :]
    hi_ref[...] = ei[SEGS - 1:, :]
    hnr_ref[...] = er[SEGS - 1:, :]
    hni_ref[...] = ei[SEGS - 1:, :]
    tr = tr_ref[...]
    ti = ti_ref[...]
    sr = jnp.tile(in_r, (steps, 1))
    si = jnp.tile(in_i, (steps, 1))
    hr = lr_ref[...] + (tr * sr - ti * si)
    hi = li_ref[...] + (tr * si + ti * sr)
    y = (jnp.dot(hr.astype(MXU_DT), cre_ref[...], preferred_element_type=F32)
         - jnp.dot(hi.astype(MXU_DT), cim_ref[...], preferred_element_type=F32)) + d_ref[...] * u
    g = jnp.dot(_gelu(y).astype(MXU_DT), gw_ref[...], preferred_element_type=F32) + gb_ref[...]
    out = g[:, :W_GRP] * _sigmoid(g[:, W_GRP:])
    out = _rms_rows(out) * on_ref[...]
    if steps > 1:
        for half in range(halves):
            stage_ref[half] = out[:, half * LANES:(half + 1) * LANES]
        out = jnp.concatenate(
            [jnp.concatenate([stage_ref.at[half][pl.ds(sg, steps, stride=SEGS), :] for half in range(halves)], axis=1)
             for sg in range(SEGS)], axis=0)
    o_ref[...] = out.astype(o_ref.dtype)


def _s5_mixer(zm, lp, h0r, h0i, *, tl):
    bsz, seq, _ = zm.shape
    const = lambda shp: pl.BlockSpec(shp, lambda b, i: (0,) * len(shp))
    per_b = pl.BlockSpec((None, 1, S5_N), lambda b, i: (b, 0, 0))
    outs = [jax.ShapeDtypeStruct((bsz, seq, W_GRP), MXU_DT),
            jax.ShapeDtypeStruct((bsz, 1, S5_N), F32),
            jax.ShapeDtypeStruct((bsz, 1, S5_N), F32)]
    return pl.pallas_call(
        _s5_body,
        grid=(bsz, seq // tl),
        in_specs=[pl.BlockSpec((None, tl, W_GRP), lambda b, i: (b, i, 5)),
                  const((W_GRP, 2 * S5_N)), const((S5_N, W_GRP)), const((S5_N, W_GRP)),
                  const((1, S5_N)), const((1, S5_N)), const((1, W_GRP)),
                  const((W_GRP, 2 * W_GRP)), const((1, 2 * W_GRP)), const((1, W_GRP)),
                  per_b, per_b],
        out_specs=[pl.BlockSpec((None, tl, W_GRP), lambda b, i: (b, i, 0)), per_b, per_b],
        out_shape=outs,
        scratch_shapes=[pltpu.VMEM((tl, S5_N), F32), pltpu.VMEM((tl, S5_N), F32),
                        pltpu.VMEM((SEGS, S5_N), F32), pltpu.VMEM((SEGS, S5_N), F32),
                        pltpu.VMEM((1, S5_N), F32), pltpu.VMEM((1, S5_N), F32),
                        pltpu.VMEM((tl, S5_N), F32), pltpu.VMEM((tl, S5_N), F32),
                        pltpu.VMEM((W_GRP // LANES, tl, LANES), F32)],
        compiler_params=_cparams(("arbitrary", "arbitrary")),
        name="s5_mixer",
    )(zm, lp['s5_bb'], lp['s5_cre'], lp['s5_cim'], lp['s5_abr'], lp['s5_abi'], lp['s5_d'],
      lp['s5_glu_w'], lp['s5_glu_b'], lp['on_c'], h0r, h0i)


CODE_MASKED = -1
CODE_ZERO = -2


def _bias_body(rb_ref, code_ref, o_ref):
    code = code_ref[...]
    acc = jnp.where(code == CODE_MASKED, NEG, 0.0).astype(F32)
    for c in range(REL_BUCKETS * ATT_HEADS):
        acc = jnp.where(code == c, rb_ref[c], acc)
    o_ref[...] = acc


def _bias_table(rel_bias, codes):
    rows, cols = codes.shape
    tr = max(t for t in range(8, 513, 8) if rows % t == 0)
    return pl.pallas_call(
        _bias_body,
        grid_spec=pltpu.PrefetchScalarGridSpec(
            num_scalar_prefetch=1,
            grid=(rows // tr,),
            in_specs=[pl.BlockSpec((tr, cols), lambda i, rb: (i, 0))],
            out_specs=pl.BlockSpec((tr, cols), lambda i, rb: (i, 0))),
        out_shape=jax.ShapeDtypeStruct((rows, cols), F32),
        compiler_params=_cparams(("arbitrary",)),
        name="bias_table",
    )(rel_bias.reshape(-1), jnp.asarray(codes))


def _t5_bucket(n):
    n = np.asarray(n).astype(np.int32)
    max_exact = REL_BUCKETS // 2
    nf = np.maximum(n, 1).astype(np.float32)
    large = max_exact + (np.log(nf / max_exact) / np.log(REL_MAX_DIST / max_exact)
                         * (REL_BUCKETS - max_exact)).astype(np.int32)
    large = np.minimum(large, REL_BUCKETS - 1)
    return np.where(n < max_exact, n, large).astype(np.int32)


def _prompt_bias_codes():
    i = np.arange(Q_BLK)[:, None]
    j = np.arange(2 * Q_BLK)[None, :]
    rel = Q_BLK + i - j
    out = np.zeros((len(PATTERNS), 2, ATT_HEADS, Q_BLK, 2 * Q_BLK), np.int32)
    for p, (win, dil) in enumerate(PATTERNS):
        span = win // dil
        valid = (rel >= 0) & (rel <= span)
        bucket = _t5_bucket(np.clip(rel, 0, None) * dil)
        for var in range(2):
            v = valid & ((j >= Q_BLK) | (var == 1))
            for h in range(ATT_HEADS):
                out[p, var, h] = np.where(v, bucket * ATT_HEADS + h, CODE_MASKED)
    return out.reshape(-1, 2 * Q_BLK)


def _sample_bias_codes(dec_seq, wb):
    def tables(n_keys, first_pos):
        codes = np.full((ATT_HEADS, dec_seq, n_keys), CODE_MASKED, np.int32)
        logm = np.zeros((ATT_HEADS, dec_seq, n_keys), np.float32)
        for s in range(dec_seq):
            dist = (wb + s) - (first_pos + np.arange(n_keys))
            mult = np.zeros(n_keys, np.int32)
            for win, dil in PATTERNS:
                mult += ((dist >= 0) & (dist % dil == 0) & (dist // dil <= win // dil)).astype(np.int32)
            bucket = _t5_bucket(np.clip(dist, 0, None))
            for h in range(ATT_HEADS):
                codes[h, s] = np.where(mult > 0, bucket * ATT_HEADS + h, CODE_MASKED)
                logm[h, s] = np.log(np.maximum(mult, 1))
        return codes.reshape(ATT_HEADS * dec_seq, n_keys), logm.reshape(ATT_HEADS * dec_seq, n_keys)
    return tables(wb, 0), tables(dec_seq, wb)


def _attn_p_body(q_ref, kp_ref, kc_ref, vp_ref, vc_ref, bias_ref, o_ref, lse_ref):
    lane = lax.broadcasted_iota(jnp.int32, (1, W_GRP), 1)
    head_of_lane = [(lane >= h * HEAD_DIM) & (lane < (h + 1) * HEAD_DIM) for h in range(ATT_HEADS)]
    n_sub = q_ref.shape[0] // Q_BLK
    first_variant = jnp.minimum(pl.program_id(2), 1)
    for sub in range(n_sub):
        rows = slice(sub * Q_BLK, (sub + 1) * Q_BLK)
        q = q_ref[rows, :]
        if sub == 0:
            k_prev, v_prev, bias = kp_ref[...], vp_ref[...], bias_ref[first_variant]
        else:
            prev_rows = slice((sub - 1) * Q_BLK, sub * Q_BLK)
            k_prev, v_prev, bias = kc_ref[prev_rows, :], vc_ref[prev_rows, :], bias_ref[1]
        k2 = jnp.concatenate([k_prev, kc_ref[rows, :]], axis=0)
        v2 = jnp.concatenate([v_prev, vc_ref[rows, :]], axis=0)
        qs = jnp.concatenate([jnp.where(hm, q, jnp.zeros_like(q)) for hm in head_of_lane], axis=0)
        s = lax.dot_general(qs, k2, (((1,), (1,)), ((), ())), preferred_element_type=F32) + bias
        m = jnp.max(s, axis=-1, keepdims=True)
        pr = jnp.exp(s - m)
        den = jnp.sum(pr, axis=-1, keepdims=True)
        pv = jnp.dot(pr.astype(MXU_DT), v2, preferred_element_type=F32) / den
        lse_rows = m + jnp.log(den)
        o = jnp.zeros((Q_BLK, W_GRP), F32)
        lse = jnp.zeros((Q_BLK, W_GRP), F32)
        for h, hm in enumerate(head_of_lane):
            o = jnp.where(hm, pv[h * Q_BLK:(h + 1) * Q_BLK], o)
            lse = jnp.where(hm, lse_rows[h * Q_BLK:(h + 1) * Q_BLK], lse)
        o_ref[rows, :] = o
        lse_ref[rows, :] = lse


def _attn_prompt_pattern(q, k, v, bias):
    bsz, dil, md, _ = q.shape
    n_sub = math.gcd(ATTN_SUB_BLOCKS, md // Q_BLK)
    cur = pl.BlockSpec((None, None, n_sub * Q_BLK, W_GRP), lambda b, r, n: (b, r, n, 0))
    prv = pl.BlockSpec((None, None, Q_BLK, W_GRP), lambda b, r, n: (b, r, jnp.maximum(n * n_sub - 1, 0), 0))
    bsp = pl.BlockSpec((2, ATT_HEADS * Q_BLK, 2 * Q_BLK), lambda b, r, n: (0, 0, 0))
    return pl.pallas_call(
        _attn_p_body,
        grid=(bsz, dil, md // (n_sub * Q_BLK)),
        in_specs=[cur, prv, cur, prv, cur, bsp],
        out_specs=[cur, cur],
        out_shape=[jax.ShapeDtypeStruct((bsz, dil, md, W_GRP), F32)] * 2,
        compiler_params=_cparams(("arbitrary", "arbitrary", "arbitrary")),
        name=f"attn_prompt_d{dil}",
    )(q, k, k, v, v, bias)


def _attn_merge_body(*refs, dils):
    n_pat = len(dils)
    on_ref, out_ref, stage_ref = refs[2 * n_pat:]
    tm = out_ref.shape[0]
    halves = W_GRP // LANES
    vals = []
    for t in range(2 * n_pat):
        dil = dils[t // 2]
        if dil == 1:
            vals.append(refs[t][...])
            continue
        for r in range(dil):
            blk = refs[t][r]
            for half in range(halves):
                stage_ref.at[t, half][pl.ds(r, tm // dil, stride=dil), :] = blk[:, half * LANES:(half + 1) * LANES]
        vals.append(jnp.concatenate([stage_ref[t, half] for half in range(halves)], axis=1))
    os_, ls_ = vals[0::2], vals[1::2]
    mx = functools.reduce(jnp.maximum, ls_)
    ws = [jnp.exp(l_ - mx) for l_ in ls_]
    num = functools.reduce(lambda a_, b_: a_ + b_, [w_ * o_ for w_, o_ in zip(ws, os_)])
    merged = num / functools.reduce(lambda a_, b_: a_ + b_, ws)
    out_ref[...] = (_rms_rows(merged) * on_ref[...]).astype(out_ref.dtype)


def _attn_merge(pattern_outs, on_d, *, tm):
    dils = tuple(o.shape[1] for o, _ in pattern_outs)
    bsz, _, seq, _ = pattern_outs[0][0].shape
    seq = seq * dils[0]
    specs, args = [], []
    for (o, lse), dil in zip(pattern_outs, dils):
        if dil == 1:
            sp = pl.BlockSpec((None, None, tm, W_GRP), lambda b, j: (b, 0, j, 0))
        else:
            sp = pl.BlockSpec((None, dil, tm // dil, W_GRP), lambda b, j: (b, 0, j, 0))
        specs += [sp, sp]
        args += [o, lse]
    return pl.pallas_call(
        functools.partial(_attn_merge_body, dils=dils),
        grid=(bsz, seq // tm),
        in_specs=specs + [pl.BlockSpec((1, W_GRP), lambda b, j: (0, 0))],
        out_specs=pl.BlockSpec((None, tm, W_GRP), lambda b, j: (b, j, 0)),
        out_shape=jax.ShapeDtypeStruct((bsz, seq, W_GRP), MXU_DT),
        scratch_shapes=[pltpu.VMEM((2 * len(dils), W_GRP // LANES, tm, LANES), F32)],
        compiler_params=_cparams(("arbitrary", "arbitrary")),
        name="attn_merge",
    )(*args, on_d)


def _attn_prompt(qkv_by_dil, bias_all, on_d, *, tm):
    outs = [_attn_prompt_pattern(q, k, v, bias_all[p]) for p, (q, k, v) in enumerate(qkv_by_dil)]
    return _attn_merge(outs, on_d, tm=tm)


def _attn_s_body(q_ref, kn_ref, vn_ref, kt_ref, vt_ref, lw_ref, lwn_ref, on_ref, o_ref):
    q = q_ref[...]
    kn = kn_ref[...].astype(MXU_DT)
    vn = vn_ref[...].astype(MXU_DT)
    nt = (((1,), (1,)), ((), ()))
    outs = []
    for h in range(ATT_HEADS):
        cols = slice(h * HEAD_DIM, (h + 1) * HEAD_DIM)
        qh = q[:, cols]
        sc = jnp.dot(qh, kt_ref[h].astype(MXU_DT), preferred_element_type=F32) + lw_ref[h]
        scn = lax.dot_general(qh, kn[:, cols], nt, preferred_element_type=F32) + lwn_ref[h]
        m = jnp.maximum(jnp.max(sc, axis=-1, keepdims=True), jnp.max(scn, axis=-1, keepdims=True))
        p = jnp.exp(sc - m)
        pn = jnp.exp(scn - m)
        den = jnp.sum(p, axis=-1, keepdims=True) + jnp.sum(pn, axis=-1, keepdims=True)
        num = (lax.dot_general(p.astype(MXU_DT), vt_ref[h].astype(MXU_DT), nt, preferred_element_type=F32)
               + jnp.dot(pn.astype(MXU_DT), vn[:, cols], preferred_element_type=F32))
        outs.append(num / den)
    merged = jnp.concatenate(outs, axis=1)
    o_ref[...] = (_rms_rows(merged) * on_ref[...]).astype(o_ref.dtype)


def _attn_sample(q, kn, vn, cache_k, cache_v, l, lw, lwn, on_d):
    bsz, dec, _ = q.shape
    wb = cache_k.shape[2]
    per_b = pl.BlockSpec((None, dec, W_GRP), lambda b: (b, 0, 0))
    pos_minor = lambda c: jnp.transpose(c, (0, 1, 3, 4, 2))
    cache_spec = pl.BlockSpec((None, None, ATT_HEADS, HEAD_DIM, wb), lambda b: (l, b, 0, 0, 0))
    full = lambda a: pl.BlockSpec(a.shape, lambda b: (0,) * a.ndim)
    return pl.pallas_call(
        _attn_s_body,
        grid=(bsz,),
        in_specs=[per_b, per_b, per_b, cache_spec, cache_spec, full(lw), full(lwn), full(on_d)],
        out_specs=per_b,
        out_shape=jax.ShapeDtypeStruct((bsz, dec, W_GRP), MXU_DT),
        compiler_params=_cparams(("arbitrary",)),
        name="attn_sample",
    )(q, kn, vn, pos_minor(cache_k), pos_minor(cache_v), lw, lwn, on_d)


ROUTER_LANES = 128
HALF_D = D_MODEL // 2
ROW_TILE = HALF_D // LANES
ROW_DT = jnp.uint32


def _store_row_tiles(ref, val):
    n = val.shape[0]
    bits = lambda t: lax.bitcast_convert_type(t.astype(jnp.bfloat16).astype(F32), ROW_DT)
    packed = (bits(val[:, HALF_D:]) & jnp.uint32(0xFFFF0000)) | (bits(val[:, :HALF_D]) >> 16)
    for j in range(ROW_TILE):
        ref[pl.ds(j, n, stride=ROW_TILE), :] = packed[:, j * LANES:(j + 1) * LANES]


def _load_row_tiles(ref):
    n = ref.shape[0] // ROW_TILE
    packed = jnp.concatenate([ref[pl.ds(j, n, stride=ROW_TILE), :] for j in range(ROW_TILE)], axis=1)
    return (lax.bitcast_convert_type(packed << 16, F32),
            lax.bitcast_convert_type(packed & jnp.uint32(0xFFFF0000), F32))


OUT_TILE = D_MODEL // LANES


def _store_out_tiles(ref, val):
    n = val.shape[0]
    for j in range(OUT_TILE):
        ref[pl.ds(j, n, stride=OUT_TILE), :] = val[:, j * LANES:(j + 1) * LANES]


def _load_out_tiles(ref):
    n = ref.shape[0] // OUT_TILE
    return jnp.concatenate([ref[pl.ds(j, n, stride=OUT_TILE), :] for j in range(OUT_TILE)], axis=1)


def _post_mix_body(x_ref, a_ref, b_ref, c_ref, d_ref, w_ref, g1_ref, sc_ref, sh_ref, g_ref,
                   wr_ref, br_ref, tri_ref, cnt_in_ref,
                   x1_ref, h2_ref, eid_ref, gate_ref, rank_ref, cnt_ref,
                   run_ref):
    i = pl.program_id(0)

    @pl.when(i == 0)
    def _():
        run_ref[...] = cnt_in_ref[...]

    mix = jnp.concatenate([a_ref[...], b_ref[...], c_ref[...], d_ref[...]], axis=1)
    y = jnp.dot(mix, w_ref[...], preferred_element_type=F32)
    x1 = x_ref[...] + g1_ref[...] * y
    x1_ref[...] = x1
    h2 = _rms_rows(x1) * g_ref[...]
    h2 = h2 * (1.0 + sc_ref[...]) + sh_ref[...]
    _store_row_tiles(h2_ref, h2)
    logits = jnp.dot(h2.astype(MXU_DT), wr_ref[...], preferred_element_type=F32) + br_ref[...]
    tm = logits.shape[0]
    lane = lax.broadcasted_iota(jnp.int32, (tm, ROUTER_LANES), 1)
    big = jnp.int32(10 ** 6)
    is_g = lane < N_GROUPS
    gl = jnp.where(is_g, logits, -jnp.inf)
    gmax = jnp.max(gl, axis=-1, keepdims=True)
    gsel = jnp.min(jnp.where(gl == gmax, lane, big), axis=-1, keepdims=True)
    gprob = 1.0 / jnp.sum(jnp.where(is_g, jnp.exp(logits - gmax), 0.0), axis=-1, keepdims=True)
    lo_lane = N_GROUPS + gsel * EXP_PER_GROUP
    in_grp = (lane >= lo_lane) & (lane < lo_lane + EXP_PER_GROUP)
    el = jnp.where(in_grp, logits, -jnp.inf)
    v1 = jnp.max(el, axis=-1, keepdims=True)
    i1 = jnp.min(jnp.where(el == v1, lane, big), axis=-1, keepdims=True)
    el2 = jnp.where(lane == i1, -jnp.inf, el)
    v2 = jnp.max(el2, axis=-1, keepdims=True)
    i2 = jnp.min(jnp.where(el2 == v2, lane, big), axis=-1, keepdims=True)
    e2w = jnp.exp(v2 - v1)
    gate1 = (1.0 / (1.0 + e2w)) * gprob
    gate2 = (e2w / (1.0 + e2w)) * gprob
    e1 = i1 - N_GROUPS
    e2 = i2 - N_GROUPS
    oh1 = lane == e1
    oh2 = lane == e2
    both = jnp.where(oh1 | oh2, 1.0, 0.0)
    before = jnp.dot(tri_ref[...], both.astype(jnp.bfloat16), preferred_element_type=F32) + run_ref[...]
    r1 = jnp.sum(jnp.where(oh1, before, 0.0), axis=-1, keepdims=True).astype(jnp.int32)
    r2 = jnp.sum(jnp.where(oh2, before, 0.0), axis=-1, keepdims=True).astype(jnp.int32)
    run = run_ref[...] + jnp.sum(both, axis=0, keepdims=True)
    run_ref[...] = run
    cnt_ref[...] = run
    eid_ref[...] = jnp.where(lane == 0, e1, jnp.where(lane == 1, e2, 0))
    gate_ref[...] = jnp.where(lane == 0, gate1, jnp.where(lane == 1, gate2, 0.0))
    rank_ref[...] = jnp.where(lane == 0, r1, jnp.where(lane == 1, r2, 0))


def _post_mix(x, pieces, mod, lp, cnt_in, *, tm, rows_per_mod):
    t = x.shape[0]
    row = lambda i: (i, 0)
    const = lambda i: (0, 0)
    piece = pl.BlockSpec((tm, W_GRP), row)
    wide = pl.BlockSpec((tm, D_MODEL), row)
    lanes = pl.BlockSpec((tm, ROUTER_LANES), row)
    outs = [jax.ShapeDtypeStruct((t, D_MODEL), F32), jax.ShapeDtypeStruct((t * ROW_TILE, LANES), ROW_DT),
            jax.ShapeDtypeStruct((t, ROUTER_LANES), jnp.int32), jax.ShapeDtypeStruct((t, ROUTER_LANES), F32),
            jax.ShapeDtypeStruct((t, ROUTER_LANES), jnp.int32), jax.ShapeDtypeStruct((1, ROUTER_LANES), F32)]
    return pl.pallas_call(
        _post_mix_body,
        grid=(t // tm,),
        in_specs=[wide, piece, piece, piece, piece,
                  pl.BlockSpec((D_MODEL, D_MODEL), const),
                  _mod_spec(mod, 2, tm, rows_per_mod), _mod_spec(mod, 4, tm, rows_per_mod),
                  _mod_spec(mod, 3, tm, rows_per_mod),
                  pl.BlockSpec((1, D_MODEL), const),
                  pl.BlockSpec((D_MODEL, ROUTER_LANES), const), pl.BlockSpec((1, ROUTER_LANES), const),
                  pl.BlockSpec((tm, tm), const), pl.BlockSpec((1, ROUTER_LANES), const)],
        out_specs=[wide, pl.BlockSpec((tm * ROW_TILE, LANES), row), lanes, lanes, lanes,
                   pl.BlockSpec((1, ROUTER_LANES), const)],
        out_shape=outs,
        scratch_shapes=[pltpu.VMEM((1, ROUTER_LANES), F32)],
        compiler_params=_cparams(("arbitrary",)),
        name="post_mix",
    )(x, *pieces, lp['w_out'], mod, mod, mod, lp['norm_ffn'], lp['router_w'], lp['router_b'], lp['tri'], cnt_in)


def _row_copy(src_ref, s, dst_ref, d, sem, tile=ROW_TILE):
    return pltpu.make_async_copy(src_ref.at[pl.ds(pl.multiple_of(s * tile, tile), tile)],
                                 dst_ref.at[pl.ds(pl.multiple_of(d * tile, tile), tile)], sem)


def _dispatch_body(dest_ref, h_ref, xs_in_ref, xs_ref, sem):
    del xs_in_ref
    tm = h_ref.shape[0] // ROW_TILE
    base = pl.program_id(0) * (2 * tm)

    def issue(t, c):
        _row_copy(h_ref, t, xs_ref, dest_ref[base + 2 * t], sem).start()
        _row_copy(h_ref, t, xs_ref, dest_ref[base + 2 * t + 1], sem).start()
        return c

    lax.fori_loop(0, tm, issue, 0)
    for _ in range(2):
        pltpu.make_async_copy(h_ref, xs_ref.at[pl.ds(0, tm * ROW_TILE)], sem).wait()


def _dispatch(h2, dest_flat, xs, *, tm):
    t = h2.shape[0] // ROW_TILE
    return pl.pallas_call(
        _dispatch_body,
        grid_spec=pltpu.PrefetchScalarGridSpec(
            num_scalar_prefetch=1,
            grid=(t // tm,),
            in_specs=[pl.BlockSpec((tm * ROW_TILE, LANES), lambda i, d: (i, 0)),
                      pl.BlockSpec(memory_space=pl.ANY)],
            out_specs=pl.BlockSpec(memory_space=pl.ANY),
            scratch_shapes=[pltpu.SemaphoreType.DMA(())]),
        out_shape=jax.ShapeDtypeStruct(xs.shape, xs.dtype),
        input_output_aliases={2: 0},
        compiler_params=_cparams(("arbitrary",)),
        name="moe_dispatch",
    )(dest_flat, h2, xs)


def _experts_body(blk_e_ref, n_used_ref, xs_ref, wg_ref, wu_ref, wd_ref, o_ref, wgb_ref, wub_ref, wdb_ref):
    i = pl.program_id(0)
    e = blk_e_ref[i]
    e_prev = blk_e_ref[jnp.maximum(i - 1, 0)]

    @pl.when((i == 0) | (e != e_prev))
    def _():
        wgb_ref[...] = wg_ref[...].astype(MXU_DT)
        wub_ref[...] = wu_ref[...].astype(MXU_DT)
        wdb_ref[...] = wd_ref[...].astype(MXU_DT)

    @pl.when(i < n_used_ref[0])
    def _():
        x_lo, x_hi = [t.astype(MXU_DT) for t in _load_row_tiles(xs_ref)]
        g = (jnp.dot(x_lo, wgb_ref[:HALF_D, :], preferred_element_type=F32)
             + jnp.dot(x_hi, wgb_ref[HALF_D:, :], preferred_element_type=F32))
        u = (jnp.dot(x_lo, wub_ref[:HALF_D, :], preferred_element_type=F32)
             + jnp.dot(x_hi, wub_ref[HALF_D:, :], preferred_element_type=F32))
        hmid = (g * _sigmoid(g)) * u
        _store_out_tiles(o_ref, jnp.dot(hmid.astype(MXU_DT), wdb_ref[...], preferred_element_type=F32))

    @pl.when(i >= n_used_ref[0])
    def _():
        o_ref[...] = jnp.zeros_like(o_ref)


def _experts(xs, blk_e, n_used, wg, wu, wd, l):
    n_blk = xs.shape[0] // (MOE_ROWS * ROW_TILE)
    xmap = lambda i, be, nu: (jnp.minimum(i, nu[0] - 1), 0)
    return pl.pallas_call(
        _experts_body,
        grid_spec=pltpu.PrefetchScalarGridSpec(
            num_scalar_prefetch=2,
            grid=(n_blk,),
            in_specs=[pl.BlockSpec((MOE_ROWS * ROW_TILE, LANES), xmap),
                      pl.BlockSpec((None, None, D_MODEL, D_EXPERT), lambda i, be, nu: (l, be[i], 0, 0)),
                      pl.BlockSpec((None, None, D_MODEL, D_EXPERT), lambda i, be, nu: (l, be[i], 0, 0)),
                      pl.BlockSpec((None, None, D_EXPERT, D_MODEL), lambda i, be, nu: (l, be[i], 0, 0))],
            out_specs=pl.BlockSpec((MOE_ROWS * OUT_TILE, LANES), lambda i, be, nu: (i, 0)),
            scratch_shapes=[pltpu.VMEM((D_MODEL, D_EXPERT), MXU_DT), pltpu.VMEM((D_MODEL, D_EXPERT), MXU_DT),
                            pltpu.VMEM((D_EXPERT, D_MODEL), MXU_DT)]),
        out_shape=jax.ShapeDtypeStruct((n_blk * MOE_ROWS * OUT_TILE, LANES), F32),
        compiler_params=_cparams(("arbitrary",)),
        name="moe_experts",
    )(blk_e, n_used, xs, wg, wu, wd)


def _combine_body(dest_ref, ys_ref, x1_ref, gate_ref, g2_ref, x2_ref, a0_ref, a1_ref, b0_ref, b1_ref, sem):
    tm = x1_ref.shape[0]
    i = pl.program_id(0)
    n = pl.num_programs(0)
    sets = ((a0_ref, a1_ref, sem.at[0]), (b0_ref, b1_ref, sem.at[1]))

    def gather(tile, bufs):
        k0_ref, k1_ref, s = bufs
        base = tile * (2 * tm)

        def issue(t, c):
            _row_copy(ys_ref, dest_ref[base + 2 * t], k0_ref, t, s, OUT_TILE).start()
            _row_copy(ys_ref, dest_ref[base + 2 * t + 1], k1_ref, t, s, OUT_TILE).start()
            return c

        lax.fori_loop(0, tm, issue, 0)

    def step(cur, nxt):
        @pl.when(i + 1 < n)
        def _():
            gather(i + 1, nxt)

        k0_ref, k1_ref, s = cur
        for buf_ref in (k0_ref, k1_ref):
            pltpu.make_async_copy(ys_ref.at[pl.ds(0, tm * OUT_TILE)], buf_ref, s).wait()
        gate = gate_ref[...]
        y = _load_out_tiles(k0_ref) * gate[:, 0:1] + _load_out_tiles(k1_ref) * gate[:, 1:2]
        x2_ref[...] = x1_ref[...] + g2_ref[...] * y

    @pl.when(i == 0)
    def _():
        gather(0, sets[0])

    @pl.when(i % 2 == 0)
    def _():
        step(sets[0], sets[1])

    @pl.when(i % 2 == 1)
    def _():
        step(sets[1], sets[0])


def _combine(ys, dest_flat, x1, gate, mod, *, tm, rows_per_mod):
    t = x1.shape[0]
    if mod.ndim == 4:
        g2_spec = pl.BlockSpec((None, None, 1, D_MODEL), lambda i, d: (5, (i * tm) // rows_per_mod, 0, 0))
    else:
        g2_spec = pl.BlockSpec((None, tm, D_MODEL), lambda i, d: (5, i, 0))
    return pl.pallas_call(
        _combine_body,
        grid_spec=pltpu.PrefetchScalarGridSpec(
            num_scalar_prefetch=1,
            grid=(t // tm,),
            in_specs=[pl.BlockSpec(memory_space=pl.ANY),
                      pl.BlockSpec((tm, D_MODEL), lambda i, d: (i, 0)),
                      pl.BlockSpec((tm, ROUTER_LANES), lambda i, d: (i, 0)),
                      g2_spec],
            out_specs=pl.BlockSpec((tm, D_MODEL), lambda i, d: (i, 0)),
            scratch_shapes=[pltpu.VMEM((tm * OUT_TILE, LANES), F32)] * 4 + [pltpu.SemaphoreType.DMA((2,))]),
        out_shape=jax.ShapeDtypeStruct((t, D_MODEL), F32),
        compiler_params=_cparams(("arbitrary",)),
        name="moe_combine",
    )(dest_flat, ys, x1, gate, mod)


def _routing_tables(cnt, eids, ranks, n_blk):
    counts = cnt[0, :N_EXPERTS].astype(jnp.int32)
    padded = (counts + MOE_ROWS - 1) // MOE_ROWS * MOE_ROWS
    ends = jnp.cumsum(padded)
    starts = ends - padded
    experts = jnp.arange(N_EXPERTS, dtype=jnp.int32)
    start_of = lambda e: jnp.sum(jnp.where(e[..., None] == experts, starts, 0), axis=-1)
    dests = [(start_of(e[:, :2]) + r[:, :2]).reshape(-1) for e, r in zip(eids, ranks)]
    blk_start = jnp.arange(n_blk, dtype=jnp.int32) * MOE_ROWS
    blk_e = jnp.minimum(jnp.sum((ends[None, :] <= blk_start[:, None]).astype(jnp.int32), axis=1), N_EXPERTS - 1)
    n_used = (ends[-1] // MOE_ROWS).astype(jnp.int32).reshape(1)
    return dests, blk_e, n_used


def _block_diag(w):
    g, r, c = w.shape
    eye = jnp.eye(g, dtype=w.dtype)
    return (eye[:, None, :, None] * w[:, :, None, :]).reshape(g * r, g * c)


def _s5_discretise(log_dt, a_re, a_im, b_re, b_im):
    step = jnp.exp(log_dt)[:, None]
    mag = jnp.exp(a_re * step)
    ang = a_im * step
    abr = mag * jnp.cos(ang)
    abi = mag * jnp.sin(ang)
    den = a_re * a_re + a_im * a_im
    zr = ((abr - 1.0) * a_re + abi * a_im) / den
    zi = (abi * a_re - (abr - 1.0) * a_im) / den
    bbr = zr[..., None] * b_re - zi[..., None] * b_im
    bbi = zr[..., None] * b_im + zi[..., None] * b_re
    return abr, abi, bbr, bbi


def _prep_layer(P, l):
    row = lambda a: a.reshape(1, -1)
    abr, abi, bbr, bbi = _s5_discretise(P['s5_log_dt'][l], P['s5_a_re'][l], P['s5_a_im'][l],
                                        P['s5_b_re'][l], P['s5_b_im'][l])
    on = P['out_norm'][l]
    bones = _block_diag(jnp.full((ATT_HEADS, HEAD_DIM, HEAD_DIM), 1.0 / HEAD_DIM, F32)).astype(jnp.bfloat16)
    return {
        'norm_mix': row(P['norm_mix'][l]), 'norm_ffn': row(P['norm_ffn'][l]),
        'w_in': P['w_in'][l].astype(MXU_DT), 'w_out': P['w_out'][l].astype(MXU_DT),
        'bones': bones,
        'qg': row(jnp.tile(P['q_norm'][l], ATT_HEADS)), 'kg': row(jnp.tile(P['k_norm'][l], ATT_HEADS)),
        'lru_conv_w': P['lru_conv_w'][l], 'lru_conv_b': row(P['lru_conv_b'][l]),
        'lru_wa_blk': _block_diag(P['lru_wa'][l]).astype(MXU_DT), 'lru_ba': row(P['lru_ba'][l]),
        'lru_wx_blk': _block_diag(P['lru_wx'][l]).astype(MXU_DT), 'lru_bx': row(P['lru_bx'][l]),
        'lru_lambda': row(P['lru_lambda'][l]), 'sconv_w': P['sconv_w'][l],
        'on_a': row(on[0:W_GRP]), 'on_b': row(on[W_GRP:2 * W_GRP]),
        'on_c': row(on[2 * W_GRP:3 * W_GRP]), 'on_d': row(on[3 * W_GRP:]),
        's5_bb': jnp.concatenate([_block_diag(bbr.transpose(0, 2, 1)), _block_diag(bbi.transpose(0, 2, 1))],
                                 axis=1).astype(MXU_DT),
        's5_cre': _block_diag(P['s5_c_re'][l].transpose(0, 2, 1)).astype(MXU_DT),
        's5_cim': _block_diag(P['s5_c_im'][l].transpose(0, 2, 1)).astype(MXU_DT),
        's5_abr': row(abr), 's5_abi': row(abi), 's5_d': row(P['s5_d'][l]),
        's5_glu_w': P['s5_glu_w'][l].astype(MXU_DT), 's5_glu_b': row(P['s5_glu_b'][l]),
        'router_w': jnp.zeros((D_MODEL, ROUTER_LANES), F32)
                       .at[:, :N_GROUPS].set(P['router_g_w'][l])
                       .at[:, N_GROUPS:N_GROUPS + N_EXPERTS].set(P['router_e_w'][l]).astype(MXU_DT),
        'router_b': jnp.zeros((1, ROUTER_LANES), F32)
                       .at[0, :N_GROUPS].set(P['router_g_b'][l])
                       .at[0, N_GROUPS:N_GROUPS + N_EXPERTS].set(P['router_e_b'][l]),
    }


TOKEN_TILE = 512
SEQ_TILE = 256

_PARAM_NAMES = ('rel_bias', 'mod_w', 'mod_b', 'norm_mix', 'norm_ffn', 'w_in', 'lru_conv_w', 'lru_conv_b',
                'lru_wa', 'lru_ba', 'lru_wx', 'lru_bx', 'lru_lambda', 'sconv_w', 's5_log_dt', 's5_a_re',
                's5_a_im', 's5_b_re', 's5_b_im', 's5_c_re', 's5_c_im', 's5_d', 's5_glu_w', 's5_glu_b',
                'q_norm', 'k_norm', 'out_norm', 'w_out', 'router_g_w', 'router_g_b', 'router_e_w',
                'router_e_b', 'moe_w_gate', 'moe_w_up', 'moe_w_down')


def _mixers(x, mod, lp, st, attn_fn, *, batch, seq, tm, tl, rows_per_mod, dils=(), kv_window=None):
    res = _in_proj(x, mod, lp['norm_mix'], lp['w_in'], lp['bones'], lp['qg'], lp['kg'],
                   tm=tm, rows_per_mod=rows_per_mod, dils=dils, seq=seq, kv_window=kv_window)
    zm, q, kf, vf, kb, vb = res[:6]
    zm3 = zm.reshape(batch, seq, 6 * W_GRP)
    oa, ob, lru_h, lru_conv, sconv = _mixer_ab(zm3, lp, st['lru_h'], st['lru_conv'], st['sconv'], tl=tl)
    oc, s5_re, s5_im = _s5_mixer(zm3, lp, st['s5_re'], st['s5_im'], tl=min(2 * tl, seq))
    r3 = lambda t: t.reshape(batch, seq, W_GRP)
    if kv_window is None:
        kf, vf = r3(kf), r3(vf)
        win = lambda t: t.reshape(batch, seq, ATT_HEADS, HEAD_DIM)
    else:
        win = lambda t: jnp.transpose(t.reshape(batch, ATT_HEADS, HEAD_DIM, kv_window), (0, 3, 1, 2))
    od = attn_fn(r3(q), kf, vf, r3(kb), r3(vb), res[6:])
    flat = lambda t: t.reshape(batch * seq, W_GRP)
    new_st = {'lru_h': lru_h[:, 0], 'lru_conv': lru_conv, 'sconv': sconv,
              's5_re': s5_re.reshape(batch, S5_GROUPS, S5_STATE), 's5_im': s5_im.reshape(batch, S5_GROUPS, S5_STATE),
              'win_k': win(kf), 'win_v': win(vf)}
    return [flat(oa), flat(ob), flat(oc), flat(od)], new_st


def kernel(x_prompt, x_sample, c_prompt, c_sample, state_lru_h, state_lru_conv, state_sconv, state_s5_re, state_s5_im, cache_win_k, cache_win_v, rel_bias, mod_w, mod_b, norm_mix, norm_ffn, w_in, lru_conv_w, lru_conv_b, lru_wa, lru_ba, lru_wx, lru_bx, lru_lambda, sconv_w, s5_log_dt, s5_a_re, s5_a_im, s5_b_re, s5_b_im, s5_c_re, s5_c_im, s5_d, s5_glu_w, s5_glu_b, q_norm, k_norm, out_norm, w_out, router_g_w, router_g_b, router_e_w, router_e_b, moe_w_gate, moe_w_up, moe_w_down):
    P = dict(zip(_PARAM_NAMES, (rel_bias, mod_w, mod_b, norm_mix, norm_ffn, w_in, lru_conv_w, lru_conv_b,
                                lru_wa, lru_ba, lru_wx, lru_bx, lru_lambda, sconv_w, s5_log_dt, s5_a_re,
                                s5_a_im, s5_b_re, s5_b_im, s5_c_re, s5_c_im, s5_d, s5_glu_w, s5_glu_b,
                                q_norm, k_norm, out_norm, w_out, router_g_w, router_g_b, router_e_w,
                                router_e_b, moe_w_gate, moe_w_up, moe_w_down)))
    bp, seq, d = x_prompt.shape
    bs, dec, _ = x_sample.shape
    depth = mod_w.shape[0]
    tp, ts = bp * seq, bs * dec
    wb = cache_win_k.shape[2]
    wp = min(PATTERNS[-1][0], seq)
    tm_p = min(TOKEN_TILE, tp)
    tm_s = min(TOKEN_TILE, ts)
    tl_p = min(SEQ_TILE, seq)

    nc = -(-(bp + bs) // 8) * 8
    c_all = jnp.zeros((nc, d), F32).at[:bp].set(c_prompt).at[bp:bp + bs].set(c_sample)
    mod_all = _modulation(c_all, mod_w, mod_b)
    bias_p = _bias_table(rel_bias, _prompt_bias_codes()).reshape(len(PATTERNS), 2, ATT_HEADS * Q_BLK, 2 * Q_BLK)
    dils_p = tuple(dil for _, dil in PATTERNS if dil > 1)
    (lw_codes, lw_logm), (lwn_codes, lwn_logm) = _sample_bias_codes(dec, wb)
    lw_s = (_bias_table(rel_bias, lw_codes) + lw_logm).reshape(ATT_HEADS, dec, wb)
    lwn_s = (_bias_table(rel_bias, lwn_codes) + lwn_logm).reshape(ATT_HEADS, dec, dec)
    tri = jnp.asarray(np.tril(np.ones((TOKEN_TILE, TOKEN_TILE), np.float32), -1), jnp.bfloat16)
    n_blk = (2 * (tp + ts)) // MOE_ROWS + N_EXPERTS

    zero_st = {'lru_h': jnp.zeros((bp, 1, W_GRP), F32), 'lru_conv': jnp.zeros((bp, LRU_CONV - 1, W_GRP), F32),
               'sconv': jnp.zeros((bp, SCONV_W - 1, W_GRP), F32),
               's5_re': jnp.zeros((bp, 1, S5_N), F32), 's5_im': jnp.zeros((bp, 1, S5_N), F32)}
    names = ('lru_h', 'lru_conv', 'sconv', 's5_re', 's5_im', 'win_k', 'win_v')
    acc_p = {n: [] for n in names}
    acc_s = {n: [] for n in names}
    xp = x_prompt.reshape(tp, d)
    xs = x_sample.reshape(ts, d)
    for l in range(depth):
        lp = _prep_layer(P, l)
        lp['tri'] = tri
        m6 = mod_all[l].reshape(nc, 6, d).transpose(1, 0, 2)
        mod_p = m6[:, :bp].reshape(6, bp, 1, d)
        mod_s = jnp.repeat(m6[:, bp:bp + bs], dec, axis=1)
        attn_p = lambda q, kf, vf, kb, vb, ex: _attn_prompt(
            [(q[:, None], kb[:, None], vb[:, None])] + [tuple(ex[3 * t:3 * t + 3]) for t in range(len(dils_p))],
            bias_p, lp['on_d'], tm=tm_p)
        pieces_p, st_p = _mixers(xp, mod_p, lp, zero_st, attn_p, batch=bp, seq=seq, tm=tm_p, tl=tl_p,
                                 rows_per_mod=seq, dils=dils_p, kv_window=wp)
        cnt0 = jnp.zeros((1, ROUTER_LANES), F32)
        x1p, h2p, eid_p, gate_p, rank_p, cnt = _post_mix(xp, pieces_p, mod_p, lp, cnt0, tm=tm_p, rows_per_mod=seq)
        samp_st = {'lru_h': state_lru_h[l][:, None], 'lru_conv': state_lru_conv[l], 'sconv': state_sconv[l],
                   's5_re': state_s5_re[l].reshape(bs, 1, S5_N), 's5_im': state_s5_im[l].reshape(bs, 1, S5_N)}
        attn_s = lambda q, kf, vf, kb, vb, ex: _attn_sample(
            q, kf, vf, cache_win_k, cache_win_v, l, lw_s, lwn_s, lp['on_d'])
        pieces_s, st_s = _mixers(xs, mod_s, lp, samp_st, attn_s, batch=bs, seq=dec, tm=tm_s, tl=dec,
                                 rows_per_mod=dec)
        x1s, h2s, eid_s, gate_s, rank_s, cnt = _post_mix(xs, pieces_s, mod_s, lp, cnt, tm=tm_s, rows_per_mod=dec)
        dests, blk_e, n_used = _routing_tables(cnt, [eid_p, eid_s], [rank_p, rank_s], n_blk)
        slots = jnp.zeros((n_blk * MOE_ROWS * ROW_TILE, LANES), ROW_DT)
        slots = _dispatch(h2p, dests[0], slots, tm=tm_p)
        slots = _dispatch(h2s, dests[1], slots, tm=tm_s)
        ys = _experts(slots, blk_e, n_used, moe_w_gate, moe_w_up, moe_w_down, l)
        xp = _combine(ys, dests[0], x1p, gate_p, mod_p, tm=tm_p, rows_per_mod=seq)
        xs = _combine(ys, dests[1], x1s, gate_s, mod_s, tm=tm_s, rows_per_mod=dec)
        for n in names:
            acc_p[n].append(st_p[n])
            acc_s[n].append(st_s[n])
    new_p = {n: jnp.stack(acc_p[n], axis=0) for n in names}
    new_s = {n: jnp.stack(acc_s[n], axis=0) for n in names}
    return (xp.reshape(bp, seq, d), xs.reshape(bs, dec, d),
            new_p['lru_h'], new_p['lru_conv'], new_p['sconv'], new_p['s5_re'], new_p['s5_im'],
            new_p['win_k'], new_p['win_v'],
            new_s['lru_h'], new_s['lru_conv'], new_s['sconv'], new_s['s5_re'], new_s['s5_im'],
            new_s['win_k'], new_s['win_v'])
```
